```python
import jax
import jax.numpy as jnp
from jax import lax
import numpy as np

D_MODEL = 1024
BATCH = 2
SEQ = 8192
DEPTH = 2
DEC_BATCH = 8
DEC_SEQ = 32
PAST_LEN = 4096

CHUNK = 64
N_A_LAYERS = DEPTH // 2
N_B_LAYERS = DEPTH - N_A_LAYERS
N_DENSE = (DEPTH + 1) // 2
N_MOE = DEPTH // 2
HEAD_A = 64
H_A = D_MODEL // HEAD_A
LORA_W = 64
LORA_A = 64
LORA_G = 128
GN_EPS = 64e-5
HEAD_B = 64
H_B = D_MODEL // HEAD_B
KV_HEADS = 4
GROUP = H_B // KV_HEADS
WINDOW = 128
N_PREV_CHUNKS = WINDOW // CHUNK
BAND = WINDOW + CHUNK
ROPE_DIM = HEAD_B // 4
ROPE_THETA = 500000.0
ATTN_SCALE = HEAD_B ** -0.5
D_FF = (D_MODEL * 7) // 2
N_EXPERTS = 8
TOP_K = 2
LN_EPS = 1e-5
ALPHA = (2.0 * DEPTH) ** 0.25
BETA = (8.0 * DEPTH) ** -0.25

kernel_name = 'yoco_rwkv7_swa_sink_streaming_step'


def layer_norm(x, g, b):
    xf = x.astype(jnp.float32)
    mu = jnp.mean(xf, axis=-1, keepdims=True)
    var = jnp.mean(jnp.square(xf - mu), axis=-1, keepdims=True)
    return ((xf - mu) * lax.rsqrt(var + LN_EPS) * g + b).astype(x.dtype)


def swiglu(x, w_gu, w_down):
    gate, up = jnp.split(x @ w_gu, 2, axis=-1)
    return (jax.nn.silu(gate) * up) @ w_down


def moe_swiglu(x, router, w_gu, w_down):
    logits = (x @ router).astype(jnp.float32)
    top_v, top_i = lax.top_k(logits, TOP_K)
    gates = jax.nn.softmax(top_v, axis=-1)
    comb = jnp.sum(jax.nn.one_hot(top_i, N_EXPERTS, dtype=jnp.float32) * gates[..., None], axis=-2)
    comb = comb.astype(x.dtype)
    y = jnp.zeros_like(x)
    for e in range(N_EXPERTS):
        y = y + comb[..., e:e + 1] * swiglu(x, w_gu[e], w_down[e])
    return y


def partial_rope(x, pos):
    inv_freq = ROPE_THETA ** (-jnp.arange(0, ROPE_DIM, 2, dtype=jnp.float32) / ROPE_DIM)
    ang = pos.astype(jnp.float32)[:, None] * inv_freq[None, :]
    cos = jnp.cos(ang)[:, None, :]
    sin = jnp.sin(ang)[:, None, :]
    xr = x[..., :ROPE_DIM].astype(jnp.float32)
    x1, x2 = xr[..., :ROPE_DIM // 2], xr[..., ROPE_DIM // 2:]
    rot = jnp.concatenate([x1 * cos - x2 * sin, x2 * cos + x1 * sin], axis=-1)
    return jnp.concatenate([rot.astype(x.dtype), x[..., ROPE_DIM:]], axis=-1)


def rwkv7_time_mix(x, shift_row, wkv0, mu, w_rkv, w0, w1, w2, a0, a1, a2, g1, g2,
                   k_k, k_a, r_k, lnx_g, lnx_b, w_o):
    bn, t, _ = x.shape
    x_prev = jnp.concatenate([shift_row[:, None, :].astype(x.dtype), x[:, :-1]], axis=1)
    xs = x[None] + (x_prev - x)[None] * mu[:, None, None, :]
    r, k, v = jnp.einsum('sbtd,sde->sbte', xs[:3], w_rkv)
    xw, xa, xg = xs[3], xs[4], xs[5]
    w_log = -jax.nn.softplus(-(w0 + jnp.tanh(xw @ w1) @ w2).astype(jnp.float32)) - 0.5
    decay = jnp.exp(-jnp.exp(w_log))
    a = jax.nn.sigmoid((a0 + (xa @ a1) @ a2).astype(jnp.float32))
    g = jax.nn.sigmoid(xg @ g1) @ g2

    def heads(z):
        return z.reshape(bn, t, H_A, HEAD_A).astype(jnp.float32)

    kk = heads(k * k_k)
    kk = kk / jnp.maximum(jnp.sqrt(jnp.sum(kk * kk, axis=-1, keepdims=True)), 1e-12)
    a_h = heads(a)
    k_h = heads(k * (1.0 + (a - 1.0) * k_a))
    r_h = heads(r)
    v_h = heads(v)
    seq = (r_h, heads(decay), k_h, v_h, -kk, kk * a_h)
    seq = tuple(jnp.moveaxis(z, 1, 0) for z in seq)

    def step(s, inp):
        r_t, w_t, k_t, v_t, kn_t, ka_t = inp
        s = (s * w_t[:, :, None, :]
             + jnp.einsum('bhij,bhj->bhi', s, kn_t)[..., None] * ka_t[:, :, None, :]
             + v_t[..., None] * k_t[:, :, None, :])
        return s, jnp.einsum('bhij,bhj->bhi', s, r_t)

    s_final, o = lax.scan(step, wkv0.astype(jnp.float32), seq)
    o = jnp.moveaxis(o, 0, 1)
    mean = jnp.mean(o, axis=-1, keepdims=True)
    var = jnp.mean(jnp.square(o - mean), axis=-1, keepdims=True)
    on = ((o - mean) * lax.rsqrt(var + GN_EPS)).reshape(bn, t, D_MODEL) * lnx_g + lnx_b
    bonus = jnp.sum(r_h * k_h * r_k, axis=-1, keepdims=True) * v_h
    out = (on + bonus.reshape(bn, t, D_MODEL)) * g
    return out.astype(x.dtype) @ w_o, x[:, -1], s_final


def shared_kv(x, pos, kv_w, k_cache, v_cache):
    bn, t, _ = x.shape
    kv = (x @ kv_w).reshape(bn, t, 2, KV_HEADS, HEAD_B)
    k = partial_rope(kv[:, :, 0], pos)
    v = kv[:, :, 1]
    if k_cache is not None:
        k = jnp.concatenate([k_cache.astype(k.dtype), k], axis=1)
        v = jnp.concatenate([v_cache.astype(v.dtype), v], axis=1)
    return k, v, k[:, -WINDOW:], v[:, -WINDOW:]


def sink_attend(q, k, v, mask, sinks):
    s = jnp.einsum('bcqkgd,bcskd->bckgqs', q, k, preferred_element_type=jnp.float32) * ATTN_SCALE
    s = jnp.where(mask[None, :, None, None, None, :], s, -jnp.inf)
    sk = sinks.astype(jnp.float32).reshape(1, 1, KV_HEADS, GROUP, 1, 1)
    m = jnp.maximum(jnp.max(s, axis=-1, keepdims=True), sk)
    p = jnp.exp(s - m)
    p = p / (jnp.sum(p, axis=-1, keepdims=True) + jnp.exp(sk - m))
    return jnp.einsum('bckgqs,bcskd->bcqkgd', p.astype(v.dtype), v)


def swa_sink_attention(x, pos, k, v, banded, w_q, sinks, w_o):
    bn, t, _ = x.shape
    q = partial_rope((x @ w_q).reshape(bn, t, H_B, HEAD_B), pos)
    if banded:
        n_c = t // CHUNK
        qc = q.reshape(bn, n_c, CHUNK, KV_HEADS, GROUP, HEAD_B)
        pad = ((0, 0), (WINDOW, 0), (0, 0), (0, 0))
        kp = jnp.pad(k, pad).reshape(bn, n_c + N_PREV_CHUNKS, CHUNK, KV_HEADS, HEAD_B)
        vp = jnp.pad(v, pad).reshape(bn, n_c + N_PREV_CHUNKS, CHUNK, KV_HEADS, HEAD_B)
        kb = jnp.concatenate([kp[:, j:j + n_c] for j in range(N_PREV_CHUNKS + 1)], axis=2)
        vb = jnp.concatenate([vp[:, j:j + n_c] for j in range(N_PREV_CHUNKS + 1)], axis=2)
        key_pos = jnp.arange(n_c)[:, None] * CHUNK - WINDOW + jnp.arange(BAND)[None, :]
        mask = key_pos >= 0
    else:
        qc = q.reshape(bn, 1, t, KV_HEADS, GROUP, HEAD_B)
        kb = k[:, None]
        vb = v[:, None]
        mask = jnp.ones((1, k.shape[1]), dtype=bool)
    o = sink_attend(qc, kb, vb, mask, sinks)
    return o.reshape(bn, t, H_B * HEAD_B) @ w_o


def trunk(x, shift_in, wkv_in, k_cache, v_cache, pos0, P):
    bn, t, _ = x.shape
    pos = pos0 + jnp.arange(t, dtype=jnp.int32)
    shifts = []
    wkvs = []
    k_sh = None
    v_sh = None
    k_new = None
    v_new = None
    for layer in range(DEPTH):
        if layer < N_A_LAYERS:
            i = layer
            h, sh, st = rwkv7_time_mix(
                x, shift_in[i], wkv_in[i], P['a_mu'][i], P['a_w_rkv'][i], P['a_w0'][i],
                P['a_w1'][i], P['a_w2'][i], P['a_a0'][i], P['a_a1'][i], P['a_a2'][i],
                P['a_g1'][i], P['a_g2'][i], P['a_k_k'][i], P['a_k_a'][i], P['a_r_k'][i],
                P['a_lnx_g'][i], P['a_lnx_b'][i], P['a_w_o'][i])
            shifts.append(sh)
            wkvs.append(st)
        else:
            j = layer - N_A_LAYERS
            if k_sh is None:
                k_sh, v_sh, k_new, v_new = shared_kv(x, pos, P['kv_w'], k_cache, v_cache)
            h = swa_sink_attention(x, pos, k_sh, v_sh, k_cache is None,
                                   P['b_w_q'][j], P['b_sinks'][j], P['b_w_o'][j])
        x = layer_norm(ALPHA * x + h, P['ln_g'][layer, 0], P['ln_b'][layer, 0])
        if layer % 2 == 0:
            f = swiglu(x, P['ffn_w_gu'][layer // 2], P['ffn_w_down'][layer // 2])
        else:
            f = moe_swiglu(x, P['moe_router'][layer // 2], P['moe_w_gu'][layer // 2],
                           P['moe_w_down'][layer // 2])
        x = layer_norm(ALPHA * x + f, P['ln_g'][layer, 1], P['ln_b'][layer, 1])
    return x, jnp.stack(shifts), jnp.stack(wkvs), k_new, v_new


def setup_inputs(seed: int = 0) -> dict:
    key = jax.random.key(seed)
    keys = list(jax.random.split(key, 48))

    def nrm(shape, scale):
        return scale * jax.random.normal(keys.pop(), shape, jnp.float32)

    def uni(shape, lo, hi):
        return jax.random.uniform(keys.pop(), shape, jnp.float32, lo, hi)

    D = D_MODEL
    HB = H_B * HEAD_B
    return {
        'x_prompt': nrm((BATCH, SEQ, D), 1.0),
        'x_sample': nrm((DEC_BATCH, DEC_SEQ, D), 1.0),
        'cache_shift_a': nrm((N_A_LAYERS, DEC_BATCH, D), 1.0),
        'state_wkv_a': nrm((N_A_LAYERS, DEC_BATCH, H_A, HEAD_A, HEAD_A), 0.5),
        'cache_k_b': nrm((DEC_BATCH, WINDOW, KV_HEADS, HEAD_B), 1.0),
        'cache_v_b': nrm((DEC_BATCH, WINDOW, KV_HEADS, HEAD_B), 1.0),
        'a_mu': uni((N_A_LAYERS, 6, D), 0.0, 1.0),
        'a_w_rkv': nrm((N_A_LAYERS, 3, D, D), D ** -0.5),
        'a_w0': uni((N_A_LAYERS, D), -6.0, -1.0),
        'a_w1': nrm((N_A_LAYERS, D, LORA_W), D ** -0.5),
        'a_w2': nrm((N_A_LAYERS, LORA_W, D), 0.1 * LORA_W ** -0.5),
        'a_a0': nrm((N_A_LAYERS, D), 0.5),
        'a_a1': nrm((N_A_LAYERS, D, LORA_A), D ** -0.5),
        'a_a2': nrm((N_A_LAYERS, LORA_A, D), 0.5 * LORA_A ** -0.5),
        'a_g1': nrm((N_A_LAYERS, D, LORA_G), D ** -0.5),
        'a_g2': nrm((N_A_LAYERS, LORA_G, D), LORA_G ** -0.5),
        'a_k_k': 0.85 + nrm((N_A_LAYERS, D), 0.05),
        'a_k_a': 1.0 + nrm((N_A_LAYERS, D), 0.05),
        'a_r_k': nrm((N_A_LAYERS, H_A, HEAD_A), 0.1),
        'a_lnx_g': 1.0 + nrm((N_A_LAYERS, D), 0.05),
        'a_lnx_b': nrm((N_A_LAYERS, D), 0.02),
        'a_w_o': nrm((N_A_LAYERS, D, D), BETA * D ** -0.5),
        'kv_w': nrm((D, 2 * KV_HEADS * HEAD_B), D ** -0.5),
        'b_w_q': nrm((N_B_LAYERS, D, HB), D ** -0.5),
        'b_sinks': nrm((N_B_LAYERS, H_B), 0.5),
        'b_w_o': nrm((N_B_LAYERS, HB, D), BETA * HB ** -0.5),
        'ln_g': 1.0 + nrm((DEPTH, 2, D), 0.05),
        'ln_b': nrm((DEPTH, 2, D), 0.02),
        'ffn_w_gu': nrm((N_DENSE, D, 2 * D_FF), D ** -0.5),
        'ffn_w_down': nrm((N_DENSE, D_FF, D), BETA * D_FF ** -0.5),
        'moe_router': nrm((N_MOE, D, N_EXPERTS), D ** -0.5),
        'moe_w_gu': nrm((N_MOE, N_EXPERTS, D, 2 * D_FF), D ** -0.5),
        'moe_w_down': nrm((N_MOE, N_EXPERTS, D_FF, D), BETA * D_FF ** -0.5),
    }


def reference(x_prompt, x_sample, cache_shift_a, state_wkv_a, cache_k_b, cache_v_b,
              a_mu, a_w_rkv, a_w0, a_w1, a_w2, a_a0, a_a1, a_a2, a_g1, a_g2,
              a_k_k, a_k_a, a_r_k, a_lnx_g, a_lnx_b, a_w_o,
              kv_w, b_w_q, b_sinks, b_w_o, ln_g, ln_b,
              ffn_w_gu, ffn_w_down, moe_router, moe_w_gu, moe_w_down):
    P = {
        'a_mu': a_mu, 'a_w_rkv': a_w_rkv, 'a_w0': a_w0, 'a_w1': a_w1, 'a_w2': a_w2,
        'a_a0': a_a0, 'a_a1': a_a1, 'a_a2': a_a2, 'a_g1': a_g1, 'a_g2': a_g2,
        'a_k_k': a_k_k, 'a_k_a': a_k_a, 'a_r_k': a_r_k, 'a_lnx_g': a_lnx_g,
        'a_lnx_b': a_lnx_b, 'a_w_o': a_w_o, 'kv_w': kv_w, 'b_w_q': b_w_q,
        'b_sinks': b_sinks, 'b_w_o': b_w_o, 'ln_g': ln_g, 'ln_b': ln_b,
        'ffn_w_gu': ffn_w_gu, 'ffn_w_down': ffn_w_down, 'moe_router': moe_router,
        'moe_w_gu': moe_w_gu, 'moe_w_down': moe_w_down,
    }
    bp = x_prompt.shape[0]
    zero_shift = jnp.zeros((N_A_LAYERS, bp, D_MODEL), x_prompt.dtype)
    zero_wkv = jnp.zeros((N_A_LAYERS, bp, H_A, HEAD_A, HEAD_A), jnp.float32)
    y_prompt, p_shift, p_wkv, p_k, p_v = trunk(x_prompt, zero_shift, zero_wkv, None, None, 0, P)
    y_sample, s_shift, s_wkv, s_k, s_v = trunk(x_sample, cache_shift_a, state_wkv_a,
                                               cache_k_b, cache_v_b, PAST_LEN, P)
    return (y_prompt, y_sample, p_shift, p_wkv, p_k, p_v, s_shift, s_wkv, s_k, s_v)
```

```python
import functools

import jax
import jax.numpy as jnp
from jax import lax
from jax.experimental import pallas as pl
from jax.experimental.pallas import tpu as pltpu

F32 = jnp.float32
BF16 = jnp.bfloat16

DEPTH = 2
HEAD_A = 64
HEAD_B = 64
KV_HEADS = 4
CHUNK = 64
WINDOW = 128
PAST_LEN = 4096
ROPE_DIM = HEAD_B // 4
ROPE_THETA = 500000.0
ATTN_SCALE = HEAD_B ** -0.5
N_EXPERTS = 8
GN_EPS = 64e-5
LN_EPS = 1e-5
ALPHA = (2.0 * DEPTH) ** 0.25

LANES = 128
SUBLANES = 8
VMEM_LIMIT_BYTES = 56 * 1024 * 1024


def _dot(a, b):
    return jnp.dot(a.astype(BF16), b.astype(BF16), preferred_element_type=F32)


def _dot_nt(a, b):
    return lax.dot_general(a.astype(BF16), b.astype(BF16), (((1,), (1,)), ((), ())),
                           preferred_element_type=F32)


def _dot_tn(a, b):
    return lax.dot_general(a.astype(BF16), b.astype(BF16), (((0,), (0,)), ((), ())),
                           preferred_element_type=F32)


def _split(x, n):
    parts = []
    rem = x
    for i in range(n):
        p = rem.astype(BF16)
        parts.append(p)
        if i + 1 < n:
            rem = rem - p.astype(F32)
    return parts


def _dot_exact_rhs(a, b_bf16, n):
    acc = None
    for p in _split(a, n):
        t = jnp.dot(p, b_bf16, preferred_element_type=F32)
        acc = t if acc is None else acc + t
    return acc


def _dot_exact_lhs(a_bf16, b, n):
    acc = None
    for p in _split(b, n):
        t = jnp.dot(a_bf16, p, preferred_element_type=F32)
        acc = t if acc is None else acc + t
    return acc


def _sigmoid(z):
    return 1.0 / (1.0 + jnp.exp(-z))


def _layer_norm(z, g, b):
    mu = jnp.mean(z, axis=-1, keepdims=True)
    zc = z - mu
    var = jnp.mean(zc * zc, axis=-1, keepdims=True)
    return zc * lax.rsqrt(var + LN_EPS) * g + b


def _const_spec(shape):
    nd = len(shape)
    return pl.BlockSpec(shape, lambda *_: (0,) * nd)


def _params(sem):
    return pltpu.CompilerParams(dimension_semantics=sem, vmem_limit_bytes=VMEM_LIMIT_BYTES)


def _rwkv_pre_kernel(x_ref, bnd_ref, mu_ref, vec_ref, wr_ref, wk_ref, wv_ref, w1_ref, w2_ref,
                     a1_ref, a2_ref, g1_ref, g2_ref, ones_ref,
                     r_out, ld_out, k_out, v_out, kk_out, ka_out, g_out, bonus_out):
    x = x_ref[...]
    rows = lax.broadcasted_iota(jnp.int32, x.shape, 0)
    xp = jnp.where(rows == 0, bnd_ref[0, 0:1, :], pltpu.roll(x, 1, 0))
    dx = xp - x

    def mix(s):
        return x + dx * mu_ref[s:s + 1, :]

    w0 = vec_ref[0:1, :]
    a0 = vec_ref[1:2, :]
    k_k = vec_ref[2:3, :]
    k_a = vec_ref[3:4, :]
    r_k = vec_ref[4:5, :]
    ones = ones_ref[...]

    r = _dot(mix(0), wr_ref[...])
    k = _dot(mix(1), wk_ref[...])
    v = _dot(mix(2), wv_ref[...])
    zw = -(w0 + _dot(jnp.tanh(_dot(mix(3), w1_ref[...])), w2_ref[...]))
    softplus = jnp.maximum(zw, 0.0) + jnp.log(1.0 + jnp.exp(-jnp.abs(zw)))
    ld = -jnp.exp(-softplus - 0.5)
    a = _sigmoid(a0 + _dot(_dot(mix(4), a1_ref[...]), a2_ref[...]))
    g = _dot(_sigmoid(_dot(mix(5), g1_ref[...])), g2_ref[...])

    kk = k * k_k
    ss = _dot_exact_rhs(kk * kk, ones, 2)
    kk = kk / jnp.maximum(jnp.sqrt(ss), 1e-12)
    k_h = k * (1.0 + (a - 1.0) * k_a)
    bonus = _dot_exact_rhs(r * k_h * r_k, ones, 2) * v

    r_out[...] = r
    ld_out[...] = ld
    k_out[...] = k_h
    v_out[...] = v
    kk_out[...] = kk
    ka_out[...] = kk * a
    g_out[...] = g
    bonus_out[...] = bonus


def _rwkv_pre(x, bnd, mu8, vec8, wr, wk, wv, w1, w2, a1, a2, g1, g2, ones_bd, tm):
    n, d = x.shape
    row = pl.BlockSpec((tm, d), lambda i: (i, 0))
    ins = [row, pl.BlockSpec((1, SUBLANES, d), lambda i: (i, 0, 0))]
    ins += [_const_spec(a.shape) for a in (mu8, vec8, wr, wk, wv, w1, w2, a1, a2, g1, g2, ones_bd)]
    return pl.pallas_call(
        _rwkv_pre_kernel,
        grid=(n // tm,),
        in_specs=ins,
        out_specs=[row] * 8,
        out_shape=[jax.ShapeDtypeStruct((n, d), F32)] * 8,
        compiler_params=_params(("parallel",)),
        name="rwkv_pre",
    )(x, bnd, mu8, vec8, wr, wk, wv, w1, w2, a1, a2, g1, g2, ones_bd)


def _wkv_scan_kernel(r_ref, ld_ref, k_ref, v_ref, kk_ref, ka_ref, s0_ref, o_ref, st_ref, s_scr,
                     *, chunk, n_chunks):
    c = chunk
    pw = 2 * HEAD_A
    t_idx = pl.program_id(2)

    @pl.when(t_idx == 0)
    def _():
        s_scr[...] = s0_ref[...]

    ri = lax.broadcasted_iota(jnp.int32, (c, c), 0)
    ci = lax.broadcasted_iota(jnp.int32, (c, c), 1)
    tri_incl = ri >= ci
    cum_mat = jnp.where(tri_incl, 1.0, 0.0).astype(BF16)
    ri2 = lax.broadcasted_iota(jnp.int32, (c, 2 * c), 0)
    ci2 = lax.broadcasted_iota(jnp.int32, (c, 2 * c), 1) & (c - 1)
    strict2 = ri2 > ci2
    incl2 = ri2 >= ci2
    lane1 = lax.broadcasted_iota(jnp.int32, (1, pw), 1)
    head_a1 = lane1 < HEAD_A
    lane2 = lax.broadcasted_iota(jnp.int32, (1, 2 * pw), 1) & (pw - 1)
    head_a2 = lane2 < HEAD_A
    rs = lax.broadcasted_iota(jnp.int32, (pw, pw), 0)
    cs = lax.broadcasted_iota(jnp.int32, (pw, pw), 1)
    same_head = (rs < HEAD_A) == (cs < HEAD_A)
    eye = rs == cs
    zeros_cv = jnp.zeros((c, pw), F32)

    def body(ch, carry):
        sl = pl.ds(pl.multiple_of(ch * c, c), c)
        ld = ld_ref[sl, :]
        cw = _dot_exact_lhs(cum_mat, ld, 3)
        w_in = jnp.exp(cw)
        w_ex = jnp.exp(cw - ld)
        w_inv = jnp.exp(-cw)
        w_last = w_in[c - 1:c, :]
        kk = kk_ref[sl, :]
        knt = -(kk * w_ex)
        kat = ka_ref[sl, :] * w_inv
        kt = k_ref[sl, :] * w_inv
        rt = r_ref[sl, :] * w_in
        v = v_ref[sl, :]

        lh = jnp.concatenate([knt, rt], axis=0)
        rh = jnp.concatenate([kat, kt], axis=0)
        zv = jnp.concatenate([zeros_cv, v], axis=0)

        def head_mats(is_a):
            a_full = _dot_nt(jnp.where(is_a, lh, 0.0), rh)
            top = jnp.where(strict2, a_full[:c, :], 0.0)
            bot = jnp.where(incl2, a_full[c:, :], 0.0)
            return top, bot

        top_a, bot_a = head_mats(head_a1)
        top_b, bot_b = head_mats(jnp.logical_not(head_a1))
        akv = jnp.where(head_a1, _dot(top_a, zv), _dot(top_b, zv))

        x = jnp.concatenate([knt, akv], axis=1)
        n_a = top_a[:, :c]
        n_b = top_b[:, :c]
        span = 1
        while span < c:
            x = x + jnp.where(head_a2, _dot(n_a, x), _dot(n_b, x))
            span *= 2
            if span < c:
                n_a = _dot(n_a, n_a)
                n_b = _dot(n_b, n_b)
        g = x[:, :pw]
        u0 = x[:, pw:]

        rhs2 = jnp.concatenate([x, jnp.concatenate([zeros_cv, v], axis=1)], axis=0)
        qo = jnp.where(head_a2, _dot(bot_a, rhs2), _dot(bot_b, rhs2))
        q = rt + qo[:, :pw]
        o0 = qo[:, pw:]
        lt = jnp.concatenate([kat * w_last, kt * w_last], axis=0)
        mb = _dot_tn(lt, rhs2)
        m = jnp.where(eye, w_last, 0.0) + jnp.where(same_head, mb[:, :pw], 0.0)
        b = jnp.where(same_head, mb[:, pw:], 0.0)

        s = s_scr[...]
        s_hi, s_lo = _split(s, 2)
        mq = jnp.concatenate([m, q], axis=0)
        mq_hi, mq_lo = _split(mq, 2)
        res = (jnp.dot(mq_hi, s_hi, preferred_element_type=F32)
               + jnp.dot(mq_hi, s_lo, preferred_element_type=F32)
               + jnp.dot(mq_lo, s_hi, preferred_element_type=F32))
        s_scr[...] = res[:pw, :] + b
        o_ref[sl, :] = res[pw:, :] + o0
        return carry

    lax.fori_loop(0, n_chunks, body, 0)

    @pl.when(t_idx == pl.num_programs(2) - 1)
    def _():
        st_ref[...] = s_scr[...]


def _wkv_scan(r, ld, k, v, kk, ka, s0, chunk, t_blk):
    bn, t, d = r.shape
    pw = 2 * HEAD_A
    n_pairs = d // pw
    seq = pl.BlockSpec((None, t_blk, pw), lambda b, p, i: (b, i, p))
    st = pl.BlockSpec((None, None, pw, pw), lambda b, p, i: (b, p, 0, 0))
    kern = functools.partial(_wkv_scan_kernel, chunk=chunk, n_chunks=t_blk // chunk)
    return pl.pallas_call(
        kern,
        grid=(bn, n_pairs, t // t_blk),
        in_specs=[seq] * 6 + [st],
        out_specs=[seq, st],
        out_shape=[jax.ShapeDtypeStruct((bn, t, d), F32),
                   jax.ShapeDtypeStruct((bn, n_pairs, pw, pw), F32)],
        scratch_shapes=[pltpu.VMEM((pw, pw), F32)],
        compiler_params=_params(("parallel", "parallel", "arbitrary")),
        name="wkv_scan",
    )(r, ld, k, v, kk, ka, s0)


def _rwkv_post_kernel(o_ref, bonus_ref, g_ref, x_ref, vec_ref, ones_ref, wo_ref, out_ref):
    o = o_ref[...]
    ones = ones_ref[...]
    inv_n = 1.0 / HEAD_A
    mean = _dot_exact_rhs(o, ones, 2) * inv_n
    oc = o - mean
    var = _dot_exact_rhs(oc * oc, ones, 2) * inv_n
    on = oc * lax.rsqrt(var + GN_EPS) * vec_ref[0:1, :] + vec_ref[1:2, :]
    y = (on + bonus_ref[...]) * g_ref[...]
    h = _dot(y, wo_ref[...])
    out_ref[...] = _layer_norm(ALPHA * x_ref[...] + h, vec_ref[2:3, :], vec_ref[3:4, :])


def _rwkv_post(o, bonus, g, x, vec8, ones_bd, wo, tm):
    n, d = x.shape
    row = pl.BlockSpec((tm, d), lambda i: (i, 0))
    return pl.pallas_call(
        _rwkv_post_kernel,
        grid=(n // tm,),
        in_specs=[row] * 4 + [_const_spec(vec8.shape), _const_spec(ones_bd.shape), _const_spec(wo.shape)],
        out_specs=row,
        out_shape=jax.ShapeDtypeStruct((n, d), F32),
        compiler_params=_params(("parallel",)),
        name="rwkv_post",
    )(o, bonus, g, x, vec8, ones_bd, wo)


def _ffn_kernel(x_ref, comb_ref, wg_ref, wu_ref, wd_ref, ln_ref, out_ref, acc_ref, xb_ref, *, gated):
    e = pl.program_id(1)
    f = pl.program_id(2)
    first = jnp.logical_and(e == 0, f == 0)
    last = jnp.logical_and(e == pl.num_programs(1) - 1, f == pl.num_programs(2) - 1)

    @pl.when(first)
    def _():
        acc_ref[...] = jnp.zeros_like(acc_ref)
        xb_ref[...] = x_ref[...].astype(BF16)

    xb = xb_ref[...]
    gate = jnp.dot(xb, wg_ref[...].astype(BF16), preferred_element_type=F32)
    up = jnp.dot(xb, wu_ref[...].astype(BF16), preferred_element_type=F32)
    h = gate * _sigmoid(gate) * up
    y = jnp.dot(h.astype(BF16), wd_ref[...].astype(BF16), preferred_element_type=F32)
    if gated:
        comb = comb_ref[...]
        lane = lax.broadcasted_iota(jnp.int32, comb.shape, 1)
        c_e = jnp.sum(jnp.where(lane == e, comb, 0.0), axis=-1, keepdims=True)
        y = c_e * y
    acc_ref[...] += y

    @pl.when(last)
    def _():
        out_ref[...] = _layer_norm(ALPHA * x_ref[...] + acc_ref[...], ln_ref[0:1, :], ln_ref[1:2, :])


def _ffn(x, comb, w_gu, w_down, ln8, tm, tf, gated):
    n, d = x.shape
    n_e, _, d_ff2 = w_gu.shape
    d_ff = d_ff2 // 2
    nf = d_ff // tf
    kern = functools.partial(_ffn_kernel, gated=gated)
    return pl.pallas_call(
        kern,
        grid=(n // tm, n_e, nf),
        in_specs=[
            pl.BlockSpec((tm, d), lambda i, e, f: (i, 0)),
            pl.BlockSpec((tm, LANES), lambda i, e, f: (i, 0)),
            pl.BlockSpec((None, d, tf), lambda i, e, f: (e, 0, f)),
            pl.BlockSpec((None, d, tf), lambda i, e, f: (e, 0, nf + f)),
            pl.BlockSpec((None, tf, d), lambda i, e, f: (e, f, 0)),
            _const_spec(ln8.shape),
        ],
        out_specs=pl.BlockSpec((tm, d), lambda i, e, f: (i, 0)),
        out_shape=jax.ShapeDtypeStruct((n, d), F32),
        scratch_shapes=[pltpu.VMEM((tm, d), F32), pltpu.VMEM((tm, d), BF16)],
        compiler_params=_params(("parallel", "arbitrary", "arbitrary")),
        name="moe_ffn" if gated else "dense_ffn",
    )(x, comb, w_gu, w_gu, w_down, ln8)


def _rope(y, cos_t, sin_next, sin_prev):
    n = y.shape[1]
    reps = n // LANES
    tile = lambda t: jnp.concatenate([t] * reps, axis=1) if reps > 1 else t
    half = ROPE_DIM // 2
    return (y * tile(cos_t)
            + pltpu.roll(y, n - half, 1) * tile(sin_next)
            + pltpu.roll(y, half, 1) * tile(sin_prev))


def _kv_kernel(x_ref, w_ref, cos_ref, sn_ref, sp_ref, k_out, v_out):
    kv = _dot(x_ref[...], w_ref[...])
    nk = k_out.shape[1]
    k_out[...] = _rope(kv[:, :nk], cos_ref[...], sn_ref[...], sp_ref[...])
    v_out[...] = kv[:, nk:]


def _q_kernel(x_ref, w_ref, cos_ref, sn_ref, sp_ref, q_out):
    q_out[...] = _rope(_dot(x_ref[...], w_ref[...]), cos_ref[...], sn_ref[...], sp_ref[...])


def _rope_proj(kernel_fn, x, w, tables, tm, out_widths, name):
    n, d = x.shape
    t_tiles = tables[0].shape[0] // tm
    row = pl.BlockSpec((tm, d), lambda i: (i, 0))
    tab = pl.BlockSpec((tm, LANES), lambda i: (i % t_tiles, 0))
    return pl.pallas_call(
        kernel_fn,
        grid=(n // tm,),
        in_specs=[row, _const_spec(w.shape), tab, tab, tab],
        out_specs=[pl.BlockSpec((tm, ow), lambda i: (i, 0)) for ow in out_widths],
        out_shape=[jax.ShapeDtypeStruct((n, ow), F32) for ow in out_widths],
        compiler_params=_params(("parallel",)),
        name=name,
    )(x, w, *tables)


def _attn_kernel(sink_ref, q_ref, kp_ref, kc_ref, vp_ref, vc_ref, o_ref, *, banded):
    tq = q_ref.shape[0]
    n_prev = kp_ref.shape[0]
    tk = n_prev + kc_ref.shape[0]
    q = q_ref[...]
    kband = jnp.concatenate([kp_ref[...], kc_ref[...]], axis=0)
    vband = jnp.concatenate([vp_ref[...], vc_ref[...]], axis=0)
    if banded:
        qi = lax.broadcasted_iota(jnp.int32, (tq, tk), 0)
        kj = lax.broadcasted_iota(jnp.int32, (tq, tk), 1)
        lo = qi - (qi & (CHUNK - 1))
        key_pos = pl.program_id(1) * tq - n_prev + kj
        mask = (kj >= lo) & (kj < lo + WINDOW + CHUNK) & (key_pos >= 0)
    group = (q.shape[1] // HEAD_B) // KV_HEADS
    outs = []
    for kh in range(KV_HEADS):
        kb = kband[:, kh * HEAD_B:(kh + 1) * HEAD_B]
        vb = vband[:, kh * HEAD_B:(kh + 1) * HEAD_B]
        for j in range(group):
            h = kh * group + j
            qh = q[:, h * HEAD_B:(h + 1) * HEAD_B]
            s = _dot_nt(qh, kb) * ATTN_SCALE
            if banded:
                s = jnp.where(mask, s, -jnp.inf)
            sk = sink_ref[h]
            m = jnp.maximum(jnp.max(s, axis=-1, keepdims=True), sk)
            p = jnp.exp(s - m)
            p = p / (jnp.sum(p, axis=-1, keepdims=True) + jnp.exp(sk - m))
            outs.append(_dot(p, vb))
    o_ref[...] = jnp.concatenate(outs, axis=1)


def _attention(q, k_prev_src, k_cur_src, v_prev_src, v_cur_src, sinks, tq, banded):
    bn, t, d = q.shape
    kw = k_cur_src.shape[2]
    if banded:
        ratio = tq // WINDOW
        prev_map = lambda b, i: (b, jnp.maximum(i * ratio - 1, 0), 0)
    else:
        prev_map = lambda b, i: (b, 0, 0)
    prev = pl.BlockSpec((None, WINDOW, kw), prev_map)
    cur = pl.BlockSpec((None, tq, kw), lambda b, i: (b, i, 0))
    kern = functools.partial(_attn_kernel, banded=banded)
    return pl.pallas_call(
        kern,
        grid=(bn, t // tq),
        in_specs=[pl.BlockSpec(memory_space=pltpu.SMEM),
                  pl.BlockSpec((None, tq, d), lambda b, i: (b, i, 0)), prev, cur, prev, cur],
        out_specs=pl.BlockSpec((None, tq, d), lambda b, i: (b, i, 0)),
        out_shape=jax.ShapeDtypeStruct((bn, t, d), F32),
        compiler_params=_params(("parallel", "parallel")),
        name="swa_attn",
    )(sinks, q, k_prev_src, k_cur_src, v_prev_src, v_cur_src)


def _proj_ln_kernel(y_ref, x_ref, w_ref, ln_ref, out_ref):
    h = _dot(y_ref[...], w_ref[...])
    out_ref[...] = _layer_norm(ALPHA * x_ref[...] + h, ln_ref[0:1, :], ln_ref[1:2, :])


def _proj_ln(y, x, w, ln8, tm):
    n, d = x.shape
    return pl.pallas_call(
        _proj_ln_kernel,
        grid=(n // tm,),
        in_specs=[pl.BlockSpec((tm, y.shape[1]), lambda i: (i, 0)), pl.BlockSpec((tm, d), lambda i: (i, 0)),
                  _const_spec(w.shape), _const_spec(ln8.shape)],
        out_specs=pl.BlockSpec((tm, d), lambda i: (i, 0)),
        out_shape=jax.ShapeDtypeStruct((n, d), F32),
        compiler_params=_params(("parallel",)),
        name="proj_ln",
    )(y, x, w, ln8)


def _router_kernel(x_ref, w_ref, comb_ref):
    x_hi, x_lo = _split(x_ref[...], 2)
    w_hi, w_lo = _split(w_ref[...], 2)
    logits = (jnp.dot(x_hi, w_hi, preferred_element_type=F32)
              + jnp.dot(x_lo, w_hi, preferred_element_type=F32)
              + jnp.dot(x_hi, w_lo, preferred_element_type=F32))
    lane = lax.broadcasted_iota(jnp.int32, logits.shape, 1).astype(F32)
    neg = -jnp.inf
    logits = jnp.where(lane < N_EXPERTS, logits, neg)
    m1 = jnp.max(logits, axis=-1, keepdims=True)
    i1 = jnp.min(jnp.where(logits == m1, lane, float(LANES)), axis=-1, keepdims=True)
    rest = jnp.where(lane == i1, neg, logits)
    m2 = jnp.max(rest, axis=-1, keepdims=True)
    i2 = jnp.min(jnp.where(rest == m2, lane, float(LANES)), axis=-1, keepdims=True)
    e2 = jnp.exp(m2 - m1)
    den = 1.0 + e2
    comb_ref[...] = jnp.where(lane == i1, 1.0 / den, 0.0) + jnp.where(lane == i2, e2 / den, 0.0)


def _router(x, w_pad, tm):
    n, d = x.shape
    return pl.pallas_call(
        _router_kernel,
        grid=(n // tm,),
        in_specs=[pl.BlockSpec((tm, d), lambda i: (i, 0)), _const_spec(w_pad.shape)],
        out_specs=pl.BlockSpec((tm, LANES), lambda i: (i, 0)),
        out_shape=jax.ShapeDtypeStruct((n, LANES), F32),
        compiler_params=_params(("parallel",)),
        name="router",
    )(x, w_pad)


def _pad_rows(rows, d):
    a = jnp.stack(rows).astype(F32)
    return jnp.concatenate([a, jnp.zeros((SUBLANES - a.shape[0], d), F32)], axis=0)


def _rope_tables(pos, reps):
    inv_freq = ROPE_THETA ** (-jnp.arange(0, ROPE_DIM, 2, dtype=jnp.float32) / ROPE_DIM)
    ang = pos.astype(jnp.float32)[:, None] * inv_freq[None, :]
    cos = jnp.cos(ang)
    sin = jnp.sin(ang)
    t = pos.shape[0]
    half = ROPE_DIM // 2
    rest = HEAD_B - ROPE_DIM
    z_half = jnp.zeros((t, half), F32)
    z_rest = jnp.zeros((t, rest), F32)
    cos_h = jnp.concatenate([cos, cos, jnp.ones((t, rest), F32)], axis=1)
    sn_h = jnp.concatenate([-sin, z_half, z_rest], axis=1)
    sp_h = jnp.concatenate([z_half, sin, z_rest], axis=1)
    per_tile = LANES // HEAD_B
    return tuple(jnp.tile(a, (reps, per_tile)) for a in (cos_h, sn_h, sp_h))


def _trunk(x, shift_in, wkv_in, k_cache, v_cache, pos0, P):
    bn, t, d = x.shape
    n = bn * t
    h_a = d // HEAD_A
    pw = 2 * HEAD_A
    n_pairs = d // pw
    xf = x.reshape(n, d)

    tm_pre = min(256, t)
    tm_row = min(256, n)
    tm_ffn = min(1024, n)
    chunk = min(CHUNK, t)
    t_blk = min(512, t)

    tiles = jnp.arange(n // tm_pre) * tm_pre
    prev_rows = xf[jnp.maximum(tiles - 1, 0)]
    start_rows = shift_in[0][tiles // t]
    bnd = jnp.where(((tiles % t) == 0)[:, None], start_rows, prev_rows)
    bnd = jnp.broadcast_to(bnd[:, None, :], (n // tm_pre, SUBLANES, d))

    mu8 = jnp.concatenate([P['a_mu'][0], jnp.zeros((2, d), F32)], axis=0)
    vec_pre = _pad_rows([P['a_w0'][0], P['a_a0'][0], P['a_k_k'][0], P['a_k_a'][0],
                         P['a_r_k'][0].reshape(d)], d)
    lane_head = jnp.arange(d) // HEAD_A
    ones_bd = (lane_head[:, None] == lane_head[None, :]).astype(BF16)
    bf = lambda a: a.astype(BF16)
    w_rkv = P['a_w_rkv'][0]
    r, ld, k_h, v, kk, ka, g, bonus = _rwkv_pre(
        xf, bnd, mu8, vec_pre, bf(w_rkv[0]), bf(w_rkv[1]), bf(w_rkv[2]),
        bf(P['a_w1'][0]), bf(P['a_w2'][0]), bf(P['a_a1'][0]), bf(P['a_a2'][0]),
        bf(P['a_g1'][0]), bf(P['a_g2'][0]), ones_bd, tm_pre)

    st_t = jnp.swapaxes(wkv_in[0].astype(F32), -1, -2).reshape(bn, n_pairs, 2, HEAD_A, HEAD_A)
    eye2 = jnp.eye(2, dtype=F32)
    s0 = jnp.einsum('bpajk,ac->bpajck', st_t, eye2).reshape(bn, n_pairs, pw, pw)
    seq3 = lambda a: a.reshape(bn, t, d)
    o, s_fin = _wkv_scan(seq3(r), seq3(ld), seq3(k_h), seq3(v), seq3(kk), seq3(ka), s0, chunk, t_blk)
    s_fin = s_fin.reshape(bn, n_pairs, 2, HEAD_A, 2, HEAD_A)
    s_fin = jnp.stack([s_fin[:, :, 0, :, 0, :], s_fin[:, :, 1, :, 1, :]], axis=2)
    wkv_out = jnp.swapaxes(s_fin, -1, -2).reshape(bn, h_a, HEAD_A, HEAD_A)
    shift_out = x[:, -1]

    vec_post = _pad_rows([P['a_lnx_g'][0], P['a_lnx_b'][0], P['ln_g'][0, 0], P['ln_b'][0, 0]], d)
    x1 = _rwkv_post(o.reshape(n, d), bonus, g, xf, vec_post, ones_bd, bf(P['a_w_o'][0]), tm_row)

    ln01 = _pad_rows([P['ln_g'][0, 1], P['ln_b'][0, 1]], d)
    dummy_comb = jnp.zeros((n, LANES), F32)
    x2 = _ffn(x1, dummy_comb, P['ffn_w_gu'], P['ffn_w_down'], ln01, tm_ffn, 512, gated=False)

    pos = pos0 + jnp.arange(t, dtype=jnp.int32)
    tables = _rope_tables(pos, max(tm_row // t, 1))
    kvw = KV_HEADS * HEAD_B
    k_new, v_new = _rope_proj(_kv_kernel, x2, bf(P['kv_w']), tables, tm_row, (kvw, kvw), "kv_proj")
    k_new = k_new.reshape(bn, t, kvw)
    v_new = v_new.reshape(bn, t, kvw)
    (q,) = _rope_proj(_q_kernel, x2, bf(P['b_w_q'][0]), tables, tm_row, (d,), "q_proj")
    q = q.reshape(bn, t, d)
    sinks = P['b_sinks'][0].astype(F32)
    if k_cache is None:
        att = _attention(q, k_new, k_new, v_new, v_new, sinks, min(256, t), banded=True)
        k_out = k_new[:, -WINDOW:]
        v_out = v_new[:, -WINDOW:]
    else:
        kc = k_cache.astype(F32).reshape(bn, WINDOW, kvw)
        vc = v_cache.astype(F32).reshape(bn, WINDOW, kvw)
        att = _attention(q, kc, k_new, vc, v_new, sinks, t, banded=False)
        k_out = jnp.concatenate([kc, k_new], axis=1)[:, -WINDOW:]
        v_out = jnp.concatenate([vc, v_new], axis=1)[:, -WINDOW:]
    ln10 = _pad_rows([P['ln_g'][1, 0], P['ln_b'][1, 0]], d)
    x3 = _proj_ln(att.reshape(n, d), x2, bf(P['b_w_o'][0]), ln10, tm_row)

    router_pad = jnp.concatenate([P['moe_router'][0], jnp.zeros((d, LANES - N_EXPERTS), F32)], axis=1)
    comb = _router(x3, router_pad, tm_row)
    ln11 = _pad_rows([P['ln_g'][1, 1], P['ln_b'][1, 1]], d)
    x4 = _ffn(x3, comb, P['moe_w_gu'][0], P['moe_w_down'][0], ln11, tm_ffn, 512, gated=True)

    return (x4.reshape(bn, t, d), shift_out[None], wkv_out[None],
            k_out.reshape(bn, WINDOW, KV_HEADS, HEAD_B), v_out.reshape(bn, WINDOW, KV_HEADS, HEAD_B))


def kernel(x_prompt, x_sample, cache_shift_a, state_wkv_a, cache_k_b, cache_v_b, a_mu, a_w_rkv, a_w0, a_w1, a_w2, a_a0, a_a1, a_a2, a_g1, a_g2, a_k_k, a_k_a, a_r_k, a_lnx_g, a_lnx_b, a_w_o, kv_w, b_w_q, b_sinks, b_w_o, ln_g, ln_b, ffn_w_gu, ffn_w_down, moe_router, moe_w_gu, moe_w_down):
    P = {
        'a_mu': a_mu, 'a_w_rkv': a_w_rkv, 'a_w0': a_w0, 'a_w1': a_w1, 'a_w2': a_w2,
        'a_a0': a_a0, 'a_a1': a_a1, 'a_a2': a_a2, 'a_g1': a_g1, 'a_g2': a_g2,
        'a_k_k': a_k_k, 'a_k_a': a_k_a, 'a_r_k': a_r_k, 'a_lnx_g': a_lnx_g,
        'a_lnx_b': a_lnx_b, 'a_w_o': a_w_o, 'kv_w': kv_w, 'b_w_q': b_w_q,
        'b_sinks': b_sinks, 'b_w_o': b_w_o, 'ln_g': ln_g, 'ln_b': ln_b,
        'ffn_w_gu': ffn_w_gu, 'ffn_w_down': ffn_w_down, 'moe_router': moe_router,
        'moe_w_gu': moe_w_gu, 'moe_w_down': moe_w_down,
    }
    bp = x_prompt.shape[0]
    d = x_prompt.shape[2]
    h_a = d // HEAD_A
    zero_shift = jnp.zeros((1, bp, d), x_prompt.dtype)
    zero_wkv = jnp.zeros((1, bp, h_a, HEAD_A, HEAD_A), F32)
    y_p, p_shift, p_wkv, p_k, p_v = _trunk(x_prompt, zero_shift, zero_wkv, None, None, 0, P)
    y_s, s_shift, s_wkv, s_k, s_v = _trunk(x_sample, cache_shift_a, state_wkv_a,
                                           cache_k_b, cache_v_b, PAST_LEN, P)
    return (y_p, y_s, p_shift, p_wkv, p_k, p_v, s_shift, s_wkv, s_k, s_v)
```

```python
import functools

import jax
import jax.numpy as jnp
from jax import lax
from jax.experimental import pallas as pl
from jax.experimental.pallas import tpu as pltpu

F32 = jnp.float32
BF16 = jnp.bfloat16

DEPTH = 2
HEAD_A = 64
HEAD_B = 64
KV_HEADS = 4
CHUNK = 64
WINDOW = 128
PAST_LEN = 4096
ROPE_DIM = HEAD_B // 4
ROPE_THETA = 500000.0
ATTN_SCALE = HEAD_B ** -0.5
N_EXPERTS = 8
GN_EPS = 64e-5
LN_EPS = 1e-5
ALPHA = (2.0 * DEPTH) ** 0.25

LANES = 128
SUBLANES = 8
VMEM_LIMIT_BYTES = 56 * 1024 * 1024
SCAN_GROUP = 8
SCAN_PAIRS = 2
MOE_TILE = 1024
MOE_ROW_TILE = 256


def _dot(a, b):
    return jnp.dot(a.astype(BF16), b.astype(BF16), preferred_element_type=F32)


def _dot_nt(a, b):
    return lax.dot_general(a.astype(BF16), b.astype(BF16), (((1,), (1,)), ((), ())),
                           preferred_element_type=F32)


def _dot_tn(a, b):
    return lax.dot_general(a.astype(BF16), b.astype(BF16), (((0,), (0,)), ((), ())),
                           preferred_element_type=F32)


def _split(x, n):
    parts = []
    rem = x
    for i in range(n):
        p = rem.astype(BF16)
        parts.append(p)
        if i + 1 < n:
            rem = rem - p.astype(F32)
    return parts


def _dot_exact_rhs(a, b_bf16, n):
    acc = None
    for p in _split(a, n):
        t = jnp.dot(p, b_bf16, preferred_element_type=F32)
        acc = t if acc is None else acc + t
    return acc


def _dot_exact_lhs(a_bf16, b, n):
    acc = None
    for p in _split(b, n):
        t = jnp.dot(a_bf16, p, preferred_element_type=F32)
        acc = t if acc is None else acc + t
    return acc


def _sigmoid(z):
    return 1.0 / (1.0 + jnp.exp(-z))


def _layer_norm(z, g, b):
    mu = jnp.mean(z, axis=-1, keepdims=True)
    zc = z - mu
    var = jnp.mean(zc * zc, axis=-1, keepdims=True)
    return zc * lax.rsqrt(var + LN_EPS) * g + b


def _const_spec(shape):
    nd = len(shape)
    return pl.BlockSpec(shape, lambda *_: (0,) * nd)


def _params(sem):
    return pltpu.CompilerParams(dimension_semantics=sem, vmem_limit_bytes=VMEM_LIMIT_BYTES)


def _rwkv_pre_kernel(x_ref, bnd_ref, mu_ref, vec_ref, wr_ref, wk_ref, wv_ref, w1_ref, w2_ref,
                     a1_ref, a2_ref, g1_ref, g2_ref, ones_ref,
                     r_out, ld_out, k_out, v_out, kk_out, ka_out, g_out, bonus_out):
    x = x_ref[...]
    rows = lax.broadcasted_iota(jnp.int32, x.shape, 0)
    xp = jnp.where(rows == 0, bnd_ref[0, 0:1, :], pltpu.roll(x, 1, 0))
    dx = xp - x

    def mix(s):
        return x + dx * mu_ref[s:s + 1, :]

    w0 = vec_ref[0:1, :]
    a0 = vec_ref[1:2, :]
    k_k = vec_ref[2:3, :]
    k_a = vec_ref[3:4, :]
    r_k = vec_ref[4:5, :]
    ones = ones_ref[...]

    r = _dot(mix(0), wr_ref[...])
    k = _dot(mix(1), wk_ref[...])
    v = _dot(mix(2), wv_ref[...])
    zw = -(w0 + _dot(jnp.tanh(_dot(mix(3), w1_ref[...])), w2_ref[...]))
    softplus = jnp.maximum(zw, 0.0) + jnp.log(1.0 + jnp.exp(-jnp.abs(zw)))
    ld = -jnp.exp(-softplus - 0.5)
    a = _sigmoid(a0 + _dot(_dot(mix(4), a1_ref[...]), a2_ref[...]))
    g = _dot(_sigmoid(_dot(mix(5), g1_ref[...])), g2_ref[...])

    kk = k * k_k
    ss = _dot_exact_rhs(kk * kk, ones, 2)
    kk = kk / jnp.maximum(jnp.sqrt(ss), 1e-12)
    k_h = k * (1.0 + (a - 1.0) * k_a)
    bonus = _dot_exact_rhs(r * k_h * r_k, ones, 2) * v

    r_out[...] = r
    ld_out[...] = ld
    k_out[...] = k_h
    v_out[...] = v
    kk_out[...] = kk
    ka_out[...] = kk * a
    g_out[...] = g
    bonus_out[...] = bonus


def _rwkv_pre(x, bnd, mu8, vec8, wr, wk, wv, w1, w2, a1, a2, g1, g2, ones_bd, tm):
    n, d = x.shape
    row = pl.BlockSpec((tm, d), lambda i: (i, 0))
    ins = [row, pl.BlockSpec((1, SUBLANES, d), lambda i: (i, 0, 0))]
    ins += [_const_spec(a.shape) for a in (mu8, vec8, wr, wk, wv, w1, w2, a1, a2, g1, g2, ones_bd)]
    return pl.pallas_call(
        _rwkv_pre_kernel,
        grid=(n // tm,),
        in_specs=ins,
        out_specs=[row] * 8,
        out_shape=[jax.ShapeDtypeStruct((n, d), F32)] * 8,
        compiler_params=_params(("parallel",)),
        name="rwkv_pre",
    )(x, bnd, mu8, vec8, wr, wk, wv, w1, w2, a1, a2, g1, g2, ones_bd)


def _wkv_scan_kernel(r_ref, ld_ref, k_ref, v_ref, kk_ref, ka_ref, s0_ref, o_ref, st_ref, s_scr,
                     *, chunk, n_chunks, group):
    c = chunk
    pw = 2 * HEAD_A
    n_pp = s_scr.shape[0]
    t_idx = pl.program_id(2)

    @pl.when(t_idx == 0)
    def _():
        s_scr[...] = s0_ref[...]

    ri = lax.broadcasted_iota(jnp.int32, (c, c), 0)
    ci = lax.broadcasted_iota(jnp.int32, (c, c), 1)
    tri_incl = ri >= ci
    cum_mat = jnp.where(tri_incl, 1.0, 0.0).astype(BF16)
    ri2 = lax.broadcasted_iota(jnp.int32, (c, 2 * c), 0)
    ci2 = lax.broadcasted_iota(jnp.int32, (c, 2 * c), 1) & (c - 1)
    strict2 = ri2 > ci2
    incl2 = ri2 >= ci2
    lane1 = lax.broadcasted_iota(jnp.int32, (1, pw), 1)
    head_a1 = lane1 < HEAD_A
    lane2 = lax.broadcasted_iota(jnp.int32, (1, 2 * pw), 1) & (pw - 1)
    head_a2 = lane2 < HEAD_A
    rs = lax.broadcasted_iota(jnp.int32, (pw, pw), 0)
    cs = lax.broadcasted_iota(jnp.int32, (pw, pw), 1)
    same_head = (rs < HEAD_A) == (cs < HEAD_A)
    eye = rs == cs
    zeros_cv = jnp.zeros((c, pw), F32)

    def group_maps(slices):
        each = lambda fn, *lists: [fn(*a) for a in zip(*lists)]
        ld = [ld_ref[ix] for ix in slices]
        cw = each(lambda x: _dot_exact_lhs(cum_mat, x, 3), ld)
        w_in = each(jnp.exp, cw)
        w_ex = each(lambda a, b: jnp.exp(a - b), cw, ld)
        w_inv = each(lambda a: jnp.exp(-a), cw)
        w_last = each(lambda a: a[c - 1:c, :], w_in)
        knt = [-(kk_ref[ix] * w) for ix, w in zip(slices, w_ex)]
        kat = [ka_ref[ix] * w for ix, w in zip(slices, w_inv)]
        kt = [k_ref[ix] * w for ix, w in zip(slices, w_inv)]
        rt = [r_ref[ix] * w for ix, w in zip(slices, w_in)]
        v = [v_ref[ix] for ix in slices]
        lh = each(lambda a, b: jnp.concatenate([a, b], axis=0), knt, rt)
        rh = each(lambda a, b: jnp.concatenate([a, b], axis=0), kat, kt)
        zv = each(lambda a: jnp.concatenate([zeros_cv, a], axis=0), v)

        full_a = each(lambda a, b: _dot_nt(jnp.where(head_a1, a, 0.0), b), lh, rh)
        full_b = each(lambda a, b: _dot_nt(jnp.where(head_a1, 0.0, a), b), lh, rh)
        top_a = each(lambda a: jnp.where(strict2, a[:c, :], 0.0), full_a)
        top_b = each(lambda a: jnp.where(strict2, a[:c, :], 0.0), full_b)
        bot_a = each(lambda a: jnp.where(incl2, a[c:, :], 0.0), full_a)
        bot_b = each(lambda a: jnp.where(incl2, a[c:, :], 0.0), full_b)
        akv_a = each(_dot, top_a, zv)
        akv_b = each(_dot, top_b, zv)
        akv = each(lambda a, b: jnp.where(head_a1, a, b), akv_a, akv_b)

        x = each(lambda a, b: jnp.concatenate([a, b], axis=1), knt, akv)
        n_a = each(lambda a: a[:, :c], top_a)
        n_b = each(lambda a: a[:, :c], top_b)
        span = 1
        while span < c:
            xa = each(_dot, n_a, x)
            xb = each(_dot, n_b, x)
            span *= 2
            if span < c:
                n_a = each(_dot, n_a, n_a)
                n_b = each(_dot, n_b, n_b)
            x = each(lambda x0, a, b: x0 + jnp.where(head_a2, a, b), x, xa, xb)

        rhs2 = each(lambda a, b: jnp.concatenate(
            [a, jnp.concatenate([zeros_cv, b], axis=1)], axis=0), x, v)
        qo_a = each(_dot, bot_a, rhs2)
        qo_b = each(_dot, bot_b, rhs2)
        lt = each(lambda a, b, w: jnp.concatenate([a * w, b * w], axis=0), kat, kt, w_last)
        mb = each(_dot_tn, lt, rhs2)
        out = []
        for j in range(len(slices)):
            qo = jnp.where(head_a2, qo_a[j], qo_b[j])
            q = rt[j] + qo[:, :pw]
            m = jnp.where(eye, w_last[j], 0.0) + jnp.where(same_head, mb[j][:, :pw], 0.0)
            b = jnp.where(same_head, mb[j][:, pw:], 0.0)
            mq = jnp.concatenate([m, q], axis=0)
            out.append((_split(mq, 2), b, qo[:, pw:]))
        return out

    def body(it, carry):
        items = [(j, pp) for j in range(group) for pp in range(n_pp)]
        slices = [(pl.ds(pl.multiple_of((it * group + j) * c, c), c), pl.ds(pp * pw, pw))
                  for j, pp in items]
        maps = group_maps(slices)
        s = [s_scr[pp] for pp in range(n_pp)]
        for (j, pp), ix, ((mq_hi, mq_lo), b, o0) in zip(items, slices, maps):
            s_hi, s_lo = _split(s[pp], 2)
            res = (jnp.dot(mq_hi, s_hi, preferred_element_type=F32)
                   + jnp.dot(mq_hi, s_lo, preferred_element_type=F32)
                   + jnp.dot(mq_lo, s_hi, preferred_element_type=F32))
            s[pp] = res[:pw, :] + b
            o_ref[ix] = res[pw:, :] + o0
        for pp in range(n_pp):
            s_scr[pp] = s[pp]
        return carry

    lax.fori_loop(0, n_chunks // group, body, 0)

    @pl.when(t_idx == pl.num_programs(2) - 1)
    def _():
        st_ref[...] = s_scr[...]


def _wkv_scan(r, ld, k, v, kk, ka, s0, chunk, t_blk):
    bn, t, d = r.shape
    pw = 2 * HEAD_A
    n_pairs = d // pw
    n_pp = SCAN_PAIRS
    seq = pl.BlockSpec((None, t_blk, n_pp * pw), lambda b, p, i: (b, i, p))
    st = pl.BlockSpec((None, n_pp, pw, pw), lambda b, p, i: (b, p, 0, 0))
    n_chunks = t_blk // chunk
    kern = functools.partial(_wkv_scan_kernel, chunk=chunk, n_chunks=n_chunks,
                             group=min(SCAN_GROUP, n_chunks))
    return pl.pallas_call(
        kern,
        grid=(bn, n_pairs // n_pp, t // t_blk),
        in_specs=[seq] * 6 + [st],
        out_specs=[seq, st],
        out_shape=[jax.ShapeDtypeStruct((bn, t, d), F32),
                   jax.ShapeDtypeStruct((bn, n_pairs, pw, pw), F32)],
        scratch_shapes=[pltpu.VMEM((n_pp, pw, pw), F32)],
        compiler_params=_params(("parallel", "parallel", "arbitrary")),
        name="wkv_scan",
    )(r, ld, k, v, kk, ka, s0)


def _rwkv_post_kernel(o_ref, bonus_ref, g_ref, x_ref, vec_ref, ones_ref, wo_ref, out_ref):
    o = o_ref[...]
    ones = ones_ref[...]
    inv_n = 1.0 / HEAD_A
    mean = _dot_exact_rhs(o, ones, 2) * inv_n
    oc = o - mean
    var = _dot_exact_rhs(oc * oc, ones, 2) * inv_n
    on = oc * lax.rsqrt(var + GN_EPS) * vec_ref[0:1, :] + vec_ref[1:2, :]
    y = (on + bonus_ref[...]) * g_ref[...]
    h = _dot(y, wo_ref[...])
    out_ref[...] = _layer_norm(ALPHA * x_ref[...] + h, vec_ref[2:3, :], vec_ref[3:4, :])


def _rwkv_post(o, bonus, g, x, vec8, ones_bd, wo, tm):
    n, d = x.shape
    row = pl.BlockSpec((tm, d), lambda i: (i, 0))
    return pl.pallas_call(
        _rwkv_post_kernel,
        grid=(n // tm,),
        in_specs=[row] * 4 + [_const_spec(vec8.shape), _const_spec(ones_bd.shape), _const_spec(wo.shape)],
        out_specs=row,
        out_shape=jax.ShapeDtypeStruct((n, d), F32),
        compiler_params=_params(("parallel",)),
        name="rwkv_post",
    )(o, bonus, g, x, vec8, ones_bd, wo)


def _ffn_kernel(x_ref, wg_ref, wu_ref, wd_ref, ln_ref, out_ref, acc_ref, xb_ref):
    f = pl.program_id(1)

    @pl.when(f == 0)
    def _():
        acc_ref[...] = jnp.zeros_like(acc_ref)
        xb_ref[...] = x_ref[...].astype(BF16)

    xb = xb_ref[...]
    gate = jnp.dot(xb, wg_ref[...].astype(BF16), preferred_element_type=F32)
    up = jnp.dot(xb, wu_ref[...].astype(BF16), preferred_element_type=F32)
    h = gate * _sigmoid(gate) * up
    acc_ref[...] += jnp.dot(h.astype(BF16), wd_ref[...].astype(BF16), preferred_element_type=F32)

    @pl.when(f == pl.num_programs(1) - 1)
    def _():
        out_ref[...] = _layer_norm(ALPHA * x_ref[...] + acc_ref[...], ln_ref[0:1, :], ln_ref[1:2, :])


def _ffn(x, w_gu, w_down, ln8, tm, tf):
    n, d = x.shape
    d_ff = w_gu.shape[1] // 2
    nf = d_ff // tf
    return pl.pallas_call(
        _ffn_kernel,
        grid=(n // tm, nf),
        in_specs=[
            pl.BlockSpec((tm, d), lambda i, f: (i, 0)),
            pl.BlockSpec((d, tf), lambda i, f: (0, f)),
            pl.BlockSpec((d, tf), lambda i, f: (0, nf + f)),
            pl.BlockSpec((tf, d), lambda i, f: (f, 0)),
            _const_spec(ln8.shape),
        ],
        out_specs=pl.BlockSpec((tm, d), lambda i, f: (i, 0)),
        out_shape=jax.ShapeDtypeStruct((n, d), F32),
        scratch_shapes=[pltpu.VMEM((tm, d), F32), pltpu.VMEM((tm, d), BF16)],
        compiler_params=_params(("parallel", "arbitrary")),
        name="dense_ffn",
    )(x, w_gu, w_gu, w_down, ln8)


ROUTE_I1, ROUTE_I2, ROUTE_G1, ROUTE_G2, ROUTE_P1, ROUTE_P2 = range(6)


def _router_kernel(x_ref, w_ref, route_ref, count_ref, carry_ref):
    i = pl.program_id(0)

    @pl.when(i == 0)
    def _():
        carry_ref[...] = jnp.zeros_like(carry_ref)

    x_hi, x_lo = _split(x_ref[...], 2)
    w_hi, w_lo = _split(w_ref[...], 2)
    logits = (jnp.dot(x_hi, w_hi, preferred_element_type=F32)
              + jnp.dot(x_lo, w_hi, preferred_element_type=F32)
              + jnp.dot(x_hi, w_lo, preferred_element_type=F32))
    tm = logits.shape[0]
    lane = lax.broadcasted_iota(jnp.int32, logits.shape, 1).astype(F32)
    neg = -jnp.inf
    logits = jnp.where(lane < N_EXPERTS, logits, neg)
    m1 = jnp.max(logits, axis=-1, keepdims=True)
    i1 = jnp.min(jnp.where(logits == m1, lane, float(LANES)), axis=-1, keepdims=True)
    rest = jnp.where(lane == i1, neg, logits)
    m2 = jnp.max(rest, axis=-1, keepdims=True)
    i2 = jnp.min(jnp.where(rest == m2, lane, float(LANES)), axis=-1, keepdims=True)
    e2 = jnp.exp(m2 - m1)
    den = 1.0 + e2

    sel1 = lane == i1
    sel2 = lane == i2
    onehot = jnp.where(sel1, 1.0, 0.0) + jnp.where(sel2, 1.0, 0.0)
    ri = lax.broadcasted_iota(jnp.int32, (tm, tm), 0)
    ci = lax.broadcasted_iota(jnp.int32, (tm, tm), 1)
    earlier = jnp.where(ri > ci, 1.0, 0.0).astype(BF16)
    rank = carry_ref[0:1, :] + jnp.dot(earlier, onehot.astype(BF16), preferred_element_type=F32)
    p1 = jnp.sum(jnp.where(sel1, rank, 0.0), axis=-1, keepdims=True)
    p2 = jnp.sum(jnp.where(sel2, rank, 0.0), axis=-1, keepdims=True)
    carry_ref[0:1, :] = carry_ref[0:1, :] + jnp.sum(onehot, axis=0, keepdims=True)

    route = jnp.zeros_like(logits)
    for col, val in ((ROUTE_I1, i1), (ROUTE_I2, i2), (ROUTE_G1, 1.0 / den), (ROUTE_G2, e2 / den),
                     (ROUTE_P1, p1), (ROUTE_P2, p2)):
        route = jnp.where(lane == float(col), val, route)
    route_ref[...] = route
    count_ref[...] = carry_ref[...]


def _router(x, w_pad, tm):
    n, d = x.shape
    return pl.pallas_call(
        _router_kernel,
        grid=(n // tm,),
        in_specs=[pl.BlockSpec((tm, d), lambda i: (i, 0)), _const_spec(w_pad.shape)],
        out_specs=[pl.BlockSpec((tm, LANES), lambda i: (i, 0)), _const_spec((SUBLANES, LANES))],
        out_shape=[jax.ShapeDtypeStruct((n, LANES), F32), jax.ShapeDtypeStruct((SUBLANES, LANES), F32)],
        scratch_shapes=[pltpu.VMEM((SUBLANES, LANES), F32)],
        compiler_params=_params(("arbitrary",)),
        name="router",
    )(x, w_pad)


def _row_copy(src_ref, src_row, dst_ref, dst_row, sem):
    return pltpu.make_async_copy(src_ref.at[pl.ds(src_row, 1), :], dst_ref.at[pl.ds(dst_row, 1), :], sem)


def _dispatch_kernel(slot_ref, x_ref, xs_in_ref, xs_ref, sem):
    del xs_in_ref
    tm = x_ref.shape[0]

    def start(r, carry):
        _row_copy(x_ref, r, xs_ref, slot_ref[0, r], sem).start()
        _row_copy(x_ref, r, xs_ref, slot_ref[0, tm + r], sem).start()
        return carry

    def wait(r, carry):
        _row_copy(x_ref, 0, xs_ref, 0, sem).wait()
        _row_copy(x_ref, 0, xs_ref, 0, sem).wait()
        return carry

    lax.fori_loop(0, tm, start, 0)
    lax.fori_loop(0, tm, wait, 0)


def _dispatch(slots, x, xs, tm):
    n, d = x.shape
    return pl.pallas_call(
        _dispatch_kernel,
        grid=(n // tm,),
        in_specs=[pl.BlockSpec((None, 1, 2 * tm), lambda i: (i, 0, 0), memory_space=pltpu.SMEM),
                  pl.BlockSpec((tm, d), lambda i: (i, 0)),
                  pl.BlockSpec(memory_space=pl.ANY)],
        out_specs=pl.BlockSpec(memory_space=pl.ANY),
        out_shape=jax.ShapeDtypeStruct(xs.shape, xs.dtype),
        scratch_shapes=[pltpu.SemaphoreType.DMA(())],
        input_output_aliases={2: 0},
        compiler_params=_params(("arbitrary",)),
        name="moe_dispatch",
    )(slots, x, xs)


def _experts_kernel(te_ref, nu_ref, x_ref, wg_ref, wu_ref, wd_ref, out_ref, acc_ref, xb_ref):
    del te_ref
    i = pl.program_id(0)
    f = pl.program_id(1)
    used = i < nu_ref[0]
    last = f == pl.num_programs(1) - 1

    @pl.when(used)
    def _():
        @pl.when(f == 0)
        def _():
            acc_ref[...] = jnp.zeros_like(acc_ref)
            xb_ref[...] = x_ref[...].astype(BF16)

        xb = xb_ref[...]
        gate = jnp.dot(xb, wg_ref[...].astype(BF16), preferred_element_type=F32)
        up = jnp.dot(xb, wu_ref[...].astype(BF16), preferred_element_type=F32)
        h = gate * _sigmoid(gate) * up
        acc_ref[...] += jnp.dot(h.astype(BF16), wd_ref[...].astype(BF16), preferred_element_type=F32)

        @pl.when(last)
        def _():
            out_ref[...] = acc_ref[...]

    @pl.when(jnp.logical_and(jnp.logical_not(used), last))
    def _():
        out_ref[...] = jnp.zeros_like(out_ref)


def _experts(tile_expert, n_used, xs, w_gu, w_down, tm, tf):
    s_total, d = xs.shape
    d_ff = w_gu.shape[2] // 2
    nf = d_ff // tf

    def f_eff(i, f, nu):
        return jnp.where(i < nu[0], f, nf - 1)

    grid_spec = pltpu.PrefetchScalarGridSpec(
        num_scalar_prefetch=2,
        grid=(s_total // tm, nf),
        in_specs=[
            pl.BlockSpec((tm, d), lambda i, f, te, nu: (i, 0)),
            pl.BlockSpec((None, d, tf), lambda i, f, te, nu: (te[i], 0, f_eff(i, f, nu))),
            pl.BlockSpec((None, d, tf), lambda i, f, te, nu: (te[i], 0, nf + f_eff(i, f, nu))),
            pl.BlockSpec((None, tf, d), lambda i, f, te, nu: (te[i], f_eff(i, f, nu), 0)),
        ],
        out_specs=pl.BlockSpec((tm, d), lambda i, f, te, nu: (i, 0)),
        scratch_shapes=[pltpu.VMEM((tm, d), F32), pltpu.VMEM((tm, d), BF16)],
    )
    return pl.pallas_call(
        _experts_kernel,
        grid_spec=grid_spec,
        out_shape=jax.ShapeDtypeStruct((s_total, d), F32),
        compiler_params=_params(("arbitrary", "arbitrary")),
        name="moe_experts",
    )(tile_expert, n_used, xs, w_gu, w_gu, w_down)


def _combine_kernel(slot_ref, route_ref, x_ref, ys_ref, ln_ref, out_ref, buf1, buf2, sem):
    tm = x_ref.shape[0]

    def start(r, carry):
        _row_copy(ys_ref, slot_ref[0, r], buf1, r, sem).start()
        _row_copy(ys_ref, slot_ref[0, tm + r], buf2, r, sem).start()
        return carry

    def wait(r, carry):
        _row_copy(ys_ref, 0, buf1, 0, sem).wait()
        _row_copy(ys_ref, 0, buf2, 0, sem).wait()
        return carry

    lax.fori_loop(0, tm, start, 0)
    lax.fori_loop(0, tm, wait, 0)
    route = route_ref[...]
    g1 = route[:, ROUTE_G1:ROUTE_G1 + 1]
    g2 = route[:, ROUTE_G2:ROUTE_G2 + 1]
    y = g1 * buf1[...] + g2 * buf2[...]
    out_ref[...] = _layer_norm(ALPHA * x_ref[...] + y, ln_ref[0:1, :], ln_ref[1:2, :])


def _combine(slots, route, x, ys, ln8, tm):
    n, d = x.shape
    return pl.pallas_call(
        _combine_kernel,
        grid=(n // tm,),
        in_specs=[pl.BlockSpec((None, 1, 2 * tm), lambda i: (i, 0, 0), memory_space=pltpu.SMEM),
                  pl.BlockSpec((tm, LANES), lambda i: (i, 0)),
                  pl.BlockSpec((tm, d), lambda i: (i, 0)),
                  pl.BlockSpec(memory_space=pl.ANY),
                  _const_spec(ln8.shape)],
        out_specs=pl.BlockSpec((tm, d), lambda i: (i, 0)),
        out_shape=jax.ShapeDtypeStruct((n, d), F32),
        scratch_shapes=[pltpu.VMEM((tm, d), F32), pltpu.VMEM((tm, d), F32), pltpu.SemaphoreType.DMA(())],
        compiler_params=_params(("arbitrary",)),
        name="moe_combine",
    )(slots, route, x, ys, ln8)


def _rope(y, cos_t, sin_next, sin_prev):
    n = y.shape[1]
    reps = n // LANES
    tile = lambda t: jnp.concatenate([t] * reps, axis=1) if reps > 1 else t
    half = ROPE_DIM // 2
    return (y * tile(cos_t)
            + pltpu.roll(y, n - half, 1) * tile(sin_next)
            + pltpu.roll(y, half, 1) * tile(sin_prev))


def _kv_kernel(x_ref, w_ref, cos_ref, sn_ref, sp_ref, k_out, v_out):
    kv = _dot(x_ref[...], w_ref[...])
    nk = k_out.shape[1]
    k_out[...] = _rope(kv[:, :nk], cos_ref[...], sn_ref[...], sp_ref[...])
    v_out[...] = kv[:, nk:]


def _q_kernel(x_ref, w_ref, cos_ref, sn_ref, sp_ref, q_out):
    q_out[...] = _rope(_dot(x_ref[...], w_ref[...]), cos_ref[...], sn_ref[...], sp_ref[...])


def _rope_proj(kernel_fn, x, w, tables, tm, out_widths, name):
    n, d = x.shape
    t_tiles = tables[0].shape[0] // tm
    row = pl.BlockSpec((tm, d), lambda i: (i, 0))
    tab = pl.BlockSpec((tm, LANES), lambda i: (i % t_tiles, 0))
    return pl.pallas_call(
        kernel_fn,
        grid=(n // tm,),
        in_specs=[row, _const_spec(w.shape), tab, tab, tab],
        out_specs=[pl.BlockSpec((tm, ow), lambda i: (i, 0)) for ow in out_widths],
        out_shape=[jax.ShapeDtypeStruct((n, ow), F32) for ow in out_widths],
        compiler_params=_params(("parallel",)),
        name=name,
    )(x, w, *tables)


def _attn_kernel(sink_ref, q_ref, kp_ref, kc_ref, vp_ref, vc_ref, o_ref, *, banded):
    tq = q_ref.shape[0]
    n_prev = kp_ref.shape[0]
    tk = n_prev + kc_ref.shape[0]
    q = q_ref[...]
    kband = jnp.concatenate([kp_ref[...], kc_ref[...]], axis=0)
    vband = jnp.concatenate([vp_ref[...], vc_ref[...]], axis=0)
    if banded:
        qi = lax.broadcasted_iota(jnp.int32, (tq, tk), 0)
        kj = lax.broadcasted_iota(jnp.int32, (tq, tk), 1)
        lo = qi - (qi & (CHUNK - 1))
        key_pos = pl.program_id(1) * tq - n_prev + kj
        mask = (kj >= lo) & (kj < lo + WINDOW + CHUNK) & (key_pos >= 0)
    group = (q.shape[1] // HEAD_B) // KV_HEADS
    outs = []
    for kh in range(KV_HEADS):
        kb = kband[:, kh * HEAD_B:(kh + 1) * HEAD_B]
        vb = vband[:, kh * HEAD_B:(kh + 1) * HEAD_B]
        for j in range(group):
            h = kh * group + j
            qh = q[:, h * HEAD_B:(h + 1) * HEAD_B]
            s = _dot_nt(qh, kb) * ATTN_SCALE
            if banded:
                s = jnp.where(mask, s, -jnp.inf)
            sk = sink_ref[h]
            m = jnp.maximum(jnp.max(s, axis=-1, keepdims=True), sk)
            p = jnp.exp(s - m)
            p = p / (jnp.sum(p, axis=-1, keepdims=True) + jnp.exp(sk - m))
            outs.append(_dot(p, vb))
    o_ref[...] = jnp.concatenate(outs, axis=1)


def _attention(q, k_prev_src, k_cur_src, v_prev_src, v_cur_src, sinks, tq, banded):
    bn, t, d = q.shape
    kw = k_cur_src.shape[2]
    if banded:
        ratio = tq // WINDOW
        prev_map = lambda b, i: (b, jnp.maximum(i * ratio - 1, 0), 0)
    else:
        prev_map = lambda b, i: (b, 0, 0)
    prev = pl.BlockSpec((None, WINDOW, kw), prev_map)
    cur = pl.BlockSpec((None, tq, kw), lambda b, i: (b, i, 0))
    kern = functools.partial(_attn_kernel, banded=banded)
    return pl.pallas_call(
        kern,
        grid=(bn, t // tq),
        in_specs=[pl.BlockSpec(memory_space=pltpu.SMEM),
                  pl.BlockSpec((None, tq, d), lambda b, i: (b, i, 0)), prev, cur, prev, cur],
        out_specs=pl.BlockSpec((None, tq, d), lambda b, i: (b, i, 0)),
        out_shape=jax.ShapeDtypeStruct((bn, t, d), F32),
        compiler_params=_params(("parallel", "parallel")),
        name="swa_attn",
    )(sinks, q, k_prev_src, k_cur_src, v_prev_src, v_cur_src)


def _proj_ln_kernel(y_ref, x_ref, w_ref, ln_ref, out_ref):
    h = _dot(y_ref[...], w_ref[...])
    out_ref[...] = _layer_norm(ALPHA * x_ref[...] + h, ln_ref[0:1, :], ln_ref[1:2, :])


def _proj_ln(y, x, w, ln8, tm):
    n, d = x.shape
    return pl.pallas_call(
        _proj_ln_kernel,
        grid=(n // tm,),
        in_specs=[pl.BlockSpec((tm, y.shape[1]), lambda i: (i, 0)), pl.BlockSpec((tm, d), lambda i: (i, 0)),
                  _const_spec(w.shape), _const_spec(ln8.shape)],
        out_specs=pl.BlockSpec((tm, d), lambda i: (i, 0)),
        out_shape=jax.ShapeDtypeStruct((n, d), F32),
        compiler_params=_params(("parallel",)),
        name="proj_ln",
    )(y, x, w, ln8)


def _moe_layer(xs_rows, P):
    d = xs_rows[0].shape[1]
    tm_e = MOE_TILE
    router_pad = jnp.concatenate([P['moe_router'][0], jnp.zeros((d, LANES - N_EXPERTS), F32)], axis=1)
    routed = [_router(x, router_pad, min(MOE_ROW_TILE, x.shape[0])) for x in xs_rows]
    counts = [c[0, :N_EXPERTS].astype(jnp.int32) for _, c in routed]
    total = sum(counts)
    padded = ((total + tm_e - 1) // tm_e) * tm_e
    ends = jnp.cumsum(padded)
    starts = ends - padded
    n_assign = 2 * sum(x.shape[0] for x in xs_rows)
    n_tiles = (n_assign + N_EXPERTS * (tm_e - 1)) // tm_e
    n_used = (ends[-1] // tm_e).astype(jnp.int32)
    tile_expert = jnp.sum((jnp.arange(n_tiles) * tm_e)[:, None] >= ends[None, :], axis=1).astype(jnp.int32)
    tile_expert = jnp.minimum(tile_expert, N_EXPERTS - 1)
    tile_expert = jnp.where(jnp.arange(n_tiles) < n_used, tile_expert,
                            tile_expert[jnp.maximum(n_used - 1, 0)])

    slots = []
    base = starts
    for (route, _), cnt, x in zip(routed, counts, xs_rows):
        tm = min(MOE_ROW_TILE, x.shape[0])
        i1 = route[:, ROUTE_I1].astype(jnp.int32)
        i2 = route[:, ROUTE_I2].astype(jnp.int32)
        s1 = base[i1] + route[:, ROUTE_P1].astype(jnp.int32)
        s2 = base[i2] + route[:, ROUTE_P2].astype(jnp.int32)
        slots.append(jnp.concatenate([s1.reshape(-1, 1, tm), s2.reshape(-1, 1, tm)], axis=2))
        base = base + cnt

    xs = jnp.zeros((n_tiles * tm_e, d), F32)
    for sl, x in zip(slots, xs_rows):
        xs = _dispatch(sl, x, xs, min(MOE_ROW_TILE, x.shape[0]))
    ys = _experts(tile_expert, n_used.reshape(1), xs, P['moe_w_gu'][0], P['moe_w_down'][0], tm_e, 512)
    ln11 = _pad_rows([P['ln_g'][1, 1], P['ln_b'][1, 1]], d)
    return [_combine(sl, route, x, ys, ln11, min(MOE_ROW_TILE, x.shape[0]))
            for sl, (route, _), x in zip(slots, routed, xs_rows)]


def _pad_rows(rows, d):
    a = jnp.stack(rows).astype(F32)
    return jnp.concatenate([a, jnp.zeros((SUBLANES - a.shape[0], d), F32)], axis=0)


def _rope_tables(pos, reps):
    inv_freq = ROPE_THETA ** (-jnp.arange(0, ROPE_DIM, 2, dtype=jnp.float32) / ROPE_DIM)
    ang = pos.astype(jnp.float32)[:, None] * inv_freq[None, :]
    cos = jnp.cos(ang)
    sin = jnp.sin(ang)
    t = pos.shape[0]
    half = ROPE_DIM // 2
    rest = HEAD_B - ROPE_DIM
    z_half = jnp.zeros((t, half), F32)
    z_rest = jnp.zeros((t, rest), F32)
    cos_h = jnp.concatenate([cos, cos, jnp.ones((t, rest), F32)], axis=1)
    sn_h = jnp.concatenate([-sin, z_half, z_rest], axis=1)
    sp_h = jnp.concatenate([z_half, sin, z_rest], axis=1)
    per_tile = LANES // HEAD_B
    return tuple(jnp.tile(a, (reps, per_tile)) for a in (cos_h, sn_h, sp_h))


def _trunk(x, shift_in, wkv_in, k_cache, v_cache, pos0, P):
    bn, t, d = x.shape
    n = bn * t
    h_a = d // HEAD_A
    pw = 2 * HEAD_A
    n_pairs = d // pw
    xf = x.reshape(n, d)

    tm_pre = min(256, t)
    tm_row = min(256, n)
    tm_ffn = min(1024, n)
    chunk = min(CHUNK, t)
    t_blk = min(512, t)

    tiles = jnp.arange(n // tm_pre) * tm_pre
    prev_rows = xf[jnp.maximum(tiles - 1, 0)]
    start_rows = shift_in[0][tiles // t]
    bnd = jnp.where(((tiles % t) == 0)[:, None], start_rows, prev_rows)
    bnd = jnp.broadcast_to(bnd[:, None, :], (n // tm_pre, SUBLANES, d))

    mu8 = jnp.concatenate([P['a_mu'][0], jnp.zeros((2, d), F32)], axis=0)
    vec_pre = _pad_rows([P['a_w0'][0], P['a_a0'][0], P['a_k_k'][0], P['a_k_a'][0],
                         P['a_r_k'][0].reshape(d)], d)
    lane_head = jnp.arange(d) // HEAD_A
    ones_bd = (lane_head[:, None] == lane_head[None, :]).astype(BF16)
    bf = lambda a: a.astype(BF16)
    w_rkv = P['a_w_rkv'][0]
    r, ld, k_h, v, kk, ka, g, bonus = _rwkv_pre(
        xf, bnd, mu8, vec_pre, bf(w_rkv[0]), bf(w_rkv[1]), bf(w_rkv[2]),
        bf(P['a_w1'][0]), bf(P['a_w2'][0]), bf(P['a_a1'][0]), bf(P['a_a2'][0]),
        bf(P['a_g1'][0]), bf(P['a_g2'][0]), ones_bd, tm_pre)

    st_t = jnp.swapaxes(wkv_in[0].astype(F32), -1, -2).reshape(bn, n_pairs, 2, HEAD_A, HEAD_A)
    eye2 = jnp.eye(2, dtype=F32)
    s0 = jnp.einsum('bpajk,ac->bpajck', st_t, eye2).reshape(bn, n_pairs, pw, pw)
    seq3 = lambda a: a.reshape(bn, t, d)
    o, s_fin = _wkv_scan(seq3(r), seq3(ld), seq3(k_h), seq3(v), seq3(kk), seq3(ka), s0, chunk, t_blk)
    s_fin = s_fin.reshape(bn, n_pairs, 2, HEAD_A, 2, HEAD_A)
    s_fin = jnp.stack([s_fin[:, :, 0, :, 0, :], s_fin[:, :, 1, :, 1, :]], axis=2)
    wkv_out = jnp.swapaxes(s_fin, -1, -2).reshape(bn, h_a, HEAD_A, HEAD_A)
    shift_out = x[:, -1]

    vec_post = _pad_rows([P['a_lnx_g'][0], P['a_lnx_b'][0], P['ln_g'][0, 0], P['ln_b'][0, 0]], d)
    x1 = _rwkv_post(o.reshape(n, d), bonus, g, xf, vec_post, ones_bd, bf(P['a_w_o'][0]), tm_row)

    ln01 = _pad_rows([P['ln_g'][0, 1], P['ln_b'][0, 1]], d)
    x2 = _ffn(x1, P['ffn_w_gu'][0], P['ffn_w_down'][0], ln01, tm_ffn, 512)

    pos = pos0 + jnp.arange(t, dtype=jnp.int32)
    tables = _rope_tables(pos, max(tm_row // t, 1))
    kvw = KV_HEADS * HEAD_B
    k_new, v_new = _rope_proj(_kv_kernel, x2, bf(P['kv_w']), tables, tm_row, (kvw, kvw), "kv_proj")
    k_new = k_new.reshape(bn, t, kvw)
    v_new = v_new.reshape(bn, t, kvw)
    (q,) = _rope_proj(_q_kernel, x2, bf(P['b_w_q'][0]), tables, tm_row, (d,), "q_proj")
    q = q.reshape(bn, t, d)
    sinks = P['b_sinks'][0].astype(F32)
    if k_cache is None:
        att = _attention(q, k_new, k_new, v_new, v_new, sinks, min(256, t), banded=True)
        k_out = k_new[:, -WINDOW:]
        v_out = v_new[:, -WINDOW:]
    else:
        kc = k_cache.astype(F32).reshape(bn, WINDOW, kvw)
        vc = v_cache.astype(F32).reshape(bn, WINDOW, kvw)
        att = _attention(q, kc, k_new, vc, v_new, sinks, t, banded=False)
        k_out = jnp.concatenate([kc, k_new], axis=1)[:, -WINDOW:]
        v_out = jnp.concatenate([vc, v_new], axis=1)[:, -WINDOW:]
    ln10 = _pad_rows([P['ln_g'][1, 0], P['ln_b'][1, 0]], d)
    x3 = _proj_ln(att.reshape(n, d), x2, bf(P['b_w_o'][0]), ln10, tm_row)

    return (x3, shift_out[None], wkv_out[None],
            k_out.reshape(bn, WINDOW, KV_HEADS, HEAD_B), v_out.reshape(bn, WINDOW, KV_HEADS, HEAD_B))


def kernel(x_prompt, x_sample, cache_shift_a, state_wkv_a, cache_k_b, cache_v_b, a_mu, a_w_rkv, a_w0, a_w1, a_w2, a_a0, a_a1, a_a2, a_g1, a_g2, a_k_k, a_k_a, a_r_k, a_lnx_g, a_lnx_b, a_w_o, kv_w, b_w_q, b_sinks, b_w_o, ln_g, ln_b, ffn_w_gu, ffn_w_down, moe_router, moe_w_gu, moe_w_down):
    P = {
        'a_mu': a_mu, 'a_w_rkv': a_w_rkv, 'a_w0': a_w0, 'a_w1': a_w1, 'a_w2': a_w2,
        'a_a0': a_a0, 'a_a1': a_a1, 'a_a2': a_a2, 'a_g1': a_g1, 'a_g2': a_g2,
        'a_k_k': a_k_k, 'a_k_a': a_k_a, 'a_r_k': a_r_k, 'a_lnx_g': a_lnx_g,
        'a_lnx_b': a_lnx_b, 'a_w_o': a_w_o, 'kv_w': kv_w, 'b_w_q': b_w_q,
        'b_sinks': b_sinks, 'b_w_o': b_w_o, 'ln_g': ln_g, 'ln_b': ln_b,
        'ffn_w_gu': ffn_w_gu, 'ffn_w_down': ffn_w_down, 'moe_router': moe_router,
        'moe_w_gu': moe_w_gu, 'moe_w_down': moe_w_down,
    }
    bp = x_prompt.shape[0]
    d = x_prompt.shape[2]
    h_a = d // HEAD_A
    zero_shift = jnp.zeros((1, bp, d), x_prompt.dtype)
    zero_wkv = jnp.zeros((1, bp, h_a, HEAD_A, HEAD_A), F32)
    x3_p, p_shift, p_wkv, p_k, p_v = _trunk(x_prompt, zero_shift, zero_wkv, None, None, 0, P)
    x3_s, s_shift, s_wkv, s_k, s_v = _trunk(x_sample, cache_shift_a, state_wkv_a,
                                            cache_k_b, cache_v_b, PAST_LEN, P)
    y_p, y_s = _moe_layer([x3_p, x3_s], P)
    return (y_p.reshape(x_prompt.shape), y_s.reshape(x_sample.shape),
            p_shift, p_wkv, p_k, p_v, s_shift, s_wkv, s_k, s_v)
```

```python
import functools

import jax
import jax.numpy as jnp
from jax import lax
from jax.experimental import pallas as pl
from jax.experimental.pallas import tpu as pltpu

F32 = jnp.float32
BF16 = jnp.bfloat16

DEPTH = 2
HEAD_A = 64
HEAD_B = 64
KV_HEADS = 4
CHUNK = 64
WINDOW = 128
PAST_LEN = 4096
ROPE_DIM = HEAD_B // 4
ROPE_THETA = 500000.0
ATTN_SCALE = HEAD_B ** -0.5
N_EXPERTS = 8
GN_EPS = 64e-5
LN_EPS = 1e-5
ALPHA = (2.0 * DEPTH) ** 0.25

LANES = 128
SUBLANES = 8
VMEM_LIMIT_BYTES = 56 * 1024 * 1024
SCAN_GROUP = 8
SCAN_PAIRS = 2
MOE_TILE = 1024
MOE_ROW_TILE = 256
MOE_SUB_TILE = 256


def _dot(a, b):
    return jnp.dot(a.astype(BF16), b.astype(BF16), preferred_element_type=F32)


def _dot_nt(a, b):
    return lax.dot_general(a.astype(BF16), b.astype(BF16), (((1,), (1,)), ((), ())),
                           preferred_element_type=F32)


def _dot_tn(a, b):
    return lax.dot_general(a.astype(BF16), b.astype(BF16), (((0,), (0,)), ((), ())),
                           preferred_element_type=F32)


def _split(x, n):
    parts = []
    rem = x
    for i in range(n):
        p = rem.astype(BF16)
        parts.append(p)
        if i + 1 < n:
            rem = rem - p.astype(F32)
    return parts


def _dot_exact_rhs(a, b_bf16, n):
    acc = None
    for p in _split(a, n):
        t = jnp.dot(p, b_bf16, preferred_element_type=F32)
        acc = t if acc is None else acc + t
    return acc


def _dot_exact_lhs(a_bf16, b, n):
    acc = None
    for p in _split(b, n):
        t = jnp.dot(a_bf16, p, preferred_element_type=F32)
        acc = t if acc is None else acc + t
    return acc


def _head_sum(x, to_head, from_head):
    return _dot_exact_rhs(_dot_exact_rhs(x, to_head, 2), from_head, 3)


def _sigmoid(z):
    return 1.0 / (1.0 + jnp.exp(-z))


def _layer_norm(z, g, b):
    mu = jnp.mean(z, axis=-1, keepdims=True)
    zc = z - mu
    var = jnp.mean(zc * zc, axis=-1, keepdims=True)
    return zc * lax.rsqrt(var + LN_EPS) * g + b


def _const_spec(shape):
    nd = len(shape)
    return pl.BlockSpec(shape, lambda *_: (0,) * nd)


def _params(sem):
    return pltpu.CompilerParams(dimension_semantics=sem, vmem_limit_bytes=VMEM_LIMIT_BYTES)


def _rwkv_pre_kernel(x_ref, bnd_ref, mu_ref, vec_ref, wr_ref, wk_ref, wv_ref, w1_ref, w2_ref,
                     a1_ref, a2_ref, g1_ref, g2_ref, th_ref, fh_ref,
                     r_out, ld_out, k_out, v_out, kk_out, ka_out, g_out, bonus_out):
    x = x_ref[...]
    rows = lax.broadcasted_iota(jnp.int32, x.shape, 0)
    xp = jnp.where(rows == 0, bnd_ref[0, 0:1, :], pltpu.roll(x, 1, 0))
    dx = xp - x

    def mix(s):
        return x + dx * mu_ref[s:s + 1, :]

    w0 = vec_ref[0:1, :]
    a0 = vec_ref[1:2, :]
    k_k = vec_ref[2:3, :]
    k_a = vec_ref[3:4, :]
    r_k = vec_ref[4:5, :]
    to_head = th_ref[...]
    from_head = fh_ref[...]

    r = _dot(mix(0), wr_ref[...])
    k = _dot(mix(1), wk_ref[...])
    v = _dot(mix(2), wv_ref[...])
    zw = -(w0 + _dot(jnp.tanh(_dot(mix(3), w1_ref[...])), w2_ref[...]))
    softplus = jnp.maximum(zw, 0.0) + jnp.log(1.0 + jnp.exp(-jnp.abs(zw)))
    ld = -jnp.exp(-softplus - 0.5)
    a = _sigmoid(a0 + _dot(_dot(mix(4), a1_ref[...]), a2_ref[...]))
    g = _dot(_sigmoid(_dot(mix(5), g1_ref[...])), g2_ref[...])

    kk = k * k_k
    ss = _head_sum(kk * kk, to_head, from_head)
    kk = kk / jnp.maximum(jnp.sqrt(ss), 1e-12)
    k_h = k * (1.0 + (a - 1.0) * k_a)
    bonus = _head_sum(r * k_h * r_k, to_head, from_head) * v

    r_out[...] = r
    ld_out[...] = ld
    k_out[...] = k_h
    v_out[...] = v
    kk_out[...] = kk
    ka_out[...] = kk * a
    g_out[...] = g
    bonus_out[...] = bonus


def _rwkv_pre(x, bnd, mu8, vec8, wr, wk, wv, w1, w2, a1, a2, g1, g2, to_head, from_head, tm):
    n, d = x.shape
    row = pl.BlockSpec((tm, d), lambda i: (i, 0))
    ins = [row, pl.BlockSpec((1, SUBLANES, d), lambda i: (i, 0, 0))]
    ins += [_const_spec(a.shape)
            for a in (mu8, vec8, wr, wk, wv, w1, w2, a1, a2, g1, g2, to_head, from_head)]
    return pl.pallas_call(
        _rwkv_pre_kernel,
        grid=(n // tm,),
        in_specs=ins,
        out_specs=[row] * 8,
        out_shape=[jax.ShapeDtypeStruct((n, d), F32)] * 8,
        compiler_params=_params(("parallel",)),
        name="rwkv_pre",
    )(x, bnd, mu8, vec8, wr, wk, wv, w1, w2, a1, a2, g1, g2, to_head, from_head)


def _wkv_scan_kernel(r_ref, ld_ref, k_ref, v_ref, kk_ref, ka_ref, s0_ref, o_ref, st_ref, s_scr,
                     *, chunk, n_chunks, group):
    c = chunk
    pw = 2 * HEAD_A
    n_pp = s_scr.shape[0]
    t_idx = pl.program_id(2)

    @pl.when(t_idx == 0)
    def _():
        s_scr[...] = s0_ref[...]

    ri = lax.broadcasted_iota(jnp.int32, (c, c), 0)
    ci = lax.broadcasted_iota(jnp.int32, (c, c), 1)
    tri_incl = ri >= ci
    cum_mat = jnp.where(tri_incl, 1.0, 0.0).astype(BF16)
    ri4 = lax.broadcasted_iota(jnp.int32, (c, 4 * c), 0)
    ci4 = lax.broadcasted_iota(jnp.int32, (c, 4 * c), 1) & (c - 1)
    strict4 = ri4 > ci4
    incl4 = ri4 >= ci4
    first_blk = lax.broadcasted_iota(jnp.int32, (1, 2 * c), 1) < c
    lane1 = lax.broadcasted_iota(jnp.int32, (1, pw), 1)
    head_a1 = lane1 < HEAD_A
    lane2 = lax.broadcasted_iota(jnp.int32, (1, 2 * pw), 1) & (pw - 1)
    head_a2 = lane2 < HEAD_A
    rs = lax.broadcasted_iota(jnp.int32, (pw, pw), 0)
    cs = lax.broadcasted_iota(jnp.int32, (pw, pw), 1)
    same_head = (rs < HEAD_A) == (cs < HEAD_A)
    eye = rs == cs
    zeros_cv = jnp.zeros((c, pw), F32)

    def group_maps(slices):
        each = lambda fn, *lists: [fn(*a) for a in zip(*lists)]
        ld = [ld_ref[ix] for ix in slices]
        cw = each(lambda x: _dot_exact_lhs(cum_mat, x, 3), ld)
        w_in = each(jnp.exp, cw)
        w_ex = each(lambda a, b: jnp.exp(a - b), cw, ld)
        w_inv = each(lambda a: jnp.exp(-a), cw)
        w_last = each(lambda a: a[c - 1:c, :], w_in)
        knt = [-(kk_ref[ix] * w) for ix, w in zip(slices, w_ex)]
        kat = [ka_ref[ix] * w for ix, w in zip(slices, w_inv)]
        kt = [k_ref[ix] * w for ix, w in zip(slices, w_inv)]
        rt = [r_ref[ix] * w for ix, w in zip(slices, w_in)]
        v = [v_ref[ix] for ix in slices]
        by_head = lambda z, is_a: jnp.concatenate(
            [jnp.where(is_a, z, 0.0), jnp.where(is_a, 0.0, z)], axis=0)
        lh = each(lambda a, b: jnp.concatenate([a, b], axis=0), knt, rt)
        rh = each(lambda a, b: jnp.concatenate(
            [by_head(a, head_a1), by_head(b, head_a1)], axis=0), kat, kt)
        v_st = each(lambda a: by_head(a, head_a1), v)

        full = each(_dot_nt, lh, rh)
        top = each(lambda a: jnp.where(strict4, a[:c, :], 0.0), full)
        bot = each(lambda a: jnp.where(incl4, a[c:, :], 0.0), full)
        n_cat = each(lambda a: a[:, :2 * c], top)
        akv = each(lambda a, b: _dot(a[:, 2 * c:], b), top, v_st)

        x = each(lambda a, b: jnp.concatenate([a, b], axis=1), knt, akv)
        span = 1
        while span < c:
            dx = each(lambda n, x0: _dot(n, by_head(x0, head_a2)), n_cat, x)
            span *= 2
            if span < c:
                n_cat = each(lambda n: _dot(n, by_head(n, first_blk)), n_cat)
            x = each(lambda x0, d: x0 + d, x, dx)

        v_wide = each(lambda a: jnp.concatenate([jnp.zeros_like(a), a], axis=1), v_st)
        qo_all = each(lambda b4, x0, vw: _dot(b4, jnp.concatenate([by_head(x0, head_a2), vw], axis=0)),
                      bot, x, v_wide)
        rhs2 = each(lambda a, b: jnp.concatenate(
            [a, jnp.concatenate([zeros_cv, b], axis=1)], axis=0), x, v)
        lt = each(lambda a, b, w: jnp.concatenate([a * w, b * w], axis=0), kat, kt, w_last)
        mb = each(_dot_tn, lt, rhs2)
        out = []
        for j in range(len(slices)):
            qo = qo_all[j]
            q = rt[j] + qo[:, :pw]
            m = jnp.where(eye, w_last[j], 0.0) + jnp.where(same_head, mb[j][:, :pw], 0.0)
            b = jnp.where(same_head, mb[j][:, pw:], 0.0)
            mq = jnp.concatenate([m, q], axis=0)
            out.append((mq.astype(BF16), b, qo[:, pw:]))
        return out

    def body(it, carry):
        items = [(j, pp) for j in range(group) for pp in range(n_pp)]
        slices = [(pl.ds(pl.multiple_of((it * group + j) * c, c), c), pl.ds(pp * pw, pw))
                  for j, pp in items]
        maps = group_maps(slices)
        s = [s_scr[pp] for pp in range(n_pp)]
        for (j, pp), ix, (mq, b, o0) in zip(items, slices, maps):
            res = _dot(mq, s[pp])
            s[pp] = res[:pw, :] + b
            o_ref[ix] = res[pw:, :] + o0
        for pp in range(n_pp):
            s_scr[pp] = s[pp]
        return carry

    lax.fori_loop(0, n_chunks // group, body, 0)

    @pl.when(t_idx == pl.num_programs(2) - 1)
    def _():
        st_ref[...] = s_scr[...]


def _wkv_scan(r, ld, k, v, kk, ka, s0, chunk, t_blk):
    bn, t, d = r.shape
    pw = 2 * HEAD_A
    n_pairs = d // pw
    n_pp = SCAN_PAIRS
    seq = pl.BlockSpec((None, t_blk, n_pp * pw), lambda b, p, i: (b, i, p))
    st = pl.BlockSpec((None, n_pp, pw, pw), lambda b, p, i: (b, p, 0, 0))
    n_chunks = t_blk // chunk
    kern = functools.partial(_wkv_scan_kernel, chunk=chunk, n_chunks=n_chunks,
                             group=min(SCAN_GROUP, n_chunks))
    return pl.pallas_call(
        kern,
        grid=(bn, n_pairs // n_pp, t // t_blk),
        in_specs=[seq] * 6 + [st],
        out_specs=[seq, st],
        out_shape=[jax.ShapeDtypeStruct((bn, t, d), F32),
                   jax.ShapeDtypeStruct((bn, n_pairs, pw, pw), F32)],
        scratch_shapes=[pltpu.VMEM((n_pp, pw, pw), F32)],
        compiler_params=_params(("parallel", "parallel", "arbitrary")),
        name="wkv_scan",
    )(r, ld, k, v, kk, ka, s0)


def _rwkv_post_kernel(o_ref, bonus_ref, g_ref, x_ref, vec_ref, th_ref, fh_ref, wo_ref, out_ref):
    o = o_ref[...]
    to_head = th_ref[...]
    from_head = fh_ref[...]
    inv_n = 1.0 / HEAD_A
    mean = _head_sum(o, to_head, from_head) * inv_n
    oc = o - mean
    var = _head_sum(oc * oc, to_head, from_head) * inv_n
    on = oc * lax.rsqrt(var + GN_EPS) * vec_ref[0:1, :] + vec_ref[1:2, :]
    y = (on + bonus_ref[...]) * g_ref[...]
    h = _dot(y, wo_ref[...])
    out_ref[...] = _layer_norm(ALPHA * x_ref[...] + h, vec_ref[2:3, :], vec_ref[3:4, :])


def _rwkv_post(o, bonus, g, x, vec8, to_head, from_head, wo, tm):
    n, d = x.shape
    row = pl.BlockSpec((tm, d), lambda i: (i, 0))
    return pl.pallas_call(
        _rwkv_post_kernel,
        grid=(n // tm,),
        in_specs=[row] * 4 + [_const_spec(a.shape) for a in (vec8, to_head, from_head, wo)],
        out_specs=row,
        out_shape=jax.ShapeDtypeStruct((n, d), F32),
        compiler_params=_params(("parallel",)),
        name="rwkv_post",
    )(o, bonus, g, x, vec8, to_head, from_head, wo)


def _ffn_kernel(x_ref, wg_ref, wu_ref, wd_ref, ln_ref, out_ref, acc_ref, xb_ref):
    f = pl.program_id(1)

    @pl.when(f == 0)
    def _():
        acc_ref[...] = jnp.zeros_like(acc_ref)
        xb_ref[...] = x_ref[...].astype(BF16)

    xb = xb_ref[...]
    gate = jnp.dot(xb, wg_ref[...].astype(BF16), preferred_element_type=F32)
    up = jnp.dot(xb, wu_ref[...].astype(BF16), preferred_element_type=F32)
    h = gate * _sigmoid(gate) * up
    acc_ref[...] += jnp.dot(h.astype(BF16), wd_ref[...].astype(BF16), preferred_element_type=F32)

    @pl.when(f == pl.num_programs(1) - 1)
    def _():
        out_ref[...] = _layer_norm(ALPHA * x_ref[...] + acc_ref[...], ln_ref[0:1, :], ln_ref[1:2, :])


def _ffn(x, w_gu, w_down, ln8, tm, tf):
    n, d = x.shape
    d_ff = w_gu.shape[1] // 2
    nf = d_ff // tf
    return pl.pallas_call(
        _ffn_kernel,
        grid=(n // tm, nf),
        in_specs=[
            pl.BlockSpec((tm, d), lambda i, f: (i, 0)),
            pl.BlockSpec((d, tf), lambda i, f: (0, f)),
            pl.BlockSpec((d, tf), lambda i, f: (0, nf + f)),
            pl.BlockSpec((tf, d), lambda i, f: (f, 0)),
            _const_spec(ln8.shape),
        ],
        out_specs=pl.BlockSpec((tm, d), lambda i, f: (i, 0)),
        out_shape=jax.ShapeDtypeStruct((n, d), F32),
        scratch_shapes=[pltpu.VMEM((tm, d), F32), pltpu.VMEM((tm, d), BF16)],
        compiler_params=_params(("parallel", "arbitrary")),
        name="dense_ffn",
    )(x, w_gu, w_gu, w_down, ln8)


ROUTE_I1, ROUTE_I2, ROUTE_G1, ROUTE_G2, ROUTE_P1, ROUTE_P2 = range(6)


def _router_kernel(x_ref, w_ref, route_ref, count_ref, carry_ref):
    i = pl.program_id(0)

    @pl.when(i == 0)
    def _():
        carry_ref[...] = jnp.zeros_like(carry_ref)

    x_hi, x_lo = _split(x_ref[...], 2)
    w_hi, w_lo = _split(w_ref[...], 2)
    logits = (jnp.dot(x_hi, w_hi, preferred_element_type=F32)
              + jnp.dot(x_lo, w_hi, preferred_element_type=F32)
              + jnp.dot(x_hi, w_lo, preferred_element_type=F32))
    tm = logits.shape[0]
    lane = lax.broadcasted_iota(jnp.int32, logits.shape, 1).astype(F32)
    neg = -jnp.inf
    logits = jnp.where(lane < N_EXPERTS, logits, neg)
    m1 = jnp.max(logits, axis=-1, keepdims=True)
    i1 = jnp.min(jnp.where(logits == m1, lane, float(LANES)), axis=-1, keepdims=True)
    rest = jnp.where(lane == i1, neg, logits)
    m2 = jnp.max(rest, axis=-1, keepdims=True)
    i2 = jnp.min(jnp.where(rest == m2, lane, float(LANES)), axis=-1, keepdims=True)
    e2 = jnp.exp(m2 - m1)
    den = 1.0 + e2

    sel1 = lane == i1
    sel2 = lane == i2
    onehot = jnp.where(sel1, 1.0, 0.0) + jnp.where(sel2, 1.0, 0.0)
    ri = lax.broadcasted_iota(jnp.int32, (tm, tm), 0)
    ci = lax.broadcasted_iota(jnp.int32, (tm, tm), 1)
    earlier = jnp.where(ri > ci, 1.0, 0.0).astype(BF16)
    rank = carry_ref[0:1, :] + jnp.dot(earlier, onehot.astype(BF16), preferred_element_type=F32)
    p1 = jnp.sum(jnp.where(sel1, rank, 0.0), axis=-1, keepdims=True)
    p2 = jnp.sum(jnp.where(sel2, rank, 0.0), axis=-1, keepdims=True)
    carry_ref[0:1, :] = carry_ref[0:1, :] + jnp.sum(onehot, axis=0, keepdims=True)

    route = jnp.zeros_like(logits)
    for col, val in ((ROUTE_I1, i1), (ROUTE_I2, i2), (ROUTE_G1, 1.0 / den), (ROUTE_G2, e2 / den),
                     (ROUTE_P1, p1), (ROUTE_P2, p2)):
        route = jnp.where(lane == float(col), val, route)
    route_ref[...] = route
    count_ref[...] = carry_ref[...]


def _router(x, w_pad, tm):
    n, d = x.shape
    return pl.pallas_call(
        _router_kernel,
        grid=(n // tm,),
        in_specs=[pl.BlockSpec((tm, d), lambda i: (i, 0)), _const_spec(w_pad.shape)],
        out_specs=[pl.BlockSpec((tm, LANES), lambda i: (i, 0)), _const_spec((SUBLANES, LANES))],
        out_shape=[jax.ShapeDtypeStruct((n, LANES), F32), jax.ShapeDtypeStruct((SUBLANES, LANES), F32)],
        scratch_shapes=[pltpu.VMEM((SUBLANES, LANES), F32)],
        compiler_params=_params(("arbitrary",)),
        name="router",
    )(x, w_pad)


def _row_copy(src_ref, src_row, dst_ref, dst_row, sem):
    return pltpu.make_async_copy(src_ref.at[pl.ds(src_row, 1), :], dst_ref.at[pl.ds(dst_row, 1), :], sem)


def _dispatch_kernel(slot_ref, x_ref, xs_in_ref, xs_ref, sem):
    del xs_in_ref
    tm = x_ref.shape[0]

    def start(r, carry):
        _row_copy(x_ref, r, xs_ref, slot_ref[0, r], sem).start()
        _row_copy(x_ref, r, xs_ref, slot_ref[0, tm + r], sem).start(priority=1)
        return carry

    def wait(r, carry):
        _row_copy(x_ref, 0, xs_ref, 0, sem).wait()
        _row_copy(x_ref, 0, xs_ref, 0, sem).wait()
        return carry

    lax.fori_loop(0, tm, start, 0)
    lax.fori_loop(0, tm, wait, 0)


def _dispatch(slots, x, xs, tm):
    n, d = x.shape
    return pl.pallas_call(
        _dispatch_kernel,
        grid=(n // tm,),
        in_specs=[pl.BlockSpec((None, 1, 2 * tm), lambda i: (i, 0, 0), memory_space=pltpu.SMEM),
                  pl.BlockSpec((tm, d), lambda i: (i, 0)),
                  pl.BlockSpec(memory_space=pl.ANY)],
        out_specs=pl.BlockSpec(memory_space=pl.ANY),
        out_shape=jax.ShapeDtypeStruct(xs.shape, xs.dtype),
        scratch_shapes=[pltpu.SemaphoreType.DMA(())],
        input_output_aliases={2: 0},
        compiler_params=_params(("arbitrary",)),
        name="moe_dispatch",
    )(slots, x, xs)


def _experts_kernel(te_ref, rows_ref, x_ref, wg_ref, wu_ref, wd_ref, out_ref, acc_ref, xb_ref):
    del te_ref
    f = pl.program_id(1)
    rows = rows_ref[pl.program_id(0)]
    tm = x_ref.shape[0]

    @pl.when(f == 0)
    def _():
        acc_ref[...] = jnp.zeros_like(acc_ref)

    for sub in range(tm // MOE_SUB_TILE):
        sl = pl.ds(sub * MOE_SUB_TILE, MOE_SUB_TILE)

        @pl.when(sub * MOE_SUB_TILE < rows)
        def _(sl=sl):
            @pl.when(f == 0)
            def _():
                xb_ref[sl, :] = x_ref[sl, :].astype(BF16)

            xb = xb_ref[sl, :]
            gate = jnp.dot(xb, wg_ref[...].astype(BF16), preferred_element_type=F32)
            up = jnp.dot(xb, wu_ref[...].astype(BF16), preferred_element_type=F32)
            h = gate * _sigmoid(gate) * up
            acc_ref[sl, :] += jnp.dot(h.astype(BF16), wd_ref[...].astype(BF16),
                                      preferred_element_type=F32)

    @pl.when(f == pl.num_programs(1) - 1)
    def _():
        out_ref[...] = acc_ref[...]


def _experts(tile_expert, tile_rows, xs, w_gu, w_down, tm, tf):
    s_total, d = xs.shape
    d_ff = w_gu.shape[2] // 2
    nf = d_ff // tf

    def f_eff(i, f, rows):
        return jnp.where(rows[i] > 0, f, nf - 1)

    grid_spec = pltpu.PrefetchScalarGridSpec(
        num_scalar_prefetch=2,
        grid=(s_total // tm, nf),
        in_specs=[
            pl.BlockSpec((tm, d), lambda i, f, te, rows: (i, 0)),
            pl.BlockSpec((None, d, tf), lambda i, f, te, rows: (te[i], 0, f_eff(i, f, rows))),
            pl.BlockSpec((None, d, tf), lambda i, f, te, rows: (te[i], 0, nf + f_eff(i, f, rows))),
            pl.BlockSpec((None, tf, d), lambda i, f, te, rows: (te[i], f_eff(i, f, rows), 0)),
        ],
        out_specs=pl.BlockSpec((tm, d), lambda i, f, te, rows: (i, 0)),
        scratch_shapes=[pltpu.VMEM((tm, d), F32), pltpu.VMEM((tm, d), BF16)],
    )
    return pl.pallas_call(
        _experts_kernel,
        grid_spec=grid_spec,
        out_shape=jax.ShapeDtypeStruct((s_total, d), F32),
        compiler_params=_params(("arbitrary", "arbitrary")),
        name="moe_experts",
    )(tile_expert, tile_rows, xs, w_gu, w_gu, w_down)


def _combine_kernel(slot_ref, route_ref, x_ref, ys_ref, ln_ref, out_ref, buf1, buf2, sem):
    tm = x_ref.shape[0]

    def start(r, carry):
        _row_copy(ys_ref, slot_ref[0, r], buf1, r, sem).start()
        _row_copy(ys_ref, slot_ref[0, tm + r], buf2, r, sem).start(priority=1)
        return carry

    def wait(r, carry):
        _row_copy(ys_ref, 0, buf1, 0, sem).wait()
        _row_copy(ys_ref, 0, buf2, 0, sem).wait()
        return carry

    lax.fori_loop(0, tm, start, 0)
    lax.fori_loop(0, tm, wait, 0)
    route = route_ref[...]
    g1 = route[:, ROUTE_G1:ROUTE_G1 + 1]
    g2 = route[:, ROUTE_G2:ROUTE_G2 + 1]
    y = g1 * buf1[...] + g2 * buf2[...]
    out_ref[...] = _layer_norm(ALPHA * x_ref[...] + y, ln_ref[0:1, :], ln_ref[1:2, :])


def _combine(slots, route, x, ys, ln8, tm):
    n, d = x.shape
    return pl.pallas_call(
        _combine_kernel,
        grid=(n // tm,),
        in_specs=[pl.BlockSpec((None, 1, 2 * tm), lambda i: (i, 0, 0), memory_space=pltpu.SMEM),
                  pl.BlockSpec((tm, LANES), lambda i: (i, 0)),
                  pl.BlockSpec((tm, d), lambda i: (i, 0)),
                  pl.BlockSpec(memory_space=pl.ANY),
                  _const_spec(ln8.shape)],
        out_specs=pl.BlockSpec((tm, d), lambda i: (i, 0)),
        out_shape=jax.ShapeDtypeStruct((n, d), F32),
        scratch_shapes=[pltpu.VMEM((tm, d), F32), pltpu.VMEM((tm, d), F32), pltpu.SemaphoreType.DMA(())],
        compiler_params=_params(("arbitrary",)),
        name="moe_combine",
    )(slots, route, x, ys, ln8)


def _rope(y, cos_t, sin_next, sin_prev):
    n = y.shape[1]
    reps = n // LANES
    tile = lambda t: jnp.concatenate([t] * reps, axis=1) if reps > 1 else t
    half = ROPE_DIM // 2
    return (y * tile(cos_t)
            + pltpu.roll(y, n - half, 1) * tile(sin_next)
            + pltpu.roll(y, half, 1) * tile(sin_prev))


def _kv_kernel(x_ref, w_ref, cos_ref, sn_ref, sp_ref, k_out, v_out):
    kv = _dot(x_ref[...], w_ref[...])
    nk = k_out.shape[1]
    k_out[...] = _rope(kv[:, :nk], cos_ref[...], sn_ref[...], sp_ref[...])
    v_out[...] = kv[:, nk:]


def _q_kernel(x_ref, w_ref, cos_ref, sn_ref, sp_ref, q_out):
    q_out[...] = _rope(_dot(x_ref[...], w_ref[...]), cos_ref[...], sn_ref[...], sp_ref[...])


def _rope_proj(kernel_fn, x, w, tables, tm, out_widths, name):
    n, d = x.shape
    t_tiles = tables[0].shape[0] // tm
    row = pl.BlockSpec((tm, d), lambda i: (i, 0))
    tab = pl.BlockSpec((tm, LANES), lambda i: (i % t_tiles, 0))
    return pl.pallas_call(
        kernel_fn,
        grid=(n // tm,),
        in_specs=[row, _const_spec(w.shape), tab, tab, tab],
        out_specs=[pl.BlockSpec((tm, ow), lambda i: (i, 0)) for ow in out_widths],
        out_shape=[jax.ShapeDtypeStruct((n, ow), F32) for ow in out_widths],
        compiler_params=_params(("parallel",)),
        name=name,
    )(x, w, *tables)


def _attn_kernel(sink_ref, q_ref, kp_ref, kc_ref, vp_ref, vc_ref, o_ref, *, banded):
    tq = q_ref.shape[0]
    n_prev = kp_ref.shape[0]
    tk = n_prev + kc_ref.shape[0]
    q = q_ref[...] * ATTN_SCALE
    kband = jnp.concatenate([kp_ref[...], kc_ref[...]], axis=0)
    vband = jnp.concatenate([vp_ref[...], vc_ref[...]], axis=0)
    n_heads = q.shape[1] // HEAD_B
    group = n_heads // KV_HEADS
    qc = CHUNK if banded else tq
    kc = WINDOW + CHUNK if banded else tk
    units = [(kh, ci) for kh in range(KV_HEADS) for ci in range(tq // qc)]
    if banded:
        kj = lax.broadcasted_iota(jnp.int32, (kc, group * qc), 0)
        band_start = pl.program_id(1) * tq - n_prev

    qlane = lax.broadcasted_iota(jnp.int32, (1, group * qc), 1)

    def scores(kh, ci):
        qs = jnp.concatenate([q[ci * qc:(ci + 1) * qc, (kh * group + j) * HEAD_B:(kh * group + j + 1) * HEAD_B]
                              for j in range(group)], axis=0)
        return _dot_nt(kband[ci * qc:ci * qc + kc, kh * HEAD_B:(kh + 1) * HEAD_B], qs)

    def softmax(kh, ci, s):
        sk = jnp.full((1, group * qc), sink_ref[kh * group], F32)
        for j in range(1, group):
            sk = jnp.where(qlane >= j * qc, sink_ref[kh * group + j], sk)
        if banded:
            s = jnp.where(band_start + ci * qc + kj >= 0, s, -jnp.inf)
        m = jnp.maximum(jnp.max(s, axis=0, keepdims=True), sk)
        p = jnp.exp(s - m)
        return p * (1.0 / (jnp.sum(p, axis=0, keepdims=True) + jnp.exp(sk - m)))

    def weighted(kh, ci, p):
        return _dot_tn(p, vband[ci * qc:ci * qc + kc, kh * HEAD_B:(kh + 1) * HEAD_B])

    chunks = range(tq // qc)
    outs = {}
    s_next = [scores(0, ci) for ci in chunks]
    for kh in range(KV_HEADS):
        s_cur = s_next
        if kh + 1 < KV_HEADS:
            s_next = [scores(kh + 1, ci) for ci in chunks]
        probs = [softmax(kh, ci, s_cur[ci]) for ci in chunks]
        for ci in chunks:
            o = weighted(kh, ci, probs[ci])
            for j in range(group):
                outs[(kh * group + j, ci)] = o[j * qc:(j + 1) * qc, :]
    o_ref[...] = jnp.concatenate(
        [jnp.concatenate([outs[(h, ci)] for ci in range(tq // qc)], axis=0) for h in range(n_heads)],
        axis=1)


def _attention(q, k_prev_src, k_cur_src, v_prev_src, v_cur_src, sinks, tq, banded):
    bn, t, d = q.shape
    kw = k_cur_src.shape[2]
    if banded:
        ratio = tq // WINDOW
        prev_map = lambda b, i: (b, jnp.maximum(i * ratio - 1, 0), 0)
    else:
        prev_map = lambda b, i: (b, 0, 0)
    prev = pl.BlockSpec((None, WINDOW, kw), prev_map)
    cur = pl.BlockSpec((None, tq, kw), lambda b, i: (b, i, 0))
    kern = functools.partial(_attn_kernel, banded=banded)
    return pl.pallas_call(
        kern,
        grid=(bn, t // tq),
        in_specs=[pl.BlockSpec(memory_space=pltpu.SMEM),
                  pl.BlockSpec((None, tq, d), lambda b, i: (b, i, 0)), prev, cur, prev, cur],
        out_specs=pl.BlockSpec((None, tq, d), lambda b, i: (b, i, 0)),
        out_shape=jax.ShapeDtypeStruct((bn, t, d), F32),
        compiler_params=_params(("parallel", "parallel")),
        name="swa_attn",
    )(sinks, q, k_prev_src, k_cur_src, v_prev_src, v_cur_src)


def _proj_ln_kernel(y_ref, x_ref, w_ref, ln_ref, out_ref):
    h = _dot(y_ref[...], w_ref[...])
    out_ref[...] = _layer_norm(ALPHA * x_ref[...] + h, ln_ref[0:1, :], ln_ref[1:2, :])


def _proj_ln(y, x, w, ln8, tm):
    n, d = x.shape
    return pl.pallas_call(
        _proj_ln_kernel,
        grid=(n // tm,),
        in_specs=[pl.BlockSpec((tm, y.shape[1]), lambda i: (i, 0)), pl.BlockSpec((tm, d), lambda i: (i, 0)),
                  _const_spec(w.shape), _const_spec(ln8.shape)],
        out_specs=pl.BlockSpec((tm, d), lambda i: (i, 0)),
        out_shape=jax.ShapeDtypeStruct((n, d), F32),
        compiler_params=_params(("parallel",)),
        name="proj_ln",
    )(y, x, w, ln8)


def _moe_layer(xs_rows, P):
    d = xs_rows[0].shape[1]
    tm_e = MOE_TILE
    router_pad = jnp.concatenate([P['moe_router'][0], jnp.zeros((d, LANES - N_EXPERTS), F32)], axis=1)
    routed = [_router(x, router_pad, min(MOE_ROW_TILE, x.shape[0])) for x in xs_rows]
    counts = [c[0, :N_EXPERTS].astype(jnp.int32) for _, c in routed]
    total = sum(counts)
    padded = ((total + tm_e - 1) // tm_e) * tm_e
    ends = jnp.cumsum(padded)
    starts = ends - padded
    n_assign = 2 * sum(x.shape[0] for x in xs_rows)
    n_tiles = (n_assign + N_EXPERTS * (tm_e - 1)) // tm_e
    n_used = (ends[-1] // tm_e).astype(jnp.int32)
    tile_expert = jnp.sum((jnp.arange(n_tiles) * tm_e)[:, None] >= ends[None, :], axis=1).astype(jnp.int32)
    tile_expert = jnp.minimum(tile_expert, N_EXPERTS - 1)
    tile_expert = jnp.where(jnp.arange(n_tiles) < n_used, tile_expert,
                            tile_expert[jnp.maximum(n_used - 1, 0)])
    tile_rows = jnp.clip((starts + total)[tile_expert] - jnp.arange(n_tiles) * tm_e, 0, tm_e)
    tile_rows = jnp.where(jnp.arange(n_tiles) < n_used, tile_rows, 0).astype(jnp.int32)

    slots = []
    base = starts
    for (route, _), cnt, x in zip(routed, counts, xs_rows):
        tm = min(MOE_ROW_TILE, x.shape[0])
        i1 = route[:, ROUTE_I1].astype(jnp.int32)
        i2 = route[:, ROUTE_I2].astype(jnp.int32)
        s1 = base[i1] + route[:, ROUTE_P1].astype(jnp.int32)
        s2 = base[i2] + route[:, ROUTE_P2].astype(jnp.int32)
        slots.append(jnp.concatenate([s1.reshape(-1, 1, tm), s2.reshape(-1, 1, tm)], axis=2))
        base = base + cnt

    xs = jnp.zeros((n_tiles * tm_e, d), F32)
    for sl, x in zip(slots, xs_rows):
        xs = _dispatch(sl, x, xs, min(MOE_ROW_TILE, x.shape[0]))
    ys = _experts(tile_expert, tile_rows, xs, P['moe_w_gu'][0], P['moe_w_down'][0], tm_e, 512)
    ln11 = _pad_rows([P['ln_g'][1, 1], P['ln_b'][1, 1]], d)
    return [_combine(sl, route, x, ys, ln11, min(MOE_ROW_TILE, x.shape[0]))
            for sl, (route, _), x in zip(slots, routed, xs_rows)]


def _pad_rows(rows, d):
    a = jnp.stack(rows).astype(F32)
    return jnp.concatenate([a, jnp.zeros((SUBLANES - a.shape[0], d), F32)], axis=0)


def _rope_tables(pos, reps):
    inv_freq = ROPE_THETA ** (-jnp.arange(0, ROPE_DIM, 2, dtype=jnp.float32) / ROPE_DIM)
    ang = pos.astype(jnp.float32)[:, None] * inv_freq[None, :]
    cos = jnp.cos(ang)
    sin = jnp.sin(ang)
    t = pos.shape[0]
    half = ROPE_DIM // 2
    rest = HEAD_B - ROPE_DIM
    z_half = jnp.zeros((t, half), F32)
    z_rest = jnp.zeros((t, rest), F32)
    cos_h = jnp.concatenate([cos, cos, jnp.ones((t, rest), F32)], axis=1)
    sn_h = jnp.concatenate([-sin, z_half, z_rest], axis=1)
    sp_h = jnp.concatenate([z_half, sin, z_rest], axis=1)
    per_tile = LANES // HEAD_B
    return tuple(jnp.tile(a, (reps, per_tile)) for a in (cos_h, sn_h, sp_h))


def _trunk(x, shift_in, wkv_in, k_cache, v_cache, pos0, P):
    bn, t, d = x.shape
    n = bn * t
    h_a = d // HEAD_A
    pw = 2 * HEAD_A
    n_pairs = d // pw
    xf = x.reshape(n, d)

    tm_pre = min(256, t)
    tm_row = min(256, n)
    tm_ffn = min(1024, n)
    chunk = min(CHUNK, t)
    t_blk = min(512, t)

    tiles = jnp.arange(n // tm_pre) * tm_pre
    prev_rows = xf[jnp.maximum(tiles - 1, 0)]
    start_rows = shift_in[0][tiles // t]
    bnd = jnp.where(((tiles % t) == 0)[:, None], start_rows, prev_rows)
    bnd = jnp.broadcast_to(bnd[:, None, :], (n // tm_pre, SUBLANES, d))

    mu8 = jnp.concatenate([P['a_mu'][0], jnp.zeros((2, d), F32)], axis=0)
    vec_pre = _pad_rows([P['a_w0'][0], P['a_a0'][0], P['a_k_k'][0], P['a_k_a'][0],
                         P['a_r_k'][0].reshape(d)], d)
    lane_head = jnp.arange(d) // HEAD_A
    to_head = (lane_head[:, None] == jnp.arange(LANES)[None, :]).astype(BF16)
    from_head = to_head.T
    bf = lambda a: a.astype(BF16)
    w_rkv = P['a_w_rkv'][0]
    r, ld, k_h, v, kk, ka, g, bonus = _rwkv_pre(
        xf, bnd, mu8, vec_pre, bf(w_rkv[0]), bf(w_rkv[1]), bf(w_rkv[2]),
        bf(P['a_w1'][0]), bf(P['a_w2'][0]), bf(P['a_a1'][0]), bf(P['a_a2'][0]),
        bf(P['a_g1'][0]), bf(P['a_g2'][0]), to_head, from_head, tm_pre)

    st_t = jnp.swapaxes(wkv_in[0].astype(F32), -1, -2).reshape(bn, n_pairs, 2, HEAD_A, HEAD_A)
    eye2 = jnp.eye(2, dtype=F32)
    s0 = jnp.einsum('bpajk,ac->bpajck', st_t, eye2).reshape(bn, n_pairs, pw, pw)
    seq3 = lambda a: a.reshape(bn, t, d)
    o, s_fin = _wkv_scan(seq3(r), seq3(ld), seq3(k_h), seq3(v), seq3(kk), seq3(ka), s0, chunk, t_blk)
    s_fin = s_fin.reshape(bn, n_pairs, 2, HEAD_A, 2, HEAD_A)
    s_fin = jnp.stack([s_fin[:, :, 0, :, 0, :], s_fin[:, :, 1, :, 1, :]], axis=2)
    wkv_out = jnp.swapaxes(s_fin, -1, -2).reshape(bn, h_a, HEAD_A, HEAD_A)
    shift_out = x[:, -1]

    vec_post = _pad_rows([P['a_lnx_g'][0], P['a_lnx_b'][0], P['ln_g'][0, 0], P['ln_b'][0, 0]], d)
    x1 = _rwkv_post(o.reshape(n, d), bonus, g, xf, vec_post, to_head, from_head, bf(P['a_w_o'][0]), tm_row)

    ln01 = _pad_rows([P['ln_g'][0, 1], P['ln_b'][0, 1]], d)
    x2 = _ffn(x1, P['ffn_w_gu'][0], P['ffn_w_down'][0], ln01, tm_ffn, 512)

    pos = pos0 + jnp.arange(t, dtype=jnp.int32)
    tables = _rope_tables(pos, max(tm_row // t, 1))
    kvw = KV_HEADS * HEAD_B
    k_new, v_new = _rope_proj(_kv_kernel, x2, bf(P['kv_w']), tables, tm_row, (kvw, kvw), "kv_proj")
    k_new = k_new.reshape(bn, t, kvw)
    v_new = v_new.reshape(bn, t, kvw)
    (q,) = _rope_proj(_q_kernel, x2, bf(P['b_w_q'][0]), tables, tm_row, (d,), "q_proj")
    q = q.reshape(bn, t, d)
    sinks = P['b_sinks'][0].astype(F32)
    if k_cache is None:
        att = _attention(q, k_new, k_new, v_new, v_new, sinks, min(256, t), banded=True)
        k_out = k_new[:, -WINDOW:]
        v_out = v_new[:, -WINDOW:]
    else:
        kc = k_cache.astype(F32).reshape(bn, WINDOW, kvw)
        vc = v_cache.astype(F32).reshape(bn, WINDOW, kvw)
        att = _attention(q, kc, k_new, vc, v_new, sinks, t, banded=False)
        k_out = jnp.concatenate([kc, k_new], axis=1)[:, -WINDOW:]
        v_out = jnp.concatenate([vc, v_new], axis=1)[:, -WINDOW:]
    ln10 = _pad_rows([P['ln_g'][1, 0], P['ln_b'][1, 0]], d)
    x3 = _proj_ln(att.reshape(n, d), x2, bf(P['b_w_o'][0]), ln10, tm_row)

    return (x3, shift_out[None], wkv_out[None],
            k_out.reshape(bn, WINDOW, KV_HEADS, HEAD_B), v_out.reshape(bn, WINDOW, KV_HEADS, HEAD_B))


def kernel(x_prompt, x_sample, cache_shift_a, state_wkv_a, cache_k_b, cache_v_b, a_mu, a_w_rkv, a_w0, a_w1, a_w2, a_a0, a_a1, a_a2, a_g1, a_g2, a_k_k, a_k_a, a_r_k, a_lnx_g, a_lnx_b, a_w_o, kv_w, b_w_q, b_sinks, b_w_o, ln_g, ln_b, ffn_w_gu, ffn_w_down, moe_router, moe_w_gu, moe_w_down):
    P = {
        'a_mu': a_mu, 'a_w_rkv': a_w_rkv, 'a_w0': a_w0, 'a_w1': a_w1, 'a_w2': a_w2,
        'a_a0': a_a0, 'a_a1': a_a1, 'a_a2': a_a2, 'a_g1': a_g1, 'a_g2': a_g2,
        'a_k_k': a_k_k, 'a_k_a': a_k_a, 'a_r_k': a_r_k, 'a_lnx_g': a_lnx_g,
        'a_lnx_b': a_lnx_b, 'a_w_o': a_w_o, 'kv_w': kv_w, 'b_w_q': b_w_q,
        'b_sinks': b_sinks, 'b_w_o': b_w_o, 'ln_g': ln_g, 'ln_b': ln_b,
        'ffn_w_gu': ffn_w_gu, 'ffn_w_down': ffn_w_down, 'moe_router': moe_router,
        'moe_w_gu': moe_w_gu, 'moe_w_down': moe_w_down,
    }
    bp = x_prompt.shape[0]
    d = x_prompt.shape[2]
    h_a = d // HEAD_A
    zero_shift = jnp.zeros((1, bp, d), x_prompt.dtype)
    zero_wkv = jnp.zeros((1, bp, h_a, HEAD_A, HEAD_A), F32)
    x3_p, p_shift, p_wkv, p_k, p_v = _trunk(x_prompt, zero_shift, zero_wkv, None, None, 0, P)
    x3_s, s_shift, s_wkv, s_k, s_v = _trunk(x_sample, cache_shift_a, state_wkv_a,
                                            cache_k_b, cache_v_b, PAST_LEN, P)
    y_p, y_s = _moe_layer([x3_p, x3_s], P)
    return (y_p.reshape(x_prompt.shape), y_s.reshape(x_sample.shape),
            p_shift, p_wkv, p_k, p_v, s_shift, s_wkv, s_k, s_v)
```

```python
import functools

import jax
import jax.numpy as jnp
from jax import lax
from jax.experimental import pallas as pl
from jax.experimental.pallas import tpu as pltpu

F32 = jnp.float32
BF16 = jnp.bfloat16

DEPTH = 2
HEAD_A = 64
HEAD_B = 64
KV_HEADS = 4
CHUNK = 64
WINDOW = 128
PAST_LEN = 4096
ROPE_DIM = HEAD_B // 4
ROPE_THETA = 500000.0
ATTN_SCALE = HEAD_B ** -0.5
N_EXPERTS = 8
GN_EPS = 64e-5
LN_EPS = 1e-5
ALPHA = (2.0 * DEPTH) ** 0.25

LANES = 128
SUBLANES = 8
VMEM_LIMIT_BYTES = 56 * 1024 * 1024
SCAN_GROUP = 8
SCAN_PAIRS = 2
MOE_TILE = 1024
MOE_ROW_TILE = 512
MOE_TILE_PARTS = 2
SLAB_UNIT = SUBLANES


def _dot(a, b):
    return jnp.dot(a.astype(BF16), b.astype(BF16), preferred_element_type=F32)


def _dot_nt(a, b):
    return lax.dot_general(a.astype(BF16), b.astype(BF16), (((1,), (1,)), ((), ())),
                           preferred_element_type=F32)


def _dot_tn(a, b):
    return lax.dot_general(a.astype(BF16), b.astype(BF16), (((0,), (0,)), ((), ())),
                           preferred_element_type=F32)


def _split(x, n):
    parts = []
    rem = x
    for i in range(n):
        p = rem.astype(BF16)
        parts.append(p)
        if i + 1 < n:
            rem = rem - p.astype(F32)
    return parts


def _dot_exact_rhs(a, b_bf16, n):
    acc = None
    for p in _split(a, n):
        t = jnp.dot(p, b_bf16, preferred_element_type=F32)
        acc = t if acc is None else acc + t
    return acc


def _dot_exact_lhs(a_bf16, b, n):
    acc = None
    for p in _split(b, n):
        t = jnp.dot(a_bf16, p, preferred_element_type=F32)
        acc = t if acc is None else acc + t
    return acc


def _head_sum(x, to_head, from_head):
    return _dot_exact_rhs(_dot_exact_rhs(x, to_head, 2), from_head, 3)


def _sigmoid(z):
    return 1.0 / (1.0 + jnp.exp(-z))


def _layer_norm(z, g, b):
    mu = jnp.mean(z, axis=-1, keepdims=True)
    zc = z - mu
    var = jnp.mean(zc * zc, axis=-1, keepdims=True)
    return zc * lax.rsqrt(var + LN_EPS) * g + b


def _const_spec(shape):
    nd = len(shape)
    return pl.BlockSpec(shape, lambda *_: (0,) * nd)


def _params(sem):
    return pltpu.CompilerParams(dimension_semantics=sem, vmem_limit_bytes=VMEM_LIMIT_BYTES)


def _rwkv_pre_kernel(x_ref, bnd_ref, mu_ref, vec_ref, wr_ref, wk_ref, wv_ref, w1_ref, w2_ref,
                     a1_ref, a2_ref, g1_ref, g2_ref, th_ref, fh_ref,
                     r_out, ld_out, k_out, v_out, kk_out, ka_out, g_out, bonus_out):
    x = x_ref[...]
    rows = lax.broadcasted_iota(jnp.int32, x.shape, 0)
    xp = jnp.where(rows == 0, bnd_ref[0, 0:1, :], pltpu.roll(x, 1, 0))
    dx = xp - x

    def mix(s):
        return x + dx * mu_ref[s:s + 1, :]

    w0 = vec_ref[0:1, :]
    a0 = vec_ref[1:2, :]
    k_k = vec_ref[2:3, :]
    k_a = vec_ref[3:4, :]
    r_k = vec_ref[4:5, :]
    to_head = th_ref[...]
    from_head = fh_ref[...]

    r = _dot(mix(0), wr_ref[...])
    k = _dot(mix(1), wk_ref[...])
    v = _dot(mix(2), wv_ref[...])
    zw = -(w0 + _dot(jnp.tanh(_dot(mix(3), w1_ref[...])), w2_ref[...]))
    softplus = jnp.maximum(zw, 0.0) + jnp.log(1.0 + jnp.exp(-jnp.abs(zw)))
    ld = -jnp.exp(-softplus - 0.5)
    a = _sigmoid(a0 + _dot(_dot(mix(4), a1_ref[...]), a2_ref[...]))
    g = _dot(_sigmoid(_dot(mix(5), g1_ref[...])), g2_ref[...])

    kk = k * k_k
    ss = _head_sum(kk * kk, to_head, from_head)
    kk = kk / jnp.maximum(jnp.sqrt(ss), 1e-12)
    k_h = k * (1.0 + (a - 1.0) * k_a)
    bonus = _head_sum(r * k_h * r_k, to_head, from_head) * v

    r_out[...] = r
    ld_out[...] = ld
    k_out[...] = k_h
    v_out[...] = v
    kk_out[...] = kk
    ka_out[...] = kk * a
    g_out[...] = g
    bonus_out[...] = bonus


def _rwkv_pre(x, bnd, mu8, vec8, wr, wk, wv, w1, w2, a1, a2, g1, g2, to_head, from_head, tm):
    n, d = x.shape
    row = pl.BlockSpec((tm, d), lambda i: (i, 0))
    ins = [row, pl.BlockSpec((1, SUBLANES, d), lambda i: (i, 0, 0))]
    ins += [_const_spec(a.shape)
            for a in (mu8, vec8, wr, wk, wv, w1, w2, a1, a2, g1, g2, to_head, from_head)]
    return pl.pallas_call(
        _rwkv_pre_kernel,
        grid=(n // tm,),
        in_specs=ins,
        out_specs=[row] * 8,
        out_shape=[jax.ShapeDtypeStruct((n, d), F32)] * 8,
        compiler_params=_params(("parallel",)),
        name="rwkv_pre",
    )(x, bnd, mu8, vec8, wr, wk, wv, w1, w2, a1, a2, g1, g2, to_head, from_head)


def _wkv_scan_kernel(r_ref, ld_ref, k_ref, v_ref, kk_ref, ka_ref, s0_ref, o_ref, st_ref, s_scr,
                     *, chunk, n_chunks, group):
    c = chunk
    pw = 2 * HEAD_A
    n_pp = s_scr.shape[0]
    t_idx = pl.program_id(2)

    @pl.when(t_idx == 0)
    def _():
        s_scr[...] = s0_ref[...]

    ri = lax.broadcasted_iota(jnp.int32, (c, c), 0)
    ci = lax.broadcasted_iota(jnp.int32, (c, c), 1)
    tri_incl = ri >= ci
    cum_mat = jnp.where(tri_incl, 1.0, 0.0).astype(BF16)
    ri4 = lax.broadcasted_iota(jnp.int32, (c, 4 * c), 0)
    ci4 = lax.broadcasted_iota(jnp.int32, (c, 4 * c), 1) & (c - 1)
    strict4 = ri4 > ci4
    incl4 = ri4 >= ci4
    first_blk = lax.broadcasted_iota(jnp.int32, (1, 2 * c), 1) < c
    lane1 = lax.broadcasted_iota(jnp.int32, (1, pw), 1)
    head_a1 = lane1 < HEAD_A
    lane2 = lax.broadcasted_iota(jnp.int32, (1, 2 * pw), 1) & (pw - 1)
    head_a2 = lane2 < HEAD_A
    rs = lax.broadcasted_iota(jnp.int32, (pw, pw), 0)
    cs = lax.broadcasted_iota(jnp.int32, (pw, pw), 1)
    same_head = (rs < HEAD_A) == (cs < HEAD_A)
    eye = rs == cs
    zeros_cv = jnp.zeros((c, pw), F32)

    def group_maps(slices):
        each = lambda fn, *lists: [fn(*a) for a in zip(*lists)]
        ld = [ld_ref[ix] for ix in slices]
        cw = each(lambda x: _dot_exact_lhs(cum_mat, x, 3), ld)
        w_in = each(jnp.exp, cw)
        w_ex = each(lambda a, b: jnp.exp(a - b), cw, ld)
        w_inv = each(lambda a: jnp.exp(-a), cw)
        w_last = each(lambda a: a[c - 1:c, :], w_in)
        knt = [-(kk_ref[ix] * w) for ix, w in zip(slices, w_ex)]
        kat = [ka_ref[ix] * w for ix, w in zip(slices, w_inv)]
        kt = [k_ref[ix] * w for ix, w in zip(slices, w_inv)]
        rt = [r_ref[ix] * w for ix, w in zip(slices, w_in)]
        v = [v_ref[ix] for ix in slices]
        by_head = lambda z, is_a: jnp.concatenate(
            [jnp.where(is_a, z, 0.0), jnp.where(is_a, 0.0, z)], axis=0)
        lh = each(lambda a, b: jnp.concatenate([a, b], axis=0), knt, rt)
        rh = each(lambda a, b: jnp.concatenate(
            [by_head(a, head_a1), by_head(b, head_a1)], axis=0), kat, kt)
        v_st = each(lambda a: by_head(a, head_a1), v)

        full = each(_dot_nt, lh, rh)
        top = each(lambda a: jnp.where(strict4, a[:c, :], 0.0), full)
        bot = each(lambda a: jnp.where(incl4, a[c:, :], 0.0), full)
        n_cat = each(lambda a: a[:, :2 * c], top)
        akv = each(lambda a, b: _dot(a[:, 2 * c:], b), top, v_st)

        x = each(lambda a, b: jnp.concatenate([a, b], axis=1), knt, akv)
        span = 1
        while span < c:
            dx = each(lambda n, x0: _dot(n, by_head(x0, head_a2)), n_cat, x)
            span *= 2
            if span < c:
                n_cat = each(lambda n: _dot(n, by_head(n, first_blk)), n_cat)
            x = each(lambda x0, d: x0 + d, x, dx)

        v_wide = each(lambda a: jnp.concatenate([jnp.zeros_like(a), a], axis=1), v_st)
        qo_all = each(lambda b4, x0, vw: _dot(b4, jnp.concatenate([by_head(x0, head_a2), vw], axis=0)),
                      bot, x, v_wide)
        rhs2 = each(lambda a, b: jnp.concatenate(
            [a, jnp.concatenate([zeros_cv, b], axis=1)], axis=0), x, v)
        lt = each(lambda a, b, w: jnp.concatenate([a * w, b * w], axis=0), kat, kt, w_last)
        mb = each(_dot_tn, lt, rhs2)
        out = []
        for j in range(len(slices)):
            qo = qo_all[j]
            q = rt[j] + qo[:, :pw]
            m = jnp.where(eye, w_last[j], 0.0) + jnp.where(same_head, mb[j][:, :pw], 0.0)
            b = jnp.where(same_head, mb[j][:, pw:], 0.0)
            mq = jnp.concatenate([m, q], axis=0)
            out.append((mq.astype(BF16), b, qo[:, pw:]))
        return out

    def body(it, carry):
        items = [(j, pp) for j in range(group) for pp in range(n_pp)]
        slices = [(pl.ds(pl.multiple_of((it * group + j) * c, c), c), pl.ds(pp * pw, pw))
                  for j, pp in items]
        maps = group_maps(slices)
        s = [s_scr[pp] for pp in range(n_pp)]
        for (j, pp), ix, (mq, b, o0) in zip(items, slices, maps):
            res = _dot(mq, s[pp])
            s[pp] = res[:pw, :] + b
            o_ref[ix] = res[pw:, :] + o0
        for pp in range(n_pp):
            s_scr[pp] = s[pp]
        return carry

    lax.fori_loop(0, n_chunks // group, body, 0)

    @pl.when(t_idx == pl.num_programs(2) - 1)
    def _():
        st_ref[...] = s_scr[...]


def _wkv_scan(r, ld, k, v, kk, ka, s0, chunk, t_blk):
    bn, t, d = r.shape
    pw = 2 * HEAD_A
    n_pairs = d // pw
    n_pp = SCAN_PAIRS
    seq = pl.BlockSpec((None, t_blk, n_pp * pw), lambda b, p, i: (b, i, p))
    st = pl.BlockSpec((None, n_pp, pw, pw), lambda b, p, i: (b, p, 0, 0))
    n_chunks = t_blk // chunk
    kern = functools.partial(_wkv_scan_kernel, chunk=chunk, n_chunks=n_chunks,
                             group=min(SCAN_GROUP, n_chunks))
    return pl.pallas_call(
        kern,
        grid=(bn, n_pairs // n_pp, t // t_blk),
        in_specs=[seq] * 6 + [st],
        out_specs=[seq, st],
        out_shape=[jax.ShapeDtypeStruct((bn, t, d), F32),
                   jax.ShapeDtypeStruct((bn, n_pairs, pw, pw), F32)],
        scratch_shapes=[pltpu.VMEM((n_pp, pw, pw), F32)],
        compiler_params=_params(("parallel", "parallel", "arbitrary")),
        name="wkv_scan",
    )(r, ld, k, v, kk, ka, s0)


def _rwkv_post_kernel(o_ref, bonus_ref, g_ref, x_ref, vec_ref, th_ref, fh_ref, wo_ref, out_ref):
    o = o_ref[...]
    to_head = th_ref[...]
    from_head = fh_ref[...]
    inv_n = 1.0 / HEAD_A
    mean = _head_sum(o, to_head, from_head) * inv_n
    oc = o - mean
    var = _head_sum(oc * oc, to_head, from_head) * inv_n
    on = oc * lax.rsqrt(var + GN_EPS) * vec_ref[0:1, :] + vec_ref[1:2, :]
    y = (on + bonus_ref[...]) * g_ref[...]
    h = _dot(y, wo_ref[...])
    out_ref[...] = _layer_norm(ALPHA * x_ref[...] + h, vec_ref[2:3, :], vec_ref[3:4, :])


def _rwkv_post(o, bonus, g, x, vec8, to_head, from_head, wo, tm):
    n, d = x.shape
    row = pl.BlockSpec((tm, d), lambda i: (i, 0))
    return pl.pallas_call(
        _rwkv_post_kernel,
        grid=(n // tm,),
        in_specs=[row] * 4 + [_const_spec(a.shape) for a in (vec8, to_head, from_head, wo)],
        out_specs=row,
        out_shape=jax.ShapeDtypeStruct((n, d), F32),
        compiler_params=_params(("parallel",)),
        name="rwkv_post",
    )(o, bonus, g, x, vec8, to_head, from_head, wo)


def _ffn_kernel(x_ref, wg_ref, wu_ref, wd_ref, ln_ref, out_ref, acc_ref, xb_ref):
    f = pl.program_id(1)

    @pl.when(f == 0)
    def _():
        acc_ref[...] = jnp.zeros_like(acc_ref)
        xb_ref[...] = x_ref[...].astype(BF16)

    xb = xb_ref[...]
    gate = jnp.dot(xb, wg_ref[...].astype(BF16), preferred_element_type=F32)
    up = jnp.dot(xb, wu_ref[...].astype(BF16), preferred_element_type=F32)
    h = gate * _sigmoid(gate) * up
    acc_ref[...] += jnp.dot(h.astype(BF16), wd_ref[...].astype(BF16), preferred_element_type=F32)

    @pl.when(f == pl.num_programs(1) - 1)
    def _():
        out_ref[...] = _layer_norm(ALPHA * x_ref[...] + acc_ref[...], ln_ref[0:1, :], ln_ref[1:2, :])


def _ffn(x, w_gu, w_down, ln8, tm, tf):
    n, d = x.shape
    d_ff = w_gu.shape[1] // 2
    nf = d_ff // tf
    return pl.pallas_call(
        _ffn_kernel,
        grid=(n // tm, nf),
        in_specs=[
            pl.BlockSpec((tm, d), lambda i, f: (i, 0)),
            pl.BlockSpec((d, tf), lambda i, f: (0, f)),
            pl.BlockSpec((d, tf), lambda i, f: (0, nf + f)),
            pl.BlockSpec((tf, d), lambda i, f: (f, 0)),
            _const_spec(ln8.shape),
        ],
        out_specs=pl.BlockSpec((tm, d), lambda i, f: (i, 0)),
        out_shape=jax.ShapeDtypeStruct((n, d), F32),
        scratch_shapes=[pltpu.VMEM((tm, d), F32), pltpu.VMEM((tm, d), BF16)],
        compiler_params=_params(("parallel", "arbitrary")),
        name="dense_ffn",
    )(x, w_gu, w_gu, w_down, ln8)


ROUTE_I1, ROUTE_I2, ROUTE_G1, ROUTE_G2, ROUTE_Q1, ROUTE_Q2 = range(6)
META_CARRY, META_COUNT = 0, 1


def _router_kernel(x_ref, w_ref, route_ref, meta_ref, count_ref, carry_ref):
    i = pl.program_id(0)

    @pl.when(i == 0)
    def _():
        carry_ref[...] = jnp.zeros_like(carry_ref)

    x_hi, x_lo = _split(x_ref[...], 2)
    w_hi, w_lo = _split(w_ref[...], 2)
    logits = (jnp.dot(x_hi, w_hi, preferred_element_type=F32)
              + jnp.dot(x_lo, w_hi, preferred_element_type=F32)
              + jnp.dot(x_hi, w_lo, preferred_element_type=F32))
    tm = logits.shape[0]
    lane = lax.broadcasted_iota(jnp.int32, logits.shape, 1).astype(F32)
    neg = -jnp.inf
    logits = jnp.where(lane < N_EXPERTS, logits, neg)
    m1 = jnp.max(logits, axis=-1, keepdims=True)
    i1 = jnp.min(jnp.where(logits == m1, lane, float(LANES)), axis=-1, keepdims=True)
    rest = jnp.where(lane == i1, neg, logits)
    m2 = jnp.max(rest, axis=-1, keepdims=True)
    i2 = jnp.min(jnp.where(rest == m2, lane, float(LANES)), axis=-1, keepdims=True)
    e2 = jnp.exp(m2 - m1)
    den = 1.0 + e2

    sel1 = lane == i1
    sel2 = lane == i2
    onehot = jnp.where(sel1, 1.0, 0.0) + jnp.where(sel2, 1.0, 0.0)
    ri = lax.broadcasted_iota(jnp.int32, (tm, tm), 0)
    ci = lax.broadcasted_iota(jnp.int32, (tm, tm), 1)
    earlier = jnp.where(ri > ci, 1.0, 0.0).astype(BF16)
    in_tile = jnp.dot(earlier, onehot.astype(BF16), preferred_element_type=F32)
    q1 = jnp.sum(jnp.where(sel1, in_tile, 0.0), axis=-1, keepdims=True)
    q2 = jnp.sum(jnp.where(sel2, in_tile, 0.0), axis=-1, keepdims=True)
    tile_count = jnp.sum(onehot, axis=0, keepdims=True)

    route = jnp.zeros_like(logits)
    for col, val in ((ROUTE_I1, i1), (ROUTE_I2, i2), (ROUTE_G1, 1.0 / den), (ROUTE_G2, e2 / den),
                     (ROUTE_Q1, q1), (ROUTE_Q2, q2)):
        route = jnp.where(lane == float(col), val, route)
    route_ref[...] = route
    meta_row = lax.broadcasted_iota(jnp.int32, meta_ref.shape, 0)
    meta_ref[...] = jnp.where(meta_row == META_CARRY, carry_ref[0:1, :],
                              jnp.where(meta_row == META_COUNT, tile_count, 0.0))
    slab_rows = jnp.floor((tile_count + (SLAB_UNIT - 1)) * (1.0 / SLAB_UNIT)) * SLAB_UNIT
    carry_ref[0:1, :] = carry_ref[0:1, :] + slab_rows
    count_ref[...] = carry_ref[...]


def _router(x, w_pad, tm):
    n, d = x.shape
    return pl.pallas_call(
        _router_kernel,
        grid=(n // tm,),
        in_specs=[pl.BlockSpec((tm, d), lambda i: (i, 0)), _const_spec(w_pad.shape)],
        out_specs=[pl.BlockSpec((tm, LANES), lambda i: (i, 0)),
                   pl.BlockSpec((None, SUBLANES, LANES), lambda i: (i, 0, 0)),
                   _const_spec((SUBLANES, LANES))],
        out_shape=[jax.ShapeDtypeStruct((n, LANES), F32),
                   jax.ShapeDtypeStruct((n // tm, SUBLANES, LANES), F32),
                   jax.ShapeDtypeStruct((SUBLANES, LANES), F32)],
        scratch_shapes=[pltpu.VMEM((SUBLANES, LANES), F32)],
        compiler_params=_params(("arbitrary",)),
        name="router",
    )(x, w_pad)


def _slab_plan(meta_ref):
    plan = []
    off = 0
    for e in range(N_EXPERTS):
        units = (meta_ref[0, e] + (SLAB_UNIT - 1)) // SLAB_UNIT
        plan.append((units, meta_ref[0, N_EXPERTS + e], off))
        off = off + units * SLAB_UNIT
    return plan


def _slab_buffer_rows(tm):
    worst = 2 * tm + N_EXPERTS * (SLAB_UNIT - 1)
    return ((worst + LANES - 1) // LANES) * LANES


def _slab_rows(expert, rank, plan):
    off = jnp.zeros_like(expert)
    for e, (_, _, e_off) in enumerate(plan):
        off = jnp.where(expert == e, e_off, off)
    return off + rank


def _slab_copies(plan, make_copy):
    total = 0
    for units, first_row, off in plan:
        def start(u, carry, first_row=first_row, off=off):
            make_copy(pl.multiple_of(off + u * SLAB_UNIT, SLAB_UNIT),
                      pl.multiple_of(first_row + u * SLAB_UNIT, SLAB_UNIT)).start()
            return carry
        lax.fori_loop(0, units, start, 0)
        total = total + units
    return total


def _dispatch_kernel(meta_ref, route_ref, x_ref, xs_in_ref, xs_ref, cbuf, sem):
    del xs_in_ref
    tm = x_ref.shape[0]
    plan = _slab_plan(meta_ref)

    sel = (lax.broadcasted_iota(jnp.int32, (SUBLANES, LANES), 0)
           == lax.broadcasted_iota(jnp.int32, (SUBLANES, LANES), 1)).astype(BF16)
    route_t = None
    for piece in _split(route_ref[...], 3):
        t = lax.dot_general(sel, piece, (((1,), (1,)), ((), ())), preferred_element_type=F32)
        route_t = t if route_t is None else route_t + t
    route_t = route_t.astype(jnp.int32)
    row1 = _slab_rows(route_t[ROUTE_I1:ROUTE_I1 + 1, :], route_t[ROUTE_Q1:ROUTE_Q1 + 1, :], plan)
    row2 = _slab_rows(route_t[ROUTE_I2:ROUTE_I2 + 1, :], route_t[ROUTE_Q2:ROUTE_Q2 + 1, :], plan)

    r_iota = lax.broadcasted_iota(jnp.int32, (cbuf.shape[0], tm), 0)
    perm = jnp.where((r_iota == row1) | (r_iota == row2), 1.0, 0.0).astype(BF16)
    cbuf[...] = jnp.dot(perm, x_ref[...].astype(BF16), preferred_element_type=F32)

    def make_copy(buf_row, sorted_row):
        return pltpu.make_async_copy(cbuf.at[pl.ds(buf_row, SLAB_UNIT), :],
                                     xs_ref.at[pl.ds(sorted_row, SLAB_UNIT), :], sem)

    n_started = _slab_copies(plan, make_copy)

    def wait(u, carry):
        make_copy(0, 0).wait()
        return carry

    lax.fori_loop(0, n_started, wait, 0)


def _dispatch(meta, route, x, xs, tm):
    n, d = x.shape
    return pl.pallas_call(
        _dispatch_kernel,
        grid=(n // tm,),
        in_specs=[pl.BlockSpec((None, 1, 2 * N_EXPERTS), lambda i: (i, 0, 0), memory_space=pltpu.SMEM),
                  pl.BlockSpec((tm, LANES), lambda i: (i, 0)),
                  pl.BlockSpec((tm, d), lambda i: (i, 0)),
                  pl.BlockSpec(memory_space=pl.ANY)],
        out_specs=pl.BlockSpec(memory_space=pl.ANY),
        out_shape=jax.ShapeDtypeStruct(xs.shape, xs.dtype),
        scratch_shapes=[pltpu.VMEM((_slab_buffer_rows(tm), d), F32), pltpu.SemaphoreType.DMA(())],
        input_output_aliases={3: 0},
        compiler_params=_params(("arbitrary",)),
        name="moe_dispatch",
    )(meta, route, x, xs)


def _experts_kernel(te_ref, rows_ref, x_ref, wg_ref, wu_ref, wd_ref, out_ref, acc_ref, xb_ref):
    del te_ref
    f = pl.program_id(1)
    rows = rows_ref[pl.program_id(0)]
    tm = x_ref.shape[0]

    @pl.when(f == 0)
    def _():
        acc_ref[...] = jnp.zeros_like(acc_ref)

    def swiglu_rows(n_rows):
        sl = pl.ds(0, n_rows)

        @pl.when(f == 0)
        def _():
            xb_ref[sl, :] = x_ref[sl, :].astype(BF16)

        xb = xb_ref[sl, :]
        gate = jnp.dot(xb, wg_ref[...].astype(BF16), preferred_element_type=F32)
        up = jnp.dot(xb, wu_ref[...].astype(BF16), preferred_element_type=F32)
        h = gate * _sigmoid(gate) * up
        acc_ref[sl, :] += jnp.dot(h.astype(BF16), wd_ref[...].astype(BF16), preferred_element_type=F32)

    @pl.when(rows > tm // MOE_TILE_PARTS)
    def _():
        swiglu_rows(tm)

    @pl.when(jnp.logical_and(rows > 0, rows <= tm // MOE_TILE_PARTS))
    def _():
        swiglu_rows(tm // MOE_TILE_PARTS)

    @pl.when(f == pl.num_programs(1) - 1)
    def _():
        out_ref[...] = acc_ref[...]


def _experts(tile_expert, tile_rows, xs, w_gu, w_down, tm, tf):
    s_total, d = xs.shape
    d_ff = w_gu.shape[2] // 2
    nf = d_ff // tf

    def f_eff(i, f, rows):
        return jnp.where(rows[i] > 0, f, nf - 1)

    grid_spec = pltpu.PrefetchScalarGridSpec(
        num_scalar_prefetch=2,
        grid=(s_total // tm, nf),
        in_specs=[
            pl.BlockSpec((tm, d), lambda i, f, te, rows: (i, 0)),
            pl.BlockSpec((None, d, tf), lambda i, f, te, rows: (te[i], 0, f_eff(i, f, rows))),
            pl.BlockSpec((None, d, tf), lambda i, f, te, rows: (te[i], 0, nf + f_eff(i, f, rows))),
            pl.BlockSpec((None, tf, d), lambda i, f, te, rows: (te[i], f_eff(i, f, rows), 0)),
        ],
        out_specs=pl.BlockSpec((tm, d), lambda i, f, te, rows: (i, 0)),
        scratch_shapes=[pltpu.VMEM((tm, d), F32), pltpu.VMEM((tm, d), BF16)],
    )
    return pl.pallas_call(
        _experts_kernel,
        grid_spec=grid_spec,
        out_shape=jax.ShapeDtypeStruct((s_total, d), F32),
        compiler_params=_params(("arbitrary", "arbitrary")),
        name="moe_experts",
    )(tile_expert, tile_rows, xs, w_gu, w_gu, w_down)


def _combine_kernel(meta_ref, route_ref, x_ref, ys_ref, ln_ref, out_ref, cbuf, sem):
    tm = x_ref.shape[0]
    plan = _slab_plan(meta_ref)

    @pl.when(pl.program_id(0) == 0)
    def _():
        cbuf[...] = jnp.zeros_like(cbuf)

    def make_copy(buf_row, sorted_row):
        return pltpu.make_async_copy(ys_ref.at[pl.ds(sorted_row, SLAB_UNIT), :],
                                     cbuf.at[pl.ds(buf_row, SLAB_UNIT), :], sem)

    n_started = _slab_copies(plan, make_copy)

    def wait(u, carry):
        make_copy(0, 0).wait()
        return carry

    lax.fori_loop(0, n_started, wait, 0)

    route = route_ref[...]
    col = lambda k: route[:, k:k + 1]
    row1 = _slab_rows(col(ROUTE_I1).astype(jnp.int32), col(ROUTE_Q1).astype(jnp.int32), plan)
    row2 = _slab_rows(col(ROUTE_I2).astype(jnp.int32), col(ROUTE_Q2).astype(jnp.int32), plan)
    c_iota = lax.broadcasted_iota(jnp.int32, (tm, cbuf.shape[0]), 1)
    pick = jnp.concatenate([jnp.where(c_iota == row1, 1.0, 0.0), jnp.where(c_iota == row2, 1.0, 0.0)],
                           axis=0).astype(BF16)
    y12 = _dot_exact_lhs(pick, cbuf[...], 2)
    y = col(ROUTE_G1) * y12[:tm, :] + col(ROUTE_G2) * y12[tm:, :]
    out_ref[...] = _layer_norm(ALPHA * x_ref[...] + y, ln_ref[0:1, :], ln_ref[1:2, :])


def _combine(meta, route, x, ys, ln8, tm):
    n, d = x.shape
    return pl.pallas_call(
        _combine_kernel,
        grid=(n // tm,),
        in_specs=[pl.BlockSpec((None, 1, 2 * N_EXPERTS), lambda i: (i, 0, 0), memory_space=pltpu.SMEM),
                  pl.BlockSpec((tm, LANES), lambda i: (i, 0)),
                  pl.BlockSpec((tm, d), lambda i: (i, 0)),
                  pl.BlockSpec(memory_space=pl.ANY),
                  _const_spec(ln8.shape)],
        out_specs=pl.BlockSpec((tm, d), lambda i: (i, 0)),
        out_shape=jax.ShapeDtypeStruct((n, d), F32),
        scratch_shapes=[pltpu.VMEM((_slab_buffer_rows(tm), d), F32), pltpu.SemaphoreType.DMA(())],
        compiler_params=_params(("arbitrary",)),
        name="moe_combine",
    )(meta, route, x, ys, ln8)


def _rope(y, cos_t, sin_next, sin_prev):
    n = y.shape[1]
    reps = n // LANES
    tile = lambda t: jnp.concatenate([t] * reps, axis=1) if reps > 1 else t
    half = ROPE_DIM // 2
    return (y * tile(cos_t)
            + pltpu.roll(y, n - half, 1) * tile(sin_next)
            + pltpu.roll(y, half, 1) * tile(sin_prev))


def _kv_kernel(x_ref, w_ref, cos_ref, sn_ref, sp_ref, k_out, v_out):
    kv = _dot(x_ref[...], w_ref[...])
    nk = k_out.shape[1]
    k_out[...] = _rope(kv[:, :nk], cos_ref[...], sn_ref[...], sp_ref[...])
    v_out[...] = kv[:, nk:]


def _q_kernel(x_ref, w_ref, cos_ref, sn_ref, sp_ref, q_out):
    q_out[...] = _rope(_dot(x_ref[...], w_ref[...]), cos_ref[...], sn_ref[...], sp_ref[...])


def _rope_proj(kernel_fn, x, w, tables, tm, out_widths, name):
    n, d = x.shape
    t_tiles = tables[0].shape[0] // tm
    row = pl.BlockSpec((tm, d), lambda i: (i, 0))
    tab = pl.BlockSpec((tm, LANES), lambda i: (i % t_tiles, 0))
    return pl.pallas_call(
        kernel_fn,
        grid=(n // tm,),
        in_specs=[row, _const_spec(w.shape), tab, tab, tab],
        out_specs=[pl.BlockSpec((tm, ow), lambda i: (i, 0)) for ow in out_widths],
        out_shape=[jax.ShapeDtypeStruct((n, ow), F32) for ow in out_widths],
        compiler_params=_params(("parallel",)),
        name=name,
    )(x, w, *tables)


def _attn_kernel(sink_ref, q_ref, kp_ref, kc_ref, vp_ref, vc_ref, o_ref, *, banded):
    tq = q_ref.shape[0]
    n_prev = kp_ref.shape[0]
    tk = n_prev + kc_ref.shape[0]
    q = q_ref[...] * ATTN_SCALE
    kband = jnp.concatenate([kp_ref[...], kc_ref[...]], axis=0)
    vband = jnp.concatenate([vp_ref[...], vc_ref[...]], axis=0)
    n_heads = q.shape[1] // HEAD_B
    group = n_heads // KV_HEADS
    qc = CHUNK if banded else tq
    kc = WINDOW + CHUNK if banded else tk
    units = [(kh, ci) for kh in range(KV_HEADS) for ci in range(tq // qc)]
    if banded:
        kj = lax.broadcasted_iota(jnp.int32, (kc, group * qc), 0)
        band_start = pl.program_id(1) * tq - n_prev

    qlane = lax.broadcasted_iota(jnp.int32, (1, group * qc), 1)

    def scores(kh, ci):
        qs = jnp.concatenate([q[ci * qc:(ci + 1) * qc, (kh * group + j) * HEAD_B:(kh * group + j + 1) * HEAD_B]
                              for j in range(group)], axis=0)
        return _dot_nt(kband[ci * qc:ci * qc + kc, kh * HEAD_B:(kh + 1) * HEAD_B], qs)

    def softmax(kh, ci, s):
        sk = jnp.full((1, group * qc), sink_ref[kh * group], F32)
        for j in range(1, group):
            sk = jnp.where(qlane >= j * qc, sink_ref[kh * group + j], sk)
        if banded:
            s = jnp.where(band_start + ci * qc + kj >= 0, s, -jnp.inf)
        m = jnp.maximum(jnp.max(s, axis=0, keepdims=True), sk)
        p = jnp.exp(s - m)
        return p * (1.0 / (jnp.sum(p, axis=0, keepdims=True) + jnp.exp(sk - m)))

    def weighted(kh, ci, p):
        return _dot_tn(p, vband[ci * qc:ci * qc + kc, kh * HEAD_B:(kh + 1) * HEAD_B])

    chunks = range(tq // qc)
    outs = {}
    s_next = [scores(0, ci) for ci in chunks]
    for kh in range(KV_HEADS):
        s_cur = s_next
        if kh + 1 < KV_HEADS:
            s_next = [scores(kh + 1, ci) for ci in chunks]
        probs = [softmax(kh, ci, s_cur[ci]) for ci in chunks]
        for ci in chunks:
            o = weighted(kh, ci, probs[ci])
            for j in range(group):
                outs[(kh * group + j, ci)] = o[j * qc:(j + 1) * qc, :]
    o_ref[...] = jnp.concatenate(
        [jnp.concatenate([outs[(h, ci)] for ci in range(tq // qc)], axis=0) for h in range(n_heads)],
        axis=1)


def _attention(q, k_prev_src, k_cur_src, v_prev_src, v_cur_src, sinks, tq, banded):
    bn, t, d = q.shape
    kw = k_cur_src.shape[2]
    if banded:
        ratio = tq // WINDOW
        prev_map = lambda b, i: (b, jnp.maximum(i * ratio - 1, 0), 0)
    else:
        prev_map = lambda b, i: (b, 0, 0)
    prev = pl.BlockSpec((None, WINDOW, kw), prev_map)
    cur = pl.BlockSpec((None, tq, kw), lambda b, i: (b, i, 0))
    kern = functools.partial(_attn_kernel, banded=banded)
    return pl.pallas_call(
        kern,
        grid=(bn, t // tq),
        in_specs=[pl.BlockSpec(memory_space=pltpu.SMEM),
                  pl.BlockSpec((None, tq, d), lambda b, i: (b, i, 0)), prev, cur, prev, cur],
        out_specs=pl.BlockSpec((None, tq, d), lambda b, i: (b, i, 0)),
        out_shape=jax.ShapeDtypeStruct((bn, t, d), F32),
        compiler_params=_params(("parallel", "parallel")),
        name="swa_attn",
    )(sinks, q, k_prev_src, k_cur_src, v_prev_src, v_cur_src)


def _proj_ln_kernel(y_ref, x_ref, w_ref, ln_ref, out_ref):
    h = _dot(y_ref[...], w_ref[...])
    out_ref[...] = _layer_norm(ALPHA * x_ref[...] + h, ln_ref[0:1, :], ln_ref[1:2, :])


def _proj_ln(y, x, w, ln8, tm):
    n, d = x.shape
    return pl.pallas_call(
        _proj_ln_kernel,
        grid=(n // tm,),
        in_specs=[pl.BlockSpec((tm, y.shape[1]), lambda i: (i, 0)), pl.BlockSpec((tm, d), lambda i: (i, 0)),
                  _const_spec(w.shape), _const_spec(ln8.shape)],
        out_specs=pl.BlockSpec((tm, d), lambda i: (i, 0)),
        out_shape=jax.ShapeDtypeStruct((n, d), F32),
        compiler_params=_params(("parallel",)),
        name="proj_ln",
    )(y, x, w, ln8)


def _moe_layer(xs_rows, P):
    d = xs_rows[0].shape[1]
    tm_e = MOE_TILE
    router_pad = jnp.concatenate([P['moe_router'][0], jnp.zeros((d, LANES - N_EXPERTS), F32)], axis=1)
    routed = [_router(x, router_pad, min(MOE_ROW_TILE, x.shape[0])) for x in xs_rows]
    counts = [c[0, :N_EXPERTS].astype(jnp.int32) for _, _, c in routed]
    total = sum(counts)
    padded = ((total + tm_e - 1) // tm_e) * tm_e
    ends = jnp.cumsum(padded)
    starts = ends - padded
    n_slabs = sum(-(-x.shape[0] // MOE_ROW_TILE) for x in xs_rows) * N_EXPERTS
    n_assign = 2 * sum(x.shape[0] for x in xs_rows) + n_slabs * (SLAB_UNIT - 1)
    n_tiles = (n_assign + N_EXPERTS * (tm_e - 1)) // tm_e
    n_used = (ends[-1] // tm_e).astype(jnp.int32)
    tile_expert = jnp.sum((jnp.arange(n_tiles) * tm_e)[:, None] >= ends[None, :], axis=1).astype(jnp.int32)
    tile_expert = jnp.minimum(tile_expert, N_EXPERTS - 1)
    tile_expert = jnp.where(jnp.arange(n_tiles) < n_used, tile_expert,
                            tile_expert[jnp.maximum(n_used - 1, 0)])
    tile_rows = jnp.clip((starts + total)[tile_expert] - jnp.arange(n_tiles) * tm_e, 0, tm_e)
    tile_rows = jnp.where(jnp.arange(n_tiles) < n_used, tile_rows, 0).astype(jnp.int32)

    metas = []
    base = starts
    for (_, meta, _), cnt in zip(routed, counts):
        carry = meta[:, META_CARRY, :N_EXPERTS].astype(jnp.int32)
        sent = meta[:, META_COUNT, :N_EXPERTS].astype(jnp.int32)
        metas.append(jnp.concatenate([sent, base[None, :] + carry], axis=1)[:, None, :])
        base = base + cnt

    xs = jnp.zeros((n_tiles * tm_e, d), F32)
    for m, (route, _, _), x in zip(metas, routed, xs_rows):
        xs = _dispatch(m, route, x, xs, min(MOE_ROW_TILE, x.shape[0]))
    ys = _experts(tile_expert, tile_rows, xs, P['moe_w_gu'][0], P['moe_w_down'][0], tm_e, 512)
    ln11 = _pad_rows([P['ln_g'][1, 1], P['ln_b'][1, 1]], d)
    return [_combine(m, route, x, ys, ln11, min(MOE_ROW_TILE, x.shape[0]))
            for m, (route, _, _), x in zip(metas, routed, xs_rows)]


def _pad_rows(rows, d):
    a = jnp.stack(rows).astype(F32)
    return jnp.concatenate([a, jnp.zeros((SUBLANES - a.shape[0], d), F32)], axis=0)


def _rope_tables(pos, reps):
    inv_freq = ROPE_THETA ** (-jnp.arange(0, ROPE_DIM, 2, dtype=jnp.float32) / ROPE_DIM)
    ang = pos.astype(jnp.float32)[:, None] * inv_freq[None, :]
    cos = jnp.cos(ang)
    sin = jnp.sin(ang)
    t = pos.shape[0]
    half = ROPE_DIM // 2
    rest = HEAD_B - ROPE_DIM
    z_half = jnp.zeros((t, half), F32)
    z_rest = jnp.zeros((t, rest), F32)
    cos_h = jnp.concatenate([cos, cos, jnp.ones((t, rest), F32)], axis=1)
    sn_h = jnp.concatenate([-sin, z_half, z_rest], axis=1)
    sp_h = jnp.concatenate([z_half, sin, z_rest], axis=1)
    per_tile = LANES // HEAD_B
    return tuple(jnp.tile(a, (reps, per_tile)) for a in (cos_h, sn_h, sp_h))


def _trunk(x, shift_in, wkv_in, k_cache, v_cache, pos0, P):
    bn, t, d = x.shape
    n = bn * t
    h_a = d // HEAD_A
    pw = 2 * HEAD_A
    n_pairs = d // pw
    xf = x.reshape(n, d)

    tm_pre = min(256, t)
    tm_row = min(256, n)
    tm_ffn = min(1024, n)
    chunk = min(CHUNK, t)
    t_blk = min(512, t)

    tiles = jnp.arange(n // tm_pre) * tm_pre
    prev_rows = xf[jnp.maximum(tiles - 1, 0)]
    start_rows = shift_in[0][tiles // t]
    bnd = jnp.where(((tiles % t) == 0)[:, None], start_rows, prev_rows)
    bnd = jnp.broadcast_to(bnd[:, None, :], (n // tm_pre, SUBLANES, d))

    mu8 = jnp.concatenate([P['a_mu'][0], jnp.zeros((2, d), F32)], axis=0)
    vec_pre = _pad_rows([P['a_w0'][0], P['a_a0'][0], P['a_k_k'][0], P['a_k_a'][0],
                         P['a_r_k'][0].reshape(d)], d)
    lane_head = jnp.arange(d) // HEAD_A
    to_head = (lane_head[:, None] == jnp.arange(LANES)[None, :]).astype(BF16)
    from_head = to_head.T
    bf = lambda a: a.astype(BF16)
    w_rkv = P['a_w_rkv'][0]
    r, ld, k_h, v, kk, ka, g, bonus = _rwkv_pre(
        xf, bnd, mu8, vec_pre, bf(w_rkv[0]), bf(w_rkv[1]), bf(w_rkv[2]),
        bf(P['a_w1'][0]), bf(P['a_w2'][0]), bf(P['a_a1'][0]), bf(P['a_a2'][0]),
        bf(P['a_g1'][0]), bf(P['a_g2'][0]), to_head, from_head, tm_pre)

    st_t = jnp.swapaxes(wkv_in[0].astype(F32), -1, -2).reshape(bn, n_pairs, 2, HEAD_A, HEAD_A)
    eye2 = jnp.eye(2, dtype=F32)
    s0 = jnp.einsum('bpajk,ac->bpajck', st_t, eye2).reshape(bn, n_pairs, pw, pw)
    seq3 = lambda a: a.reshape(bn, t, d)
    o, s_fin = _wkv_scan(seq3(r), seq3(ld), seq3(k_h), seq3(v), seq3(kk), seq3(ka), s0, chunk, t_blk)
    s_fin = s_fin.reshape(bn, n_pairs, 2, HEAD_A, 2, HEAD_A)
    s_fin = jnp.stack([s_fin[:, :, 0, :, 0, :], s_fin[:, :, 1, :, 1, :]], axis=2)
    wkv_out = jnp.swapaxes(s_fin, -1, -2).reshape(bn, h_a, HEAD_A, HEAD_A)
    shift_out = x[:, -1]

    vec_post = _pad_rows([P['a_lnx_g'][0], P['a_lnx_b'][0], P['ln_g'][0, 0], P['ln_b'][0, 0]], d)
    x1 = _rwkv_post(o.reshape(n, d), bonus, g, xf, vec_post, to_head, from_head, bf(P['a_w_o'][0]), tm_row)

    ln01 = _pad_rows([P['ln_g'][0, 1], P['ln_b'][0, 1]], d)
    x2 = _ffn(x1, P['ffn_w_gu'][0], P['ffn_w_down'][0], ln01, tm_ffn, 512)

    pos = pos0 + jnp.arange(t, dtype=jnp.int32)
    tables = _rope_tables(pos, max(tm_row // t, 1))
    kvw = KV_HEADS * HEAD_B
    k_new, v_new = _rope_proj(_kv_kernel, x2, bf(P['kv_w']), tables, tm_row, (kvw, kvw), "kv_proj")
    k_new = k_new.reshape(bn, t, kvw)
    v_new = v_new.reshape(bn, t, kvw)
    (q,) = _rope_proj(_q_kernel, x2, bf(P['b_w_q'][0]), tables, tm_row, (d,), "q_proj")
    q = q.reshape(bn, t, d)
    sinks = P['b_sinks'][0].astype(F32)
    if k_cache is None:
        att = _attention(q, k_new, k_new, v_new, v_new, sinks, min(256, t), banded=True)
        k_out = k_new[:, -WINDOW:]
        v_out = v_new[:, -WINDOW:]
    else:
        kc = k_cache.astype(F32).reshape(bn, WINDOW, kvw)
        vc = v_cache.astype(F32).reshape(bn, WINDOW, kvw)
        att = _attention(q, kc, k_new, vc, v_new, sinks, t, banded=False)
        k_out = jnp.concatenate([kc, k_new], axis=1)[:, -WINDOW:]
        v_out = jnp.concatenate([vc, v_new], axis=1)[:, -WINDOW:]
    ln10 = _pad_rows([P['ln_g'][1, 0], P['ln_b'][1, 0]], d)
    x3 = _proj_ln(att.reshape(n, d), x2, bf(P['b_w_o'][0]), ln10, tm_row)

    return (x3, shift_out[None], wkv_out[None],
            k_out.reshape(bn, WINDOW, KV_HEADS, HEAD_B), v_out.reshape(bn, WINDOW, KV_HEADS, HEAD_B))


def kernel(x_prompt, x_sample, cache_shift_a, state_wkv_a, cache_k_b, cache_v_b, a_mu, a_w_rkv, a_w0, a_w1, a_w2, a_a0, a_a1, a_a2, a_g1, a_g2, a_k_k, a_k_a, a_r_k, a_lnx_g, a_lnx_b, a_w_o, kv_w, b_w_q, b_sinks, b_w_o, ln_g, ln_b, ffn_w_gu, ffn_w_down, moe_router, moe_w_gu, moe_w_down):
    P = {
        'a_mu': a_mu, 'a_w_rkv': a_w_rkv, 'a_w0': a_w0, 'a_w1': a_w1, 'a_w2': a_w2,
        'a_a0': a_a0, 'a_a1': a_a1, 'a_a2': a_a2, 'a_g1': a_g1, 'a_g2': a_g2,
        'a_k_k': a_k_k, 'a_k_a': a_k_a, 'a_r_k': a_r_k, 'a_lnx_g': a_lnx_g,
        'a_lnx_b': a_lnx_b, 'a_w_o': a_w_o, 'kv_w': kv_w, 'b_w_q': b_w_q,
        'b_sinks': b_sinks, 'b_w_o': b_w_o, 'ln_g': ln_g, 'ln_b': ln_b,
        'ffn_w_gu': ffn_w_gu, 'ffn_w_down': ffn_w_down, 'moe_router': moe_router,
        'moe_w_gu': moe_w_gu, 'moe_w_down': moe_w_down,
    }
    bp = x_prompt.shape[0]
    d = x_prompt.shape[2]
    h_a = d // HEAD_A
    zero_shift = jnp.zeros((1, bp, d), x_prompt.dtype)
    zero_wkv = jnp.zeros((1, bp, h_a, HEAD_A, HEAD_A), F32)
    x3_p, p_shift, p_wkv, p_k, p_v = _trunk(x_prompt, zero_shift, zero_wkv, None, None, 0, P)
    x3_s, s_shift, s_wkv, s_k, s_v = _trunk(x_sample, cache_shift_a, state_wkv_a,
                                            cache_k_b, cache_v_b, PAST_LEN, P)
    y_p, y_s = _moe_layer([x3_p, x3_s], P)
    return (y_p.reshape(x_prompt.shape), y_s.reshape(x_sample.shape),
            p_shift, p_wkv, p_k, p_v, s_shift, s_wkv, s_k, s_v)
```

```python
import functools

import jax
import jax.numpy as jnp
from jax import lax
from jax.experimental import pallas as pl
from jax.experimental.pallas import tpu as pltpu

F32 = jnp.float32
BF16 = jnp.bfloat16

DEPTH = 2
HEAD_A = 64
HEAD_B = 64
KV_HEADS = 4
CHUNK = 64
WINDOW = 128
PAST_LEN = 4096
ROPE_DIM = HEAD_B // 4
ROPE_THETA = 500000.0
ATTN_SCALE = HEAD_B ** -0.5
N_EXPERTS = 8
GN_EPS = 64e-5
LN_EPS = 1e-5
ALPHA = (2.0 * DEPTH) ** 0.25

LANES = 128
SUBLANES = 8
VMEM_LIMIT_BYTES = 56 * 1024 * 1024
SCAN_GROUP = 8
SCAN_PAIRS = 2
MOE_TILE = 1024
MOE_ROW_TILE = 512
MOE_TILE_PARTS = 4
SLAB_UNIT = SUBLANES


def _dot(a, b):
    return jnp.dot(a.astype(BF16), b.astype(BF16), preferred_element_type=F32)


def _dot_nt(a, b):
    return lax.dot_general(a.astype(BF16), b.astype(BF16), (((1,), (1,)), ((), ())),
                           preferred_element_type=F32)


def _dot_tn(a, b):
    return lax.dot_general(a.astype(BF16), b.astype(BF16), (((0,), (0,)), ((), ())),
                           preferred_element_type=F32)


def _split(x, n):
    parts = []
    rem = x
    for i in range(n):
        p = rem.astype(BF16)
        parts.append(p)
        if i + 1 < n:
            rem = rem - p.astype(F32)
    return parts


def _dot_exact_rhs(a, b_bf16, n):
    acc = None
    for p in _split(a, n):
        t = jnp.dot(p, b_bf16, preferred_element_type=F32)
        acc = t if acc is None else acc + t
    return acc


def _dot_exact_lhs(a_bf16, b, n):
    acc = None
    for p in _split(b, n):
        t = jnp.dot(a_bf16, p, preferred_element_type=F32)
        acc = t if acc is None else acc + t
    return acc


def _head_sum(x, to_head, from_head):
    return _dot_exact_rhs(_dot_exact_rhs(x, to_head, 2), from_head, 3)


def _sigmoid(z):
    return 1.0 / (1.0 + jnp.exp(-z))


def _layer_norm(z, g, b):
    mu = jnp.mean(z, axis=-1, keepdims=True)
    zc = z - mu
    var = jnp.mean(zc * zc, axis=-1, keepdims=True)
    return zc * lax.rsqrt(var + LN_EPS) * g + b


def _const_spec(shape):
    nd = len(shape)
    return pl.BlockSpec(shape, lambda *_: (0,) * nd)


def _params(sem):
    return pltpu.CompilerParams(dimension_semantics=sem, vmem_limit_bytes=VMEM_LIMIT_BYTES)


def _rwkv_pre_kernel(x_ref, bnd_ref, mu_ref, vec_ref, wr_ref, wk_ref, wv_ref, w1_ref, w2_ref,
                     a1_ref, a2_ref, g1_ref, g2_ref, th_ref, fh_ref,
                     r_out, ld_out, k_out, v_out, kk_out, ka_out, g_out, bonus_out):
    x = x_ref[...]
    rows = lax.broadcasted_iota(jnp.int32, x.shape, 0)
    xp = jnp.where(rows == 0, bnd_ref[0, 0:1, :], pltpu.roll(x, 1, 0))
    dx = xp - x

    def mix(s):
        return x + dx * mu_ref[s:s + 1, :]

    w0 = vec_ref[0:1, :]
    a0 = vec_ref[1:2, :]
    k_k = vec_ref[2:3, :]
    k_a = vec_ref[3:4, :]
    r_k = vec_ref[4:5, :]
    to_head = th_ref[...]
    from_head = fh_ref[...]

    r = _dot(mix(0), wr_ref[...])
    k = _dot(mix(1), wk_ref[...])
    v = _dot(mix(2), wv_ref[...])
    zw = -(w0 + _dot(jnp.tanh(_dot(mix(3), w1_ref[...])), w2_ref[...]))
    softplus = jnp.maximum(zw, 0.0) + jnp.log(1.0 + jnp.exp(-jnp.abs(zw)))
    ld = -jnp.exp(-softplus - 0.5)
    a = _sigmoid(a0 + _dot(_dot(mix(4), a1_ref[...]), a2_ref[...]))
    g = _dot(_sigmoid(_dot(mix(5), g1_ref[...])), g2_ref[...])

    kk = k * k_k
    ss = _head_sum(kk * kk, to_head, from_head)
    kk = kk / jnp.maximum(jnp.sqrt(ss), 1e-12)
    k_h = k * (1.0 + (a - 1.0) * k_a)
    bonus = _head_sum(r * k_h * r_k, to_head, from_head) * v

    r_out[...] = r
    ld_out[...] = ld
    k_out[...] = k_h
    v_out[...] = v
    kk_out[...] = kk
    ka_out[...] = kk * a
    g_out[...] = g
    bonus_out[...] = bonus


def _rwkv_pre(x, bnd, mu8, vec8, wr, wk, wv, w1, w2, a1, a2, g1, g2, to_head, from_head, tm):
    n, d = x.shape
    row = pl.BlockSpec((tm, d), lambda i: (i, 0))
    ins = [row, pl.BlockSpec((1, SUBLANES, d), lambda i: (i, 0, 0))]
    ins += [_const_spec(a.shape)
            for a in (mu8, vec8, wr, wk, wv, w1, w2, a1, a2, g1, g2, to_head, from_head)]
    return pl.pallas_call(
        _rwkv_pre_kernel,
        grid=(n // tm,),
        in_specs=ins,
        out_specs=[row] * 8,
        out_shape=[jax.ShapeDtypeStruct((n, d), F32)] * 8,
        compiler_params=_params(("parallel",)),
        name="rwkv_pre",
    )(x, bnd, mu8, vec8, wr, wk, wv, w1, w2, a1, a2, g1, g2, to_head, from_head)


def _wkv_scan_kernel(r_ref, ld_ref, k_ref, v_ref, kk_ref, ka_ref, s0_ref, o_ref, st_ref, s_scr,
                     *, chunk, n_chunks, group):
    c = chunk
    pw = 2 * HEAD_A
    n_pp = s_scr.shape[0]
    t_idx = pl.program_id(2)

    @pl.when(t_idx == 0)
    def _():
        s_scr[...] = s0_ref[...]

    ri = lax.broadcasted_iota(jnp.int32, (c, c), 0)
    ci = lax.broadcasted_iota(jnp.int32, (c, c), 1)
    tri_incl = ri >= ci
    cum_mat = jnp.where(tri_incl, 1.0, 0.0).astype(BF16)
    ri4 = lax.broadcasted_iota(jnp.int32, (c, 4 * c), 0)
    ci4 = lax.broadcasted_iota(jnp.int32, (c, 4 * c), 1) & (c - 1)
    strict4 = ri4 > ci4
    incl4 = ri4 >= ci4
    first_blk = lax.broadcasted_iota(jnp.int32, (1, 2 * c), 1) < c
    eye_cat = (lax.broadcasted_iota(jnp.int32, (c, 2 * c), 0)
               == (lax.broadcasted_iota(jnp.int32, (c, 2 * c), 1) & (c - 1)))
    lane1 = lax.broadcasted_iota(jnp.int32, (1, pw), 1)
    head_a1 = lane1 < HEAD_A
    lane2 = lax.broadcasted_iota(jnp.int32, (1, 2 * pw), 1) & (pw - 1)
    head_a2 = lane2 < HEAD_A
    rs = lax.broadcasted_iota(jnp.int32, (pw, pw), 0)
    cs = lax.broadcasted_iota(jnp.int32, (pw, pw), 1)
    same_head = (rs < HEAD_A) == (cs < HEAD_A)
    eye = rs == cs
    zeros_cv = jnp.zeros((c, pw), F32)

    def group_maps(slices):
        each = lambda fn, *lists: [fn(*a) for a in zip(*lists)]
        ld = [ld_ref[ix] for ix in slices]
        cw = each(lambda x: _dot_exact_lhs(cum_mat, x, 3), ld)
        w_in = each(jnp.exp, cw)
        w_ex = each(lambda a, b: jnp.exp(a - b), cw, ld)
        w_inv = each(lambda a: jnp.exp(-a), cw)
        w_last = each(lambda a: a[c - 1:c, :], w_in)
        knt = [-(kk_ref[ix] * w) for ix, w in zip(slices, w_ex)]
        kat = [ka_ref[ix] * w for ix, w in zip(slices, w_inv)]
        kt = [k_ref[ix] * w for ix, w in zip(slices, w_inv)]
        rt = [r_ref[ix] * w for ix, w in zip(slices, w_in)]
        v = [v_ref[ix] for ix in slices]
        by_head = lambda z, is_a: jnp.concatenate(
            [jnp.where(is_a, z, 0.0), jnp.where(is_a, 0.0, z)], axis=0)
        lh = each(lambda a, b: jnp.concatenate([a, b], axis=0), knt, rt)
        rh = each(lambda a, b: jnp.concatenate(
            [by_head(a, head_a1), by_head(b, head_a1)], axis=0), kat, kt)
        v_st = each(lambda a: by_head(a, head_a1), v)

        full = each(_dot_nt, lh, rh)
        top = each(lambda a: jnp.where(strict4, a[:c, :], 0.0), full)
        bot = each(lambda a: jnp.where(incl4, a[c:, :], 0.0), full)
        n_cat = each(lambda a: a[:, :2 * c], top)
        akv = each(lambda a, b: _dot(a[:, 2 * c:], b), top, v_st)

        t_cat = each(lambda n: jnp.where(eye_cat, 1.0, 0.0) + n, n_cat)
        span = 2
        while span < c:
            n_cat = each(lambda n: _dot(n, by_head(n, first_blk)), n_cat)
            dt = each(lambda n, t0: _dot(n, by_head(t0, first_blk)), n_cat, t_cat)
            t_cat = each(lambda t0, d: t0 + d, t_cat, dt)
            span *= 2
        x = each(lambda t0, a, b: _dot(t0, by_head(jnp.concatenate([a, b], axis=1), head_a2)),
                 t_cat, knt, akv)

        v_wide = each(lambda a: jnp.concatenate([jnp.zeros_like(a), a], axis=1), v_st)
        qo_all = each(lambda b4, x0, vw: _dot(b4, jnp.concatenate([by_head(x0, head_a2), vw], axis=0)),
                      bot, x, v_wide)
        rhs2 = each(lambda a, b: jnp.concatenate(
            [a, jnp.concatenate([zeros_cv, b], axis=1)], axis=0), x, v)
        lt = each(lambda a, b, w: jnp.concatenate([a * w, b * w], axis=0), kat, kt, w_last)
        mb = each(_dot_tn, lt, rhs2)
        out = []
        for j in range(len(slices)):
            qo = qo_all[j]
            q = rt[j] + qo[:, :pw]
            m = jnp.where(eye, w_last[j], 0.0) + jnp.where(same_head, mb[j][:, :pw], 0.0)
            b = jnp.where(same_head, mb[j][:, pw:], 0.0)
            mq = jnp.concatenate([m, q], axis=0)
            out.append((mq.astype(BF16), b, qo[:, pw:]))
        return out

    def body(it, carry):
        items = [(j, pp) for j in range(group) for pp in range(n_pp)]
        slices = [(pl.ds(pl.multiple_of((it * group + j) * c, c), c), pl.ds(pp * pw, pw))
                  for j, pp in items]
        maps = group_maps(slices)
        s = [s_scr[pp] for pp in range(n_pp)]
        for (j, pp), ix, (mq, b, o0) in zip(items, slices, maps):
            res = _dot(mq, s[pp])
            s[pp] = res[:pw, :] + b
            o_ref[ix] = res[pw:, :] + o0
        for pp in range(n_pp):
            s_scr[pp] = s[pp]
        return carry

    lax.fori_loop(0, n_chunks // group, body, 0)

    @pl.when(t_idx == pl.num_programs(2) - 1)
    def _():
        st_ref[...] = s_scr[...]


def _wkv_scan(r, ld, k, v, kk, ka, s0, chunk, t_blk):
    bn, t, d = r.shape
    pw = 2 * HEAD_A
    n_pairs = d // pw
    n_pp = SCAN_PAIRS
    seq = pl.BlockSpec((None, t_blk, n_pp * pw), lambda b, p, i: (b, i, p))
    st = pl.BlockSpec((None, n_pp, pw, pw), lambda b, p, i: (b, p, 0, 0))
    n_chunks = t_blk // chunk
    kern = functools.partial(_wkv_scan_kernel, chunk=chunk, n_chunks=n_chunks,
                             group=min(SCAN_GROUP, n_chunks))
    return pl.pallas_call(
        kern,
        grid=(bn, n_pairs // n_pp, t // t_blk),
        in_specs=[seq] * 6 + [st],
        out_specs=[seq, st],
        out_shape=[jax.ShapeDtypeStruct((bn, t, d), F32),
                   jax.ShapeDtypeStruct((bn, n_pairs, pw, pw), F32)],
        scratch_shapes=[pltpu.VMEM((n_pp, pw, pw), F32)],
        compiler_params=_params(("parallel", "parallel", "arbitrary")),
        name="wkv_scan",
    )(r, ld, k, v, kk, ka, s0)


def _rwkv_post_kernel(o_ref, bonus_ref, g_ref, x_ref, vec_ref, th_ref, fh_ref, wo_ref, out_ref):
    o = o_ref[...]
    to_head = th_ref[...]
    from_head = fh_ref[...]
    inv_n = 1.0 / HEAD_A
    mean = _head_sum(o, to_head, from_head) * inv_n
    oc = o - mean
    var = _head_sum(oc * oc, to_head, from_head) * inv_n
    on = oc * lax.rsqrt(var + GN_EPS) * vec_ref[0:1, :] + vec_ref[1:2, :]
    y = (on + bonus_ref[...]) * g_ref[...]
    h = _dot(y, wo_ref[...])
    out_ref[...] = _layer_norm(ALPHA * x_ref[...] + h, vec_ref[2:3, :], vec_ref[3:4, :])


def _rwkv_post(o, bonus, g, x, vec8, to_head, from_head, wo, tm):
    n, d = x.shape
    row = pl.BlockSpec((tm, d), lambda i: (i, 0))
    return pl.pallas_call(
        _rwkv_post_kernel,
        grid=(n // tm,),
        in_specs=[row] * 4 + [_const_spec(a.shape) for a in (vec8, to_head, from_head, wo)],
        out_specs=row,
        out_shape=jax.ShapeDtypeStruct((n, d), F32),
        compiler_params=_params(("parallel",)),
        name="rwkv_post",
    )(o, bonus, g, x, vec8, to_head, from_head, wo)


def _ffn_kernel(x_ref, wg_ref, wu_ref, wd_ref, ln_ref, out_ref, acc_ref, xb_ref):
    f = pl.program_id(1)

    @pl.when(f == 0)
    def _():
        acc_ref[...] = jnp.zeros_like(acc_ref)
        xb_ref[...] = x_ref[...].astype(BF16)

    xb = xb_ref[...]
    gate = jnp.dot(xb, wg_ref[...].astype(BF16), preferred_element_type=F32)
    up = jnp.dot(xb, wu_ref[...].astype(BF16), preferred_element_type=F32)
    h = gate * _sigmoid(gate) * up
    acc_ref[...] += jnp.dot(h.astype(BF16), wd_ref[...].astype(BF16), preferred_element_type=F32)

    @pl.when(f == pl.num_programs(1) - 1)
    def _():
        out_ref[...] = _layer_norm(ALPHA * x_ref[...] + acc_ref[...], ln_ref[0:1, :], ln_ref[1:2, :])


def _ffn(x, w_gu, w_down, ln8, tm, tf):
    n, d = x.shape
    d_ff = w_gu.shape[1] // 2
    nf = d_ff // tf
    return pl.pallas_call(
        _ffn_kernel,
        grid=(n // tm, nf),
        in_specs=[
            pl.BlockSpec((tm, d), lambda i, f: (i, 0)),
            pl.BlockSpec((d, tf), lambda i, f: (0, f)),
            pl.BlockSpec((d, tf), lambda i, f: (0, nf + f)),
            pl.BlockSpec((tf, d), lambda i, f: (f, 0)),
            _const_spec(ln8.shape),
        ],
        out_specs=pl.BlockSpec((tm, d), lambda i, f: (i, 0)),
        out_shape=jax.ShapeDtypeStruct((n, d), F32),
        scratch_shapes=[pltpu.VMEM((tm, d), F32), pltpu.VMEM((tm, d), BF16)],
        compiler_params=_params(("parallel", "arbitrary")),
        name="dense_ffn",
    )(x, w_gu, w_gu, w_down, ln8)


ROUTE_I1, ROUTE_I2, ROUTE_G1, ROUTE_G2, ROUTE_Q1, ROUTE_Q2 = range(6)
META_CARRY, META_COUNT = 0, 1


def _proj_ln_router_kernel(y_ref, x_ref, w_ref, ln_ref, rw_ref, out_ref, route_ref, meta_ref, count_ref,
                           carry_ref):
    h = _dot(y_ref[...], w_ref[...])
    x3 = _layer_norm(ALPHA * x_ref[...] + h, ln_ref[0:1, :], ln_ref[1:2, :])
    out_ref[...] = x3
    _route_rows(x3, rw_ref, route_ref, meta_ref, count_ref, carry_ref)


def _route_rows(x, w_ref, route_ref, meta_ref, count_ref, carry_ref):
    i = pl.program_id(0)

    @pl.when(i == 0)
    def _():
        carry_ref[...] = jnp.zeros_like(carry_ref)

    x_hi, x_lo = _split(x, 2)
    w_hi, w_lo = _split(w_ref[...], 2)
    logits = (jnp.dot(x_hi, w_hi, preferred_element_type=F32)
              + jnp.dot(x_lo, w_hi, preferred_element_type=F32)
              + jnp.dot(x_hi, w_lo, preferred_element_type=F32))
    tm = logits.shape[0]
    lane = lax.broadcasted_iota(jnp.int32, logits.shape, 1).astype(F32)
    neg = -jnp.inf
    logits = jnp.where(lane < N_EXPERTS, logits, neg)
    m1 = jnp.max(logits, axis=-1, keepdims=True)
    i1 = jnp.min(jnp.where(logits == m1, lane, float(LANES)), axis=-1, keepdims=True)
    rest = jnp.where(lane == i1, neg, logits)
    m2 = jnp.max(rest, axis=-1, keepdims=True)
    i2 = jnp.min(jnp.where(rest == m2, lane, float(LANES)), axis=-1, keepdims=True)
    e2 = jnp.exp(m2 - m1)
    den = 1.0 + e2

    sel1 = lane == i1
    sel2 = lane == i2
    onehot = jnp.where(sel1, 1.0, 0.0) + jnp.where(sel2, 1.0, 0.0)
    ri = lax.broadcasted_iota(jnp.int32, (tm, tm), 0)
    ci = lax.broadcasted_iota(jnp.int32, (tm, tm), 1)
    earlier = jnp.where(ri > ci, 1.0, 0.0).astype(BF16)
    in_tile = jnp.dot(earlier, onehot.astype(BF16), preferred_element_type=F32)
    q1 = jnp.sum(jnp.where(sel1, in_tile, 0.0), axis=-1, keepdims=True)
    q2 = jnp.sum(jnp.where(sel2, in_tile, 0.0), axis=-1, keepdims=True)
    tile_count = jnp.sum(onehot, axis=0, keepdims=True)

    route = jnp.zeros_like(logits)
    for col, val in ((ROUTE_I1, i1), (ROUTE_I2, i2), (ROUTE_G1, 1.0 / den), (ROUTE_G2, e2 / den),
                     (ROUTE_Q1, q1), (ROUTE_Q2, q2)):
        route = jnp.where(lane == float(col), val, route)
    route_ref[...] = route
    meta_row = lax.broadcasted_iota(jnp.int32, meta_ref.shape, 0)
    meta_ref[...] = jnp.where(meta_row == META_CARRY, carry_ref[0:1, :],
                              jnp.where(meta_row == META_COUNT, tile_count, 0.0))
    slab_rows = jnp.floor((tile_count + (SLAB_UNIT - 1)) * (1.0 / SLAB_UNIT)) * SLAB_UNIT
    carry_ref[0:1, :] = carry_ref[0:1, :] + slab_rows
    count_ref[...] = carry_ref[...]


def _proj_ln_router(y, x, w, ln8, router_pad, tm):
    n, d = x.shape
    row = pl.BlockSpec((tm, d), lambda i: (i, 0))
    return pl.pallas_call(
        _proj_ln_router_kernel,
        grid=(n // tm,),
        in_specs=[pl.BlockSpec((tm, y.shape[1]), lambda i: (i, 0)), row,
                  _const_spec(w.shape), _const_spec(ln8.shape), _const_spec(router_pad.shape)],
        out_specs=[row,
                   pl.BlockSpec((tm, LANES), lambda i: (i, 0)),
                   pl.BlockSpec((None, SUBLANES, LANES), lambda i: (i, 0, 0)),
                   _const_spec((SUBLANES, LANES))],
        out_shape=[jax.ShapeDtypeStruct((n, d), F32),
                   jax.ShapeDtypeStruct((n, LANES), F32),
                   jax.ShapeDtypeStruct((n // tm, SUBLANES, LANES), F32),
                   jax.ShapeDtypeStruct((SUBLANES, LANES), F32)],
        scratch_shapes=[pltpu.VMEM((SUBLANES, LANES), F32)],
        compiler_params=_params(("arbitrary",)),
        name="proj_ln_router",
    )(y, x, w, ln8, router_pad)


def _slab_plan(meta_ref):
    plan = []
    off = 0
    for e in range(N_EXPERTS):
        units = (meta_ref[0, e] + (SLAB_UNIT - 1)) // SLAB_UNIT
        plan.append((units, meta_ref[0, N_EXPERTS + e], off))
        off = off + units * SLAB_UNIT
    return plan


def _slab_buffer_rows(tm):
    worst = 2 * tm + N_EXPERTS * (SLAB_UNIT - 1)
    return ((worst + LANES - 1) // LANES) * LANES


def _slab_rows(expert, rank, plan):
    off = jnp.zeros_like(expert)
    for e, (_, _, e_off) in enumerate(plan):
        off = jnp.where(expert == e, e_off, off)
    return off + rank


def _slab_copies(plan, make_copy):
    total = 0
    for units, first_row, off in plan:
        def start(u, carry, first_row=first_row, off=off):
            make_copy(pl.multiple_of(off + u * SLAB_UNIT, SLAB_UNIT),
                      pl.multiple_of(first_row + u * SLAB_UNIT, SLAB_UNIT)).start()
            return carry
        lax.fori_loop(0, units, start, 0)
        total = total + units
    return total


def _dispatch_kernel(meta_ref, route_ref, x_ref, xs_in_ref, xs_ref, cbuf, sem):
    del xs_in_ref
    tm = x_ref.shape[0]
    plan = _slab_plan(meta_ref)

    sel = (lax.broadcasted_iota(jnp.int32, (SUBLANES, LANES), 0)
           == lax.broadcasted_iota(jnp.int32, (SUBLANES, LANES), 1)).astype(BF16)
    route_t = None
    for piece in _split(route_ref[...], 3):
        t = lax.dot_general(sel, piece, (((1,), (1,)), ((), ())), preferred_element_type=F32)
        route_t = t if route_t is None else route_t + t
    route_t = route_t.astype(jnp.int32)
    row1 = _slab_rows(route_t[ROUTE_I1:ROUTE_I1 + 1, :], route_t[ROUTE_Q1:ROUTE_Q1 + 1, :], plan)
    row2 = _slab_rows(route_t[ROUTE_I2:ROUTE_I2 + 1, :], route_t[ROUTE_Q2:ROUTE_Q2 + 1, :], plan)

    r_iota = lax.broadcasted_iota(jnp.int32, (cbuf.shape[0], tm), 0)
    perm = jnp.where((r_iota == row1) | (r_iota == row2), 1.0, 0.0).astype(BF16)
    cbuf[...] = jnp.dot(perm, x_ref[...].astype(BF16), preferred_element_type=F32)

    def make_copy(buf_row, sorted_row):
        return pltpu.make_async_copy(cbuf.at[pl.ds(buf_row, SLAB_UNIT), :],
                                     xs_ref.at[pl.ds(sorted_row, SLAB_UNIT), :], sem)

    n_started = _slab_copies(plan, make_copy)

    def wait(u, carry):
        make_copy(0, 0).wait()
        return carry

    lax.fori_loop(0, n_started, wait, 0)


def _dispatch(meta, route, x, xs, tm):
    n, d = x.shape
    return pl.pallas_call(
        _dispatch_kernel,
        grid=(n // tm,),
        in_specs=[pl.BlockSpec((None, 1, 2 * N_EXPERTS), lambda i: (i, 0, 0), memory_space=pltpu.SMEM),
                  pl.BlockSpec((tm, LANES), lambda i: (i, 0)),
                  pl.BlockSpec((tm, d), lambda i: (i, 0)),
                  pl.BlockSpec(memory_space=pl.ANY)],
        out_specs=pl.BlockSpec(memory_space=pl.ANY),
        out_shape=jax.ShapeDtypeStruct(xs.shape, xs.dtype),
        scratch_shapes=[pltpu.VMEM((_slab_buffer_rows(tm), d), F32), pltpu.SemaphoreType.DMA(())],
        input_output_aliases={3: 0},
        compiler_params=_params(("arbitrary",)),
        name="moe_dispatch",
    )(meta, route, x, xs)


def _experts_kernel(te_ref, rows_ref, x_ref, wg_ref, wu_ref, wd_ref, out_ref, acc_ref, xb_ref):
    del te_ref
    f = pl.program_id(1)
    rows = rows_ref[pl.program_id(0)]
    tm = x_ref.shape[0]

    @pl.when(f == 0)
    def _():
        acc_ref[...] = jnp.zeros_like(acc_ref)

    def swiglu_rows(n_rows):
        sl = pl.ds(0, n_rows)

        @pl.when(f == 0)
        def _():
            xb_ref[sl, :] = x_ref[sl, :].astype(BF16)

        xb = xb_ref[sl, :]
        gate = jnp.dot(xb, wg_ref[...].astype(BF16), preferred_element_type=F32)
        up = jnp.dot(xb, wu_ref[...].astype(BF16), preferred_element_type=F32)
        h = gate * _sigmoid(gate) * up
        acc_ref[sl, :] += jnp.dot(h.astype(BF16), wd_ref[...].astype(BF16), preferred_element_type=F32)

    part = tm // MOE_TILE_PARTS
    for k in range(1, MOE_TILE_PARTS + 1):
        @pl.when(jnp.logical_and(rows > (k - 1) * part, rows <= k * part))
        def _(k=k):
            swiglu_rows(k * part)

    @pl.when(f == pl.num_programs(1) - 1)
    def _():
        out_ref[...] = acc_ref[...]


def _experts(tile_expert, tile_rows, xs, w_gu, w_down, tm, tf):
    s_total, d = xs.shape
    d_ff = w_gu.shape[2] // 2
    nf = d_ff // tf

    def f_eff(i, f, rows):
        return jnp.where(rows[i] > 0, f, nf - 1)

    grid_spec = pltpu.PrefetchScalarGridSpec(
        num_scalar_prefetch=2,
        grid=(s_total // tm, nf),
        in_specs=[
            pl.BlockSpec((tm, d), lambda i, f, te, rows: (i, 0)),
            pl.BlockSpec((None, d, tf), lambda i, f, te, rows: (te[i], 0, f_eff(i, f, rows))),
            pl.BlockSpec((None, d, tf), lambda i, f, te, rows: (te[i], 0, nf + f_eff(i, f, rows))),
            pl.BlockSpec((None, tf, d), lambda i, f, te, rows: (te[i], f_eff(i, f, rows), 0)),
        ],
        out_specs=pl.BlockSpec((tm, d), lambda i, f, te, rows: (i, 0)),
        scratch_shapes=[pltpu.VMEM((tm, d), F32), pltpu.VMEM((tm, d), BF16)],
    )
    return pl.pallas_call(
        _experts_kernel,
        grid_spec=grid_spec,
        out_shape=jax.ShapeDtypeStruct((s_total, d), F32),
        compiler_params=_params(("arbitrary", "arbitrary")),
        name="moe_experts",
    )(tile_expert, tile_rows, xs, w_gu, w_gu, w_down)


def _combine_kernel(meta_ref, route_ref, x_ref, ys_ref, ln_ref, out_ref, cbuf, sem):
    tm = x_ref.shape[0]
    plan = _slab_plan(meta_ref)

    @pl.when(pl.program_id(0) == 0)
    def _():
        cbuf[...] = jnp.zeros_like(cbuf)

    def make_copy(buf_row, sorted_row):
        return pltpu.make_async_copy(ys_ref.at[pl.ds(sorted_row, SLAB_UNIT), :],
                                     cbuf.at[pl.ds(buf_row, SLAB_UNIT), :], sem)

    n_started = _slab_copies(plan, make_copy)

    def wait(u, carry):
        make_copy(0, 0).wait()
        return carry

    lax.fori_loop(0, n_started, wait, 0)

    route = route_ref[...]
    col = lambda k: route[:, k:k + 1]
    row1 = _slab_rows(col(ROUTE_I1).astype(jnp.int32), col(ROUTE_Q1).astype(jnp.int32), plan)
    row2 = _slab_rows(col(ROUTE_I2).astype(jnp.int32), col(ROUTE_Q2).astype(jnp.int32), plan)
    c_iota = lax.broadcasted_iota(jnp.int32, (tm, cbuf.shape[0]), 1)
    pick = jnp.concatenate([jnp.where(c_iota == row1, 1.0, 0.0), jnp.where(c_iota == row2, 1.0, 0.0)],
                           axis=0).astype(BF16)
    y12 = _dot_exact_lhs(pick, cbuf[...], 2)
    y = col(ROUTE_G1) * y12[:tm, :] + col(ROUTE_G2) * y12[tm:, :]
    out_ref[...] = _layer_norm(ALPHA * x_ref[...] + y, ln_ref[0:1, :], ln_ref[1:2, :])


def _combine(meta, route, x, ys, ln8, tm):
    n, d = x.shape
    return pl.pallas_call(
        _combine_kernel,
        grid=(n // tm,),
        in_specs=[pl.BlockSpec((None, 1, 2 * N_EXPERTS), lambda i: (i, 0, 0), memory_space=pltpu.SMEM),
                  pl.BlockSpec((tm, LANES), lambda i: (i, 0)),
                  pl.BlockSpec((tm, d), lambda i: (i, 0)),
                  pl.BlockSpec(memory_space=pl.ANY),
                  _const_spec(ln8.shape)],
        out_specs=pl.BlockSpec((tm, d), lambda i: (i, 0)),
        out_shape=jax.ShapeDtypeStruct((n, d), F32),
        scratch_shapes=[pltpu.VMEM((_slab_buffer_rows(tm), d), F32), pltpu.SemaphoreType.DMA(())],
        compiler_params=_params(("arbitrary",)),
        name="moe_combine",
    )(meta, route, x, ys, ln8)


def _rope(y, cos_t, sin_next, sin_prev):
    n = y.shape[1]
    reps = n // LANES
    tile = lambda t: jnp.concatenate([t] * reps, axis=1) if reps > 1 else t
    half = ROPE_DIM // 2
    return (y * tile(cos_t)
            + pltpu.roll(y, n - half, 1) * tile(sin_next)
            + pltpu.roll(y, half, 1) * tile(sin_prev))


def _qkv_kernel(x_ref, wq_ref, wkv_ref, cos_ref, sn_ref, sp_ref, q_out, k_out, v_out):
    xb = x_ref[...].astype(BF16)
    tables = (cos_ref[...], sn_ref[...], sp_ref[...])
    q_out[...] = _rope(jnp.dot(xb, wq_ref[...], preferred_element_type=F32), *tables)
    kv = jnp.dot(xb, wkv_ref[...], preferred_element_type=F32)
    nk = k_out.shape[1]
    k_out[...] = _rope(kv[:, :nk], *tables)
    v_out[...] = kv[:, nk:]


def _qkv_proj(x, wq, wkv, tables, tm):
    n, d = x.shape
    kvw = wkv.shape[1] // 2
    t_tiles = tables[0].shape[0] // tm
    tab = pl.BlockSpec((tm, LANES), lambda i: (i % t_tiles, 0))
    widths = (wq.shape[1], kvw, kvw)
    return pl.pallas_call(
        _qkv_kernel,
        grid=(n // tm,),
        in_specs=[pl.BlockSpec((tm, d), lambda i: (i, 0)), _const_spec(wq.shape), _const_spec(wkv.shape),
                  tab, tab, tab],
        out_specs=[pl.BlockSpec((tm, ow), lambda i: (i, 0)) for ow in widths],
        out_shape=[jax.ShapeDtypeStruct((n, ow), F32) for ow in widths],
        compiler_params=_params(("parallel",)),
        name="qkv_proj",
    )(x, wq, wkv, *tables)


def _attn_kernel(sink_ref, q_ref, kp_ref, kc_ref, vp_ref, vc_ref, o_ref, *, banded):
    tq = q_ref.shape[0]
    n_prev = kp_ref.shape[0]
    tk = n_prev + kc_ref.shape[0]
    q = q_ref[...] * ATTN_SCALE
    kband = jnp.concatenate([kp_ref[...], kc_ref[...]], axis=0)
    vband = jnp.concatenate([vp_ref[...], vc_ref[...]], axis=0)
    n_heads = q.shape[1] // HEAD_B
    group = n_heads // KV_HEADS
    qc = CHUNK if banded else tq
    kc = WINDOW + CHUNK if banded else tk
    units = [(kh, ci) for kh in range(KV_HEADS) for ci in range(tq // qc)]
    if banded:
        kj = lax.broadcasted_iota(jnp.int32, (kc, group * qc), 0)
        band_start = pl.program_id(1) * tq - n_prev

    qlane = lax.broadcasted_iota(jnp.int32, (1, group * qc), 1)

    def scores(kh, ci):
        qs = jnp.concatenate([q[ci * qc:(ci + 1) * qc, (kh * group + j) * HEAD_B:(kh * group + j + 1) * HEAD_B]
                              for j in range(group)], axis=0)
        return _dot_nt(kband[ci * qc:ci * qc + kc, kh * HEAD_B:(kh + 1) * HEAD_B], qs)

    def softmax(kh, ci, s):
        sk = jnp.full((1, group * qc), sink_ref[kh * group], F32)
        for j in range(1, group):
            sk = jnp.where(qlane >= j * qc, sink_ref[kh * group + j], sk)
        if banded:
            s = jnp.where(band_start + ci * qc + kj >= 0, s, -jnp.inf)
        m = jnp.maximum(jnp.max(s, axis=0, keepdims=True), sk)
        p = jnp.exp(s - m)
        return p * (1.0 / (jnp.sum(p, axis=0, keepdims=True) + jnp.exp(sk - m)))

    def weighted(kh, ci, p):
        return _dot_tn(p, vband[ci * qc:ci * qc + kc, kh * HEAD_B:(kh + 1) * HEAD_B])

    chunks = range(tq // qc)
    outs = {}
    s_next = [scores(0, ci) for ci in chunks]
    for kh in range(KV_HEADS):
        s_cur = s_next
        if kh + 1 < KV_HEADS:
            s_next = [scores(kh + 1, ci) for ci in chunks]
        probs = [softmax(kh, ci, s_cur[ci]) for ci in chunks]
        for ci in chunks:
            o = weighted(kh, ci, probs[ci])
            for j in range(group):
                outs[(kh * group + j, ci)] = o[j * qc:(j + 1) * qc, :]
    o_ref[...] = jnp.concatenate(
        [jnp.concatenate([outs[(h, ci)] for ci in range(tq // qc)], axis=0) for h in range(n_heads)],
        axis=1)


def _attention(q, k_prev_src, k_cur_src, v_prev_src, v_cur_src, sinks, tq, banded):
    bn, t, d = q.shape
    kw = k_cur_src.shape[2]
    if banded:
        ratio = tq // WINDOW
        prev_map = lambda b, i: (b, jnp.maximum(i * ratio - 1, 0), 0)
    else:
        prev_map = lambda b, i: (b, 0, 0)
    prev = pl.BlockSpec((None, WINDOW, kw), prev_map)
    cur = pl.BlockSpec((None, tq, kw), lambda b, i: (b, i, 0))
    kern = functools.partial(_attn_kernel, banded=banded)
    return pl.pallas_call(
        kern,
        grid=(bn, t // tq),
        in_specs=[pl.BlockSpec(memory_space=pltpu.SMEM),
                  pl.BlockSpec((None, tq, d), lambda b, i: (b, i, 0)), prev, cur, prev, cur],
        out_specs=pl.BlockSpec((None, tq, d), lambda b, i: (b, i, 0)),
        out_shape=jax.ShapeDtypeStruct((bn, t, d), F32),
        compiler_params=_params(("parallel", "parallel")),
        name="swa_attn",
    )(sinks, q, k_prev_src, k_cur_src, v_prev_src, v_cur_src)


def _moe_layer(groups, P):
    xs_rows = [g[0] for g in groups]
    routed = [g[1:] for g in groups]
    d = xs_rows[0].shape[1]
    tm_e = MOE_TILE
    counts = [c[0, :N_EXPERTS].astype(jnp.int32) for _, _, c in routed]
    total = sum(counts)
    padded = ((total + tm_e - 1) // tm_e) * tm_e
    ends = jnp.cumsum(padded)
    starts = ends - padded
    n_slabs = sum(-(-x.shape[0] // MOE_ROW_TILE) for x in xs_rows) * N_EXPERTS
    n_assign = 2 * sum(x.shape[0] for x in xs_rows) + n_slabs * (SLAB_UNIT - 1)
    n_tiles = (n_assign + N_EXPERTS * (tm_e - 1)) // tm_e
    n_used = (ends[-1] // tm_e).astype(jnp.int32)
    tile_expert = jnp.sum((jnp.arange(n_tiles) * tm_e)[:, None] >= ends[None, :], axis=1).astype(jnp.int32)
    tile_expert = jnp.minimum(tile_expert, N_EXPERTS - 1)
    tile_expert = jnp.where(jnp.arange(n_tiles) < n_used, tile_expert,
                            tile_expert[jnp.maximum(n_used - 1, 0)])
    tile_rows = jnp.clip((starts + total)[tile_expert] - jnp.arange(n_tiles) * tm_e, 0, tm_e)
    tile_rows = jnp.where(jnp.arange(n_tiles) < n_used, tile_rows, 0).astype(jnp.int32)

    metas = []
    base = starts
    for (_, meta, _), cnt in zip(routed, counts):
        carry = meta[:, META_CARRY, :N_EXPERTS].astype(jnp.int32)
        sent = meta[:, META_COUNT, :N_EXPERTS].astype(jnp.int32)
        metas.append(jnp.concatenate([sent, base[None, :] + carry], axis=1)[:, None, :])
        base = base + cnt

    xs = jnp.zeros((n_tiles * tm_e, d), F32)
    for m, (route, _, _), x in zip(metas, routed, xs_rows):
        xs = _dispatch(m, route, x, xs, min(MOE_ROW_TILE, x.shape[0]))
    ys = _experts(tile_expert, tile_rows, xs, P['moe_w_gu'][0], P['moe_w_down'][0], tm_e, 512)
    ln11 = _pad_rows([P['ln_g'][1, 1], P['ln_b'][1, 1]], d)
    return [_combine(m, route, x, ys, ln11, min(MOE_ROW_TILE, x.shape[0]))
            for m, (route, _, _), x in zip(metas, routed, xs_rows)]


def _pad_rows(rows, d):
    a = jnp.stack(rows).astype(F32)
    return jnp.concatenate([a, jnp.zeros((SUBLANES - a.shape[0], d), F32)], axis=0)


def _rope_tables(pos, reps):
    inv_freq = ROPE_THETA ** (-jnp.arange(0, ROPE_DIM, 2, dtype=jnp.float32) / ROPE_DIM)
    ang = pos.astype(jnp.float32)[:, None] * inv_freq[None, :]
    cos = jnp.cos(ang)
    sin = jnp.sin(ang)
    t = pos.shape[0]
    half = ROPE_DIM // 2
    rest = HEAD_B - ROPE_DIM
    z_half = jnp.zeros((t, half), F32)
    z_rest = jnp.zeros((t, rest), F32)
    cos_h = jnp.concatenate([cos, cos, jnp.ones((t, rest), F32)], axis=1)
    sn_h = jnp.concatenate([-sin, z_half, z_rest], axis=1)
    sp_h = jnp.concatenate([z_half, sin, z_rest], axis=1)
    per_tile = LANES // HEAD_B
    return tuple(jnp.tile(a, (reps, per_tile)) for a in (cos_h, sn_h, sp_h))


def _trunk(x, shift_in, wkv_in, k_cache, v_cache, pos0, P):
    bn, t, d = x.shape
    n = bn * t
    h_a = d // HEAD_A
    pw = 2 * HEAD_A
    n_pairs = d // pw
    xf = x.reshape(n, d)

    tm_pre = min(256, t)
    tm_row = min(256, n)
    tm_ffn = min(1024, n)
    chunk = min(CHUNK, t)
    t_blk = min(512, t)

    tiles = jnp.arange(n // tm_pre) * tm_pre
    prev_rows = xf[jnp.maximum(tiles - 1, 0)]
    start_rows = shift_in[0][tiles // t]
    bnd = jnp.where(((tiles % t) == 0)[:, None], start_rows, prev_rows)
    bnd = jnp.broadcast_to(bnd[:, None, :], (n // tm_pre, SUBLANES, d))

    mu8 = jnp.concatenate([P['a_mu'][0], jnp.zeros((2, d), F32)], axis=0)
    vec_pre = _pad_rows([P['a_w0'][0], P['a_a0'][0], P['a_k_k'][0], P['a_k_a'][0],
                         P['a_r_k'][0].reshape(d)], d)
    lane_head = jnp.arange(d) // HEAD_A
    to_head = (lane_head[:, None] == jnp.arange(LANES)[None, :]).astype(BF16)
    from_head = to_head.T
    bf = lambda a: a.astype(BF16)
    w_rkv = P['a_w_rkv'][0]
    r, ld, k_h, v, kk, ka, g, bonus = _rwkv_pre(
        xf, bnd, mu8, vec_pre, bf(w_rkv[0]), bf(w_rkv[1]), bf(w_rkv[2]),
        bf(P['a_w1'][0]), bf(P['a_w2'][0]), bf(P['a_a1'][0]), bf(P['a_a2'][0]),
        bf(P['a_g1'][0]), bf(P['a_g2'][0]), to_head, from_head, tm_pre)

    st_t = jnp.swapaxes(wkv_in[0].astype(F32), -1, -2).reshape(bn, n_pairs, 2, HEAD_A, HEAD_A)
    eye2 = jnp.eye(2, dtype=F32)
    s0 = jnp.einsum('bpajk,ac->bpajck', st_t, eye2).reshape(bn, n_pairs, pw, pw)
    seq3 = lambda a: a.reshape(bn, t, d)
    o, s_fin = _wkv_scan(seq3(r), seq3(ld), seq3(k_h), seq3(v), seq3(kk), seq3(ka), s0, chunk, t_blk)
    s_fin = s_fin.reshape(bn, n_pairs, 2, HEAD_A, 2, HEAD_A)
    s_fin = jnp.stack([s_fin[:, :, 0, :, 0, :], s_fin[:, :, 1, :, 1, :]], axis=2)
    wkv_out = jnp.swapaxes(s_fin, -1, -2).reshape(bn, h_a, HEAD_A, HEAD_A)
    shift_out = x[:, -1]

    vec_post = _pad_rows([P['a_lnx_g'][0], P['a_lnx_b'][0], P['ln_g'][0, 0], P['ln_b'][0, 0]], d)
    x1 = _rwkv_post(o.reshape(n, d), bonus, g, xf, vec_post, to_head, from_head, bf(P['a_w_o'][0]), tm_row)

    ln01 = _pad_rows([P['ln_g'][0, 1], P['ln_b'][0, 1]], d)
    x2 = _ffn(x1, P['ffn_w_gu'][0], P['ffn_w_down'][0], ln01, tm_ffn, 512)

    pos = pos0 + jnp.arange(t, dtype=jnp.int32)
    tm_qkv = min(512, n)
    tables = _rope_tables(pos, max(tm_qkv // t, 1))
    kvw = KV_HEADS * HEAD_B
    q, k_new, v_new = _qkv_proj(x2, bf(P['b_w_q'][0]), bf(P['kv_w']), tables, tm_qkv)
    k_new = k_new.reshape(bn, t, kvw)
    v_new = v_new.reshape(bn, t, kvw)
    q = q.reshape(bn, t, d)
    sinks = P['b_sinks'][0].astype(F32)
    if k_cache is None:
        att = _attention(q, k_new, k_new, v_new, v_new, sinks, min(256, t), banded=True)
        k_out = k_new[:, -WINDOW:]
        v_out = v_new[:, -WINDOW:]
    else:
        kc = k_cache.astype(F32).reshape(bn, WINDOW, kvw)
        vc = v_cache.astype(F32).reshape(bn, WINDOW, kvw)
        att = _attention(q, kc, k_new, vc, v_new, sinks, t, banded=False)
        k_out = jnp.concatenate([kc, k_new], axis=1)[:, -WINDOW:]
        v_out = jnp.concatenate([vc, v_new], axis=1)[:, -WINDOW:]
    ln10 = _pad_rows([P['ln_g'][1, 0], P['ln_b'][1, 0]], d)
    router_pad = jnp.concatenate([P['moe_router'][0], jnp.zeros((d, LANES - N_EXPERTS), F32)], axis=1)
    routed = _proj_ln_router(att.reshape(n, d), x2, bf(P['b_w_o'][0]), ln10, router_pad,
                             min(MOE_ROW_TILE, n))

    return (routed, shift_out[None], wkv_out[None],
            k_out.reshape(bn, WINDOW, KV_HEADS, HEAD_B), v_out.reshape(bn, WINDOW, KV_HEADS, HEAD_B))


def kernel(x_prompt, x_sample, cache_shift_a, state_wkv_a, cache_k_b, cache_v_b, a_mu, a_w_rkv, a_w0, a_w1, a_w2, a_a0, a_a1, a_a2, a_g1, a_g2, a_k_k, a_k_a, a_r_k, a_lnx_g, a_lnx_b, a_w_o, kv_w, b_w_q, b_sinks, b_w_o, ln_g, ln_b, ffn_w_gu, ffn_w_down, moe_router, moe_w_gu, moe_w_down):
    P = {
        'a_mu': a_mu, 'a_w_rkv': a_w_rkv, 'a_w0': a_w0, 'a_w1': a_w1, 'a_w2': a_w2,
        'a_a0': a_a0, 'a_a1': a_a1, 'a_a2': a_a2, 'a_g1': a_g1, 'a_g2': a_g2,
        'a_k_k': a_k_k, 'a_k_a': a_k_a, 'a_r_k': a_r_k, 'a_lnx_g': a_lnx_g,
        'a_lnx_b': a_lnx_b, 'a_w_o': a_w_o, 'kv_w': kv_w, 'b_w_q': b_w_q,
        'b_sinks': b_sinks, 'b_w_o': b_w_o, 'ln_g': ln_g, 'ln_b': ln_b,
        'ffn_w_gu': ffn_w_gu, 'ffn_w_down': ffn_w_down, 'moe_router': moe_router,
        'moe_w_gu': moe_w_gu, 'moe_w_down': moe_w_down,
    }
    bp = x_prompt.shape[0]
    d = x_prompt.shape[2]
    h_a = d // HEAD_A
    zero_shift = jnp.zeros((1, bp, d), x_prompt.dtype)
    zero_wkv = jnp.zeros((1, bp, h_a, HEAD_A, HEAD_A), F32)
    x3_p, p_shift, p_wkv, p_k, p_v = _trunk(x_prompt, zero_shift, zero_wkv, None, None, 0, P)
    x3_s, s_shift, s_wkv, s_k, s_v = _trunk(x_sample, cache_shift_a, state_wkv_a,
                                            cache_k_b, cache_v_b, PAST_LEN, P)
    y_p, y_s = _moe_layer([x3_p, x3_s], P)
    return (y_p.reshape(x_prompt.shape), y_s.reshape(x_sample.shape),
            p_shift, p_wkv, p_k, p_v, s_shift, s_wkv, s_k, s_v)
```

```python
import functools

import jax
import jax.numpy as jnp
from jax import lax
from jax.experimental import pallas as pl
from jax.experimental.pallas import tpu as pltpu

F32 = jnp.float32
BF16 = jnp.bfloat16

DEPTH = 2
HEAD_A = 64
HEAD_B = 64
KV_HEADS = 4
CHUNK = 64
WINDOW = 128
PAST_LEN = 4096
ROPE_DIM = HEAD_B // 4
ROPE_THETA = 500000.0
ATTN_SCALE = HEAD_B ** -0.5
N_EXPERTS = 8
GN_EPS = 64e-5
LN_EPS = 1e-5
ALPHA = (2.0 * DEPTH) ** 0.25

LANES = 128
SUBLANES = 8
VMEM_LIMIT_BYTES = 56 * 1024 * 1024
SCAN_GROUP = 8
SCAN_PAIRS = 2
MOE_TILE = 1024
MOE_ROW_TILE = 512
MOE_TILE_PARTS = 4
SLAB_UNIT = SUBLANES


def _dot(a, b):
    return jnp.dot(a.astype(BF16), b.astype(BF16), preferred_element_type=F32)


def _dot_nt(a, b):
    return lax.dot_general(a.astype(BF16), b.astype(BF16), (((1,), (1,)), ((), ())),
                           preferred_element_type=F32)


def _dot_tn(a, b):
    return lax.dot_general(a.astype(BF16), b.astype(BF16), (((0,), (0,)), ((), ())),
                           preferred_element_type=F32)


def _split(x, n):
    parts = []
    rem = x
    for i in range(n):
        p = rem.astype(BF16)
        parts.append(p)
        if i + 1 < n:
            rem = rem - p.astype(F32)
    return parts


def _dot_exact_rhs(a, b_bf16, n):
    acc = None
    for p in _split(a, n):
        t = jnp.dot(p, b_bf16, preferred_element_type=F32)
        acc = t if acc is None else acc + t
    return acc


def _dot_exact_lhs(a_bf16, b, n):
    acc = None
    for p in _split(b, n):
        t = jnp.dot(a_bf16, p, preferred_element_type=F32)
        acc = t if acc is None else acc + t
    return acc


def _head_sum(x, to_head, from_head):
    return _dot_exact_rhs(_dot_exact_rhs(x, to_head, 2), from_head, 3)


def _sigmoid(z):
    return 1.0 / (1.0 + jnp.exp(-z))


def _layer_norm(z, g, b):
    mu = jnp.mean(z, axis=-1, keepdims=True)
    zc = z - mu
    var = jnp.mean(zc * zc, axis=-1, keepdims=True)
    return zc * lax.rsqrt(var + LN_EPS) * g + b


def _const_spec(shape):
    nd = len(shape)
    return pl.BlockSpec(shape, lambda *_: (0,) * nd)


def _params(sem):
    return pltpu.CompilerParams(dimension_semantics=sem, vmem_limit_bytes=VMEM_LIMIT_BYTES)


def _rwkv_pre_kernel(x_ref, bnd_ref, mu_ref, vec_ref, wr_ref, wk_ref, wv_ref, w1_ref, w2_ref,
                     a1_ref, a2_ref, g1_ref, g2_ref, th_ref, fh_ref,
                     r_out, ld_out, k_out, v_out, kk_out, ka_out, g_out, bonus_out):
    x = x_ref[...]
    rows = lax.broadcasted_iota(jnp.int32, x.shape, 0)
    xp = jnp.where(rows == 0, bnd_ref[0, 0:1, :], pltpu.roll(x, 1, 0))
    dx = xp - x

    def mix(s):
        return x + dx * mu_ref[s:s + 1, :]

    w0 = vec_ref[0:1, :]
    a0 = vec_ref[1:2, :]
    k_k = vec_ref[2:3, :]
    k_a = vec_ref[3:4, :]
    r_k = vec_ref[4:5, :]
    to_head = th_ref[...]
    from_head = fh_ref[...]

    r = _dot(mix(0), wr_ref[...])
    k = _dot(mix(1), wk_ref[...])
    v = _dot(mix(2), wv_ref[...])
    zw = -(w0 + _dot(jnp.tanh(_dot(mix(3), w1_ref[...])), w2_ref[...]))
    softplus = jnp.maximum(zw, 0.0) + jnp.log(1.0 + jnp.exp(-jnp.abs(zw)))
    ld = -jnp.exp(-softplus - 0.5)
    a = _sigmoid(a0 + _dot(_dot(mix(4), a1_ref[...]), a2_ref[...]))
    g = _dot(_sigmoid(_dot(mix(5), g1_ref[...])), g2_ref[...])

    kk = k * k_k
    ss = _head_sum(kk * kk, to_head, from_head)
    kk = kk / jnp.maximum(jnp.sqrt(ss), 1e-12)
    k_h = k * (1.0 + (a - 1.0) * k_a)
    bonus = _head_sum(r * k_h * r_k, to_head, from_head) * v

    r_out[...] = r
    ld_out[...] = ld
    k_out[...] = k_h
    v_out[...] = v
    kk_out[...] = kk
    ka_out[...] = kk * a
    g_out[...] = g
    bonus_out[...] = bonus


def _rwkv_pre(x, bnd, mu8, vec8, wr, wk, wv, w1, w2, a1, a2, g1, g2, to_head, from_head, tm):
    n, d = x.shape
    row = pl.BlockSpec((tm, d), lambda i: (i, 0))
    ins = [row, pl.BlockSpec((1, SUBLANES, d), lambda i: (i, 0, 0))]
    ins += [_const_spec(a.shape)
            for a in (mu8, vec8, wr, wk, wv, w1, w2, a1, a2, g1, g2, to_head, from_head)]
    return pl.pallas_call(
        _rwkv_pre_kernel,
        grid=(n // tm,),
        in_specs=ins,
        out_specs=[row] * 8,
        out_shape=[jax.ShapeDtypeStruct((n, d), F32)] * 8,
        compiler_params=_params(("parallel",)),
        name="rwkv_pre",
    )(x, bnd, mu8, vec8, wr, wk, wv, w1, w2, a1, a2, g1, g2, to_head, from_head)


def _wkv_scan_kernel(r_ref, ld_ref, k_ref, v_ref, kk_ref, ka_ref, s0_ref, o_ref, st_ref, s_scr,
                     *, chunk, n_chunks, group):
    c = chunk
    pw = 2 * HEAD_A
    n_pp = s_scr.shape[0]
    t_idx = pl.program_id(2)

    @pl.when(t_idx == 0)
    def _():
        zeros_hh = jnp.zeros((HEAD_A, HEAD_A), F32)
        for pp in range(n_pp):
            s_scr[pp] = jnp.concatenate(
                [jnp.concatenate([s0_ref[pp, 0], zeros_hh], axis=1),
                 jnp.concatenate([zeros_hh, s0_ref[pp, 1]], axis=1)], axis=0)

    ri = lax.broadcasted_iota(jnp.int32, (c, c), 0)
    ci = lax.broadcasted_iota(jnp.int32, (c, c), 1)
    tri_incl = ri >= ci
    cum_mat = jnp.where(tri_incl, 1.0, 0.0).astype(BF16)
    ri4 = lax.broadcasted_iota(jnp.int32, (c, 4 * c), 0)
    ci4 = lax.broadcasted_iota(jnp.int32, (c, 4 * c), 1) & (c - 1)
    strict4 = ri4 > ci4
    incl4 = ri4 >= ci4
    first_blk = lax.broadcasted_iota(jnp.int32, (1, 2 * c), 1) < c
    eye_cat = (lax.broadcasted_iota(jnp.int32, (c, 2 * c), 0)
               == (lax.broadcasted_iota(jnp.int32, (c, 2 * c), 1) & (c - 1)))
    lane1 = lax.broadcasted_iota(jnp.int32, (1, pw), 1)
    head_a1 = lane1 < HEAD_A
    lane2 = lax.broadcasted_iota(jnp.int32, (1, 2 * pw), 1) & (pw - 1)
    head_a2 = lane2 < HEAD_A
    rs = lax.broadcasted_iota(jnp.int32, (pw, pw), 0)
    cs = lax.broadcasted_iota(jnp.int32, (pw, pw), 1)
    same_head = (rs < HEAD_A) == (cs < HEAD_A)
    eye = rs == cs
    zeros_cv = jnp.zeros((c, pw), F32)

    def group_maps(slices):
        each = lambda fn, *lists: [fn(*a) for a in zip(*lists)]
        ld = [ld_ref[ix] for ix in slices]
        cw = each(lambda x: _dot_exact_lhs(cum_mat, x, 3), ld)
        w_in = each(jnp.exp, cw)
        w_ex = each(lambda a, b: jnp.exp(a - b), cw, ld)
        w_inv = each(lambda a: jnp.exp(-a), cw)
        w_last = each(lambda a: a[c - 1:c, :], w_in)
        knt = [-(kk_ref[ix] * w) for ix, w in zip(slices, w_ex)]
        kat = [ka_ref[ix] * w for ix, w in zip(slices, w_inv)]
        kt = [k_ref[ix] * w for ix, w in zip(slices, w_inv)]
        rt = [r_ref[ix] * w for ix, w in zip(slices, w_in)]
        v = [v_ref[ix] for ix in slices]
        by_head = lambda z, is_a: jnp.concatenate(
            [jnp.where(is_a, z, 0.0), jnp.where(is_a, 0.0, z)], axis=0)
        lh = each(lambda a, b: jnp.concatenate([a, b], axis=0), knt, rt)
        rh = each(lambda a, b: jnp.concatenate(
            [by_head(a, head_a1), by_head(b, head_a1)], axis=0), kat, kt)
        v_st = each(lambda a: by_head(a, head_a1), v)

        full = each(_dot_nt, lh, rh)
        top = each(lambda a: jnp.where(strict4, a[:c, :], 0.0), full)
        bot = each(lambda a: jnp.where(incl4, a[c:, :], 0.0), full)
        n_cat = each(lambda a: a[:, :2 * c], top)
        akv = each(lambda a, b: _dot(a[:, 2 * c:], b), top, v_st)

        t_cat = each(lambda n: jnp.where(eye_cat, 1.0, 0.0) + n, n_cat)
        span = 2
        while span < c:
            n_cat = each(lambda n: _dot(n, by_head(n, first_blk)), n_cat)
            dt = each(lambda n, t0: _dot(n, by_head(t0, first_blk)), n_cat, t_cat)
            t_cat = each(lambda t0, d: t0 + d, t_cat, dt)
            span *= 2
        x = each(lambda t0, a, b: _dot(t0, by_head(jnp.concatenate([a, b], axis=1), head_a2)),
                 t_cat, knt, akv)

        v_wide = each(lambda a: jnp.concatenate([jnp.zeros_like(a), a], axis=1), v_st)
        qo_all = each(lambda b4, x0, vw: _dot(b4, jnp.concatenate([by_head(x0, head_a2), vw], axis=0)),
                      bot, x, v_wide)
        rhs2 = each(lambda a, b: jnp.concatenate(
            [a, jnp.concatenate([zeros_cv, b], axis=1)], axis=0), x, v)
        lt = each(lambda a, b, w: jnp.concatenate([a * w, b * w], axis=0), kat, kt, w_last)
        mb = each(_dot_tn, lt, rhs2)
        out = []
        for j in range(len(slices)):
            qo = qo_all[j]
            q = rt[j] + qo[:, :pw]
            m = jnp.where(eye, w_last[j], 0.0) + jnp.where(same_head, mb[j][:, :pw], 0.0)
            b = jnp.where(same_head, mb[j][:, pw:], 0.0)
            mq = jnp.concatenate([m, q], axis=0)
            out.append((mq.astype(BF16), b, qo[:, pw:]))
        return out

    def body(it, carry):
        items = [(j, pp) for j in range(group) for pp in range(n_pp)]
        slices = [(pl.ds(pl.multiple_of((it * group + j) * c, c), c), pl.ds(pp * pw, pw))
                  for j, pp in items]
        maps = group_maps(slices)
        s = [s_scr[pp] for pp in range(n_pp)]
        for (j, pp), ix, (mq, b, o0) in zip(items, slices, maps):
            res = _dot(mq, s[pp])
            s[pp] = res[:pw, :] + b
            o_ref[ix] = res[pw:, :] + o0
        for pp in range(n_pp):
            s_scr[pp] = s[pp]
        return carry

    lax.fori_loop(0, n_chunks // group, body, 0)

    @pl.when(t_idx == pl.num_programs(2) - 1)
    def _():
        for pp in range(n_pp):
            s = s_scr[pp]
            st_ref[pp, 0] = s[:HEAD_A, :HEAD_A]
            st_ref[pp, 1] = s[HEAD_A:, HEAD_A:]


def _wkv_scan(r, ld, k, v, kk, ka, s0, chunk, t_blk):
    bn, t, d = r.shape
    pw = 2 * HEAD_A
    n_pairs = d // pw
    n_chunks = t_blk // chunk
    n_pp = SCAN_PAIRS if n_chunks >= SCAN_GROUP else n_pairs
    seq = pl.BlockSpec((None, t_blk, n_pp * pw), lambda b, p, i: (b, i, p))
    st = pl.BlockSpec((None, n_pp, 2, HEAD_A, HEAD_A), lambda b, p, i: (b, p, 0, 0, 0))
    kern = functools.partial(_wkv_scan_kernel, chunk=chunk, n_chunks=n_chunks,
                             group=min(SCAN_GROUP, n_chunks))
    return pl.pallas_call(
        kern,
        grid=(bn, n_pairs // n_pp, t // t_blk),
        in_specs=[seq] * 6 + [st],
        out_specs=[seq, st],
        out_shape=[jax.ShapeDtypeStruct((bn, t, d), F32),
                   jax.ShapeDtypeStruct((bn, n_pairs, 2, HEAD_A, HEAD_A), F32)],
        scratch_shapes=[pltpu.VMEM((n_pp, pw, pw), F32)],
        compiler_params=_params(("parallel", "parallel", "arbitrary")),
        name="wkv_scan",
    )(r, ld, k, v, kk, ka, s0)


def _rwkv_post_kernel(o_ref, bonus_ref, g_ref, x_ref, vec_ref, th_ref, fh_ref, wo_ref, out_ref):
    o = o_ref[...]
    to_head = th_ref[...]
    from_head = fh_ref[...]
    inv_n = 1.0 / HEAD_A
    mean = _head_sum(o, to_head, from_head) * inv_n
    oc = o - mean
    var = _head_sum(oc * oc, to_head, from_head) * inv_n
    on = oc * lax.rsqrt(var + GN_EPS) * vec_ref[0:1, :] + vec_ref[1:2, :]
    y = (on + bonus_ref[...]) * g_ref[...]
    h = _dot(y, wo_ref[...])
    out_ref[...] = _layer_norm(ALPHA * x_ref[...] + h, vec_ref[2:3, :], vec_ref[3:4, :])


def _rwkv_post(o, bonus, g, x, vec8, to_head, from_head, wo, tm):
    n, d = x.shape
    row = pl.BlockSpec((tm, d), lambda i: (i, 0))
    return pl.pallas_call(
        _rwkv_post_kernel,
        grid=(n // tm,),
        in_specs=[row] * 4 + [_const_spec(a.shape) for a in (vec8, to_head, from_head, wo)],
        out_specs=row,
        out_shape=jax.ShapeDtypeStruct((n, d), F32),
        compiler_params=_params(("parallel",)),
        name="rwkv_post",
    )(o, bonus, g, x, vec8, to_head, from_head, wo)


def _ffn_kernel(x_ref, wg_ref, wu_ref, wd_ref, ln_ref, out_ref, acc_ref, xb_ref):
    f = pl.program_id(1)

    @pl.when(f == 0)
    def _():
        acc_ref[...] = jnp.zeros_like(acc_ref)
        xb_ref[...] = x_ref[...].astype(BF16)

    xb = xb_ref[...]
    gate = jnp.dot(xb, wg_ref[...].astype(BF16), preferred_element_type=F32)
    up = jnp.dot(xb, wu_ref[...].astype(BF16), preferred_element_type=F32)
    h = gate * _sigmoid(gate) * up
    acc_ref[...] += jnp.dot(h.astype(BF16), wd_ref[...].astype(BF16), preferred_element_type=F32)

    @pl.when(f == pl.num_programs(1) - 1)
    def _():
        out_ref[...] = _layer_norm(ALPHA * x_ref[...] + acc_ref[...], ln_ref[0:1, :], ln_ref[1:2, :])


def _ffn(x, w_gu, w_down, ln8, tm, tf):
    n, d = x.shape
    d_ff = w_gu.shape[1] // 2
    nf = d_ff // tf
    return pl.pallas_call(
        _ffn_kernel,
        grid=(n // tm, nf),
        in_specs=[
            pl.BlockSpec((tm, d), lambda i, f: (i, 0)),
            pl.BlockSpec((d, tf), lambda i, f: (0, f)),
            pl.BlockSpec((d, tf), lambda i, f: (0, nf + f)),
            pl.BlockSpec((tf, d), lambda i, f: (f, 0)),
            _const_spec(ln8.shape),
        ],
        out_specs=pl.BlockSpec((tm, d), lambda i, f: (i, 0)),
        out_shape=jax.ShapeDtypeStruct((n, d), F32),
        scratch_shapes=[pltpu.VMEM((tm, d), F32), pltpu.VMEM((tm, d), BF16)],
        compiler_params=_params(("parallel", "arbitrary")),
        name="dense_ffn",
    )(x, w_gu, w_gu, w_down, ln8)


ROUTE_I1, ROUTE_I2, ROUTE_G1, ROUTE_G2, ROUTE_Q1, ROUTE_Q2 = range(6)
META_CARRY, META_COUNT = 0, 1


def _proj_ln_router_kernel(y_ref, x_ref, w_ref, ln_ref, rw_ref, out_ref, route_ref, meta_ref, count_ref,
                           carry_ref):
    h = _dot(y_ref[...], w_ref[...])
    x3 = _layer_norm(ALPHA * x_ref[...] + h, ln_ref[0:1, :], ln_ref[1:2, :])
    out_ref[...] = x3
    _route_rows(x3, rw_ref, route_ref, meta_ref, count_ref, carry_ref)


def _route_rows(x, w_ref, route_ref, meta_ref, count_ref, carry_ref):
    i = pl.program_id(0)

    @pl.when(i == 0)
    def _():
        carry_ref[...] = jnp.zeros_like(carry_ref)

    x_hi, x_lo = _split(x, 2)
    w_hi, w_lo = _split(w_ref[...], 2)
    logits = (jnp.dot(x_hi, w_hi, preferred_element_type=F32)
              + jnp.dot(x_lo, w_hi, preferred_element_type=F32)
              + jnp.dot(x_hi, w_lo, preferred_element_type=F32))
    tm = logits.shape[0]
    lane = lax.broadcasted_iota(jnp.int32, logits.shape, 1).astype(F32)
    neg = -jnp.inf
    logits = jnp.where(lane < N_EXPERTS, logits, neg)
    m1 = jnp.max(logits, axis=-1, keepdims=True)
    i1 = jnp.min(jnp.where(logits == m1, lane, float(LANES)), axis=-1, keepdims=True)
    rest = jnp.where(lane == i1, neg, logits)
    m2 = jnp.max(rest, axis=-1, keepdims=True)
    i2 = jnp.min(jnp.where(rest == m2, lane, float(LANES)), axis=-1, keepdims=True)
    e2 = jnp.exp(m2 - m1)
    den = 1.0 + e2

    sel1 = lane == i1
    sel2 = lane == i2
    onehot = jnp.where(sel1, 1.0, 0.0) + jnp.where(sel2, 1.0, 0.0)
    ri = lax.broadcasted_iota(jnp.int32, (tm, tm), 0)
    ci = lax.broadcasted_iota(jnp.int32, (tm, tm), 1)
    earlier = jnp.where(ri > ci, 1.0, 0.0).astype(BF16)
    in_tile = jnp.dot(earlier, onehot.astype(BF16), preferred_element_type=F32)
    q1 = jnp.sum(jnp.where(sel1, in_tile, 0.0), axis=-1, keepdims=True)
    q2 = jnp.sum(jnp.where(sel2, in_tile, 0.0), axis=-1, keepdims=True)
    tile_count = jnp.sum(onehot, axis=0, keepdims=True)

    route = jnp.zeros_like(logits)
    for col, val in ((ROUTE_I1, i1), (ROUTE_I2, i2), (ROUTE_G1, 1.0 / den), (ROUTE_G2, e2 / den),
                     (ROUTE_Q1, q1), (ROUTE_Q2, q2)):
        route = jnp.where(lane == float(col), val, route)
    route_ref[...] = route
    meta_row = lax.broadcasted_iota(jnp.int32, meta_ref.shape, 0)
    meta_ref[...] = jnp.where(meta_row == META_CARRY, carry_ref[0:1, :],
                              jnp.where(meta_row == META_COUNT, tile_count, 0.0))
    slab_rows = jnp.floor((tile_count + (SLAB_UNIT - 1)) * (1.0 / SLAB_UNIT)) * SLAB_UNIT
    carry_ref[0:1, :] = carry_ref[0:1, :] + slab_rows
    count_ref[...] = carry_ref[...]


def _proj_ln_router(y, x, w, ln8, router_pad, tm):
    n, d = x.shape
    row = pl.BlockSpec((tm, d), lambda i: (i, 0))
    return pl.pallas_call(
        _proj_ln_router_kernel,
        grid=(n // tm,),
        in_specs=[pl.BlockSpec((tm, y.shape[1]), lambda i: (i, 0)), row,
                  _const_spec(w.shape), _const_spec(ln8.shape), _const_spec(router_pad.shape)],
        out_specs=[row,
                   pl.BlockSpec((tm, LANES), lambda i: (i, 0)),
                   pl.BlockSpec((None, SUBLANES, LANES), lambda i: (i, 0, 0)),
                   _const_spec((SUBLANES, LANES))],
        out_shape=[jax.ShapeDtypeStruct((n, d), F32),
                   jax.ShapeDtypeStruct((n, LANES), F32),
                   jax.ShapeDtypeStruct((n // tm, SUBLANES, LANES), F32),
                   jax.ShapeDtypeStruct((SUBLANES, LANES), F32)],
        scratch_shapes=[pltpu.VMEM((SUBLANES, LANES), F32)],
        compiler_params=_params(("arbitrary",)),
        name="proj_ln_router",
    )(y, x, w, ln8, router_pad)


def _slab_plan(meta_ref):
    plan = []
    off = 0
    for e in range(N_EXPERTS):
        units = (meta_ref[0, e] + (SLAB_UNIT - 1)) // SLAB_UNIT
        plan.append((units, meta_ref[0, N_EXPERTS + e], off))
        off = off + units * SLAB_UNIT
    return plan


def _slab_buffer_rows(tm):
    worst = 2 * tm + N_EXPERTS * (SLAB_UNIT - 1)
    return ((worst + LANES - 1) // LANES) * LANES


def _slab_rows(expert, rank, plan):
    off = jnp.zeros_like(expert)
    for e, (_, _, e_off) in enumerate(plan):
        off = jnp.where(expert == e, e_off, off)
    return off + rank


def _slab_copies(plan, make_copy):
    total = 0
    for units, first_row, off in plan:
        def start(u, carry, first_row=first_row, off=off):
            make_copy(pl.multiple_of(off + u * SLAB_UNIT, SLAB_UNIT),
                      pl.multiple_of(first_row + u * SLAB_UNIT, SLAB_UNIT)).start()
            return carry
        lax.fori_loop(0, units, start, 0)
        total = total + units
    return total


def _dispatch_kernel(meta_ref, route_ref, x_ref, xs_in_ref, xs_ref, cbuf, sem):
    del xs_in_ref
    tm = x_ref.shape[0]
    plan = _slab_plan(meta_ref)

    sel = (lax.broadcasted_iota(jnp.int32, (SUBLANES, LANES), 0)
           == lax.broadcasted_iota(jnp.int32, (SUBLANES, LANES), 1)).astype(BF16)
    route_t = None
    for piece in _split(route_ref[...], 3):
        t = lax.dot_general(sel, piece, (((1,), (1,)), ((), ())), preferred_element_type=F32)
        route_t = t if route_t is None else route_t + t
    route_t = route_t.astype(jnp.int32)
    row1 = _slab_rows(route_t[ROUTE_I1:ROUTE_I1 + 1, :], route_t[ROUTE_Q1:ROUTE_Q1 + 1, :], plan)
    row2 = _slab_rows(route_t[ROUTE_I2:ROUTE_I2 + 1, :], route_t[ROUTE_Q2:ROUTE_Q2 + 1, :], plan)

    r_iota = lax.broadcasted_iota(jnp.int32, (cbuf.shape[0], tm), 0)
    perm = jnp.where((r_iota == row1) | (r_iota == row2), 1.0, 0.0).astype(BF16)
    cbuf[...] = jnp.dot(perm, x_ref[...].astype(BF16), preferred_element_type=F32)

    def make_copy(buf_row, sorted_row):
        return pltpu.make_async_copy(cbuf.at[pl.ds(buf_row, SLAB_UNIT), :],
                                     xs_ref.at[pl.ds(sorted_row, SLAB_UNIT), :], sem)

    n_started = _slab_copies(plan, make_copy)

    def wait(u, carry):
        make_copy(0, 0).wait()
        return carry

    lax.fori_loop(0, n_started, wait, 0)


def _dispatch(meta, route, x, xs, tm):
    n, d = x.shape
    return pl.pallas_call(
        _dispatch_kernel,
        grid=(n // tm,),
        in_specs=[pl.BlockSpec((None, 1, 2 * N_EXPERTS), lambda i: (i, 0, 0), memory_space=pltpu.SMEM),
                  pl.BlockSpec((tm, LANES), lambda i: (i, 0)),
                  pl.BlockSpec((tm, d), lambda i: (i, 0)),
                  pl.BlockSpec(memory_space=pl.ANY)],
        out_specs=pl.BlockSpec(memory_space=pl.ANY),
        out_shape=jax.ShapeDtypeStruct(xs.shape, xs.dtype),
        scratch_shapes=[pltpu.VMEM((_slab_buffer_rows(tm), d), F32), pltpu.SemaphoreType.DMA(())],
        input_output_aliases={3: 0},
        compiler_params=_params(("arbitrary",)),
        name="moe_dispatch",
    )(meta, route, x, xs)


def _experts_kernel(te_ref, rows_ref, x_ref, wg_ref, wu_ref, wd_ref, out_ref, acc_ref, xb_ref):
    del te_ref
    f = pl.program_id(1)
    rows = rows_ref[pl.program_id(0)]
    tm = x_ref.shape[0]

    @pl.when(f == 0)
    def _():
        acc_ref[...] = jnp.zeros_like(acc_ref)

    def swiglu_rows(n_rows):
        sl = pl.ds(0, n_rows)

        @pl.when(f == 0)
        def _():
            xb_ref[sl, :] = x_ref[sl, :].astype(BF16)

        xb = xb_ref[sl, :]
        gate = jnp.dot(xb, wg_ref[...].astype(BF16), preferred_element_type=F32)
        up = jnp.dot(xb, wu_ref[...].astype(BF16), preferred_element_type=F32)
        h = gate * _sigmoid(gate) * up
        acc_ref[sl, :] += jnp.dot(h.astype(BF16), wd_ref[...].astype(BF16), preferred_element_type=F32)

    part = tm // MOE_TILE_PARTS
    for k in range(1, MOE_TILE_PARTS + 1):
        @pl.when(jnp.logical_and(rows > (k - 1) * part, rows <= k * part))
        def _(k=k):
            swiglu_rows(k * part)

    @pl.when(f == pl.num_programs(1) - 1)
    def _():
        out_ref[...] = acc_ref[...]


def _experts(tile_expert, tile_rows, xs, w_gu, w_down, tm, tf):
    s_total, d = xs.shape
    d_ff = w_gu.shape[2] // 2
    nf = d_ff // tf

    def f_eff(i, f, rows):
        return jnp.where(rows[i] > 0, f, nf - 1)

    grid_spec = pltpu.PrefetchScalarGridSpec(
        num_scalar_prefetch=2,
        grid=(s_total // tm, nf),
        in_specs=[
            pl.BlockSpec((tm, d), lambda i, f, te, rows: (i, 0)),
            pl.BlockSpec((None, d, tf), lambda i, f, te, rows: (te[i], 0, f_eff(i, f, rows))),
            pl.BlockSpec((None, d, tf), lambda i, f, te, rows: (te[i], 0, nf + f_eff(i, f, rows))),
            pl.BlockSpec((None, tf, d), lambda i, f, te, rows: (te[i], f_eff(i, f, rows), 0)),
        ],
        out_specs=pl.BlockSpec((tm, d), lambda i, f, te, rows: (i, 0)),
        scratch_shapes=[pltpu.VMEM((tm, d), F32), pltpu.VMEM((tm, d), BF16)],
    )
    return pl.pallas_call(
        _experts_kernel,
        grid_spec=grid_spec,
        out_shape=jax.ShapeDtypeStruct((s_total, d), F32),
        compiler_params=_params(("arbitrary", "arbitrary")),
        name="moe_experts",
    )(tile_expert, tile_rows, xs, w_gu, w_gu, w_down)


def _combine_kernel(meta_ref, meta_next_ref, route_ref, x_ref, ys_ref, ln_ref, out_ref, cbuf, sem):
    tm = x_ref.shape[0]
    i = pl.program_id(0)
    slot = i & 1
    plan = _slab_plan(meta_ref)

    def gather(tile_plan, dst_slot):
        def make_copy(buf_row, sorted_row):
            return pltpu.make_async_copy(ys_ref.at[pl.ds(sorted_row, SLAB_UNIT), :],
                                         cbuf.at[dst_slot, pl.ds(buf_row, SLAB_UNIT), :],
                                         sem.at[dst_slot])
        return make_copy, lambda: _slab_copies(tile_plan, make_copy)

    @pl.when(i == 0)
    def _():
        cbuf[...] = jnp.zeros_like(cbuf)
        gather(plan, 0)[1]()

    @pl.when(i + 1 < pl.num_programs(0))
    def _():
        gather(_slab_plan(meta_next_ref), 1 - slot)[1]()

    make_copy = gather(plan, slot)[0]
    n_mine = sum(units for units, _, _ in plan)

    def wait(u, carry):
        make_copy(0, 0).wait()
        return carry

    lax.fori_loop(0, n_mine, wait, 0)

    route = route_ref[...]
    col = lambda k: route[:, k:k + 1]
    row1 = _slab_rows(col(ROUTE_I1).astype(jnp.int32), col(ROUTE_Q1).astype(jnp.int32), plan)
    row2 = _slab_rows(col(ROUTE_I2).astype(jnp.int32), col(ROUTE_Q2).astype(jnp.int32), plan)
    c_iota = lax.broadcasted_iota(jnp.int32, (tm, cbuf.shape[1]), 1)
    gate = (jnp.where(c_iota == row1, col(ROUTE_G1), 0.0)
            + jnp.where(c_iota == row2, col(ROUTE_G2), 0.0))
    g_hi, g_lo = _split(gate, 2)
    c_hi, c_lo = _split(cbuf[slot], 2)
    y = (jnp.dot(g_hi, c_hi, preferred_element_type=F32)
         + jnp.dot(g_hi, c_lo, preferred_element_type=F32)
         + jnp.dot(g_lo, c_hi, preferred_element_type=F32))
    out_ref[...] = _layer_norm(ALPHA * x_ref[...] + y, ln_ref[0:1, :], ln_ref[1:2, :])


def _combine(meta, route, x, ys, ln8, tm):
    n, d = x.shape
    last = n // tm - 1
    meta_spec = lambda shift: pl.BlockSpec((None, 1, 2 * N_EXPERTS),
                                           lambda i: (jnp.minimum(i + shift, last), 0, 0),
                                           memory_space=pltpu.SMEM)
    return pl.pallas_call(
        _combine_kernel,
        grid=(n // tm,),
        in_specs=[meta_spec(0), meta_spec(1),
                  pl.BlockSpec((tm, LANES), lambda i: (i, 0)),
                  pl.BlockSpec((tm, d), lambda i: (i, 0)),
                  pl.BlockSpec(memory_space=pl.ANY),
                  _const_spec(ln8.shape)],
        out_specs=pl.BlockSpec((tm, d), lambda i: (i, 0)),
        out_shape=jax.ShapeDtypeStruct((n, d), F32),
        scratch_shapes=[pltpu.VMEM((2, _slab_buffer_rows(tm), d), F32), pltpu.SemaphoreType.DMA((2,))],
        compiler_params=_params(("arbitrary",)),
        name="moe_combine",
    )(meta, meta, route, x, ys, ln8)


def _rope(y, cos_t, sin_next, sin_prev):
    n = y.shape[1]
    reps = n // LANES
    tile = lambda t: jnp.concatenate([t] * reps, axis=1) if reps > 1 else t
    half = ROPE_DIM // 2
    return (y * tile(cos_t)
            + pltpu.roll(y, n - half, 1) * tile(sin_next)
            + pltpu.roll(y, half, 1) * tile(sin_prev))


def _qkv_kernel(x_ref, wq_ref, wkv_ref, cos_ref, sn_ref, sp_ref, q_out, k_out, v_out):
    xb = x_ref[...].astype(BF16)
    tables = (cos_ref[...], sn_ref[...], sp_ref[...])
    q_out[...] = _rope(jnp.dot(xb, wq_ref[...], preferred_element_type=F32), *tables)
    kv = jnp.dot(xb, wkv_ref[...], preferred_element_type=F32)
    nk = k_out.shape[1]
    k_out[...] = _rope(kv[:, :nk], *tables)
    v_out[...] = kv[:, nk:]


def _qkv_proj(x, wq, wkv, tables, tm):
    n, d = x.shape
    kvw = wkv.shape[1] // 2
    t_tiles = tables[0].shape[0] // tm
    tab = pl.BlockSpec((tm, LANES), lambda i: (i % t_tiles, 0))
    widths = (wq.shape[1], kvw, kvw)
    return pl.pallas_call(
        _qkv_kernel,
        grid=(n // tm,),
        in_specs=[pl.BlockSpec((tm, d), lambda i: (i, 0)), _const_spec(wq.shape), _const_spec(wkv.shape),
                  tab, tab, tab],
        out_specs=[pl.BlockSpec((tm, ow), lambda i: (i, 0)) for ow in widths],
        out_shape=[jax.ShapeDtypeStruct((n, ow), F32) for ow in widths],
        compiler_params=_params(("parallel",)),
        name="qkv_proj",
    )(x, wq, wkv, *tables)


def _attn_kernel(sink_ref, q_ref, kp_ref, kc_ref, vp_ref, vc_ref, o_ref, *, banded):
    tq = q_ref.shape[0]
    n_prev = kp_ref.shape[0]
    tk = n_prev + kc_ref.shape[0]
    q = q_ref[...] * ATTN_SCALE
    kband = jnp.concatenate([kp_ref[...], kc_ref[...]], axis=0)
    vband = jnp.concatenate([vp_ref[...], vc_ref[...]], axis=0)
    n_heads = q.shape[1] // HEAD_B
    group = n_heads // KV_HEADS
    qc = CHUNK if banded else tq
    kc = WINDOW + CHUNK if banded else tk
    units = [(kh, ci) for kh in range(KV_HEADS) for ci in range(tq // qc)]
    if banded:
        kj = lax.broadcasted_iota(jnp.int32, (kc, group * qc), 0)
        band_start = pl.program_id(1) * tq - n_prev

    qlane = lax.broadcasted_iota(jnp.int32, (1, group * qc), 1)

    def scores(kh, ci):
        qs = jnp.concatenate([q[ci * qc:(ci + 1) * qc, (kh * group + j) * HEAD_B:(kh * group + j + 1) * HEAD_B]
                              for j in range(group)], axis=0)
        return _dot_nt(kband[ci * qc:ci * qc + kc, kh * HEAD_B:(kh + 1) * HEAD_B], qs)

    def softmax(kh, ci, s):
        sk = jnp.full((1, group * qc), sink_ref[kh * group], F32)
        for j in range(1, group):
            sk = jnp.where(qlane >= j * qc, sink_ref[kh * group + j], sk)
        if banded:
            s = jnp.where(band_start + ci * qc + kj >= 0, s, -jnp.inf)
        m = jnp.maximum(jnp.max(s, axis=0, keepdims=True), sk)
        p = jnp.exp(s - m)
        return p * (1.0 / (jnp.sum(p, axis=0, keepdims=True) + jnp.exp(sk - m)))

    def weighted(kh, ci, p):
        return _dot_tn(p, vband[ci * qc:ci * qc + kc, kh * HEAD_B:(kh + 1) * HEAD_B])

    chunks = range(tq // qc)
    outs = {}
    s_next = [scores(0, ci) for ci in chunks]
    for kh in range(KV_HEADS):
        s_cur = s_next
        if kh + 1 < KV_HEADS:
            s_next = [scores(kh + 1, ci) for ci in chunks]
        probs = [softmax(kh, ci, s_cur[ci]) for ci in chunks]
        for ci in chunks:
            o = weighted(kh, ci, probs[ci])
            for j in range(group):
                outs[(kh * group + j, ci)] = o[j * qc:(j + 1) * qc, :]
    o_ref[...] = jnp.concatenate(
        [jnp.concatenate([outs[(h, ci)] for ci in range(tq // qc)], axis=0) for h in range(n_heads)],
        axis=1)


def _attention(q, k_prev_src, k_cur_src, v_prev_src, v_cur_src, sinks, tq, banded):
    bn, t, d = q.shape
    kw = k_cur_src.shape[2]
    if banded:
        ratio = tq // WINDOW
        prev_map = lambda b, i: (b, jnp.maximum(i * ratio - 1, 0), 0)
    else:
        prev_map = lambda b, i: (b, 0, 0)
    prev = pl.BlockSpec((None, WINDOW, kw), prev_map)
    cur = pl.BlockSpec((None, tq, kw), lambda b, i: (b, i, 0))
    kern = functools.partial(_attn_kernel, banded=banded)
    return pl.pallas_call(
        kern,
        grid=(bn, t // tq),
        in_specs=[pl.BlockSpec(memory_space=pltpu.SMEM),
                  pl.BlockSpec((None, tq, d), lambda b, i: (b, i, 0)), prev, cur, prev, cur],
        out_specs=pl.BlockSpec((None, tq, d), lambda b, i: (b, i, 0)),
        out_shape=jax.ShapeDtypeStruct((bn, t, d), F32),
        compiler_params=_params(("parallel", "parallel")),
        name="swa_attn",
    )(sinks, q, k_prev_src, k_cur_src, v_prev_src, v_cur_src)


def _moe_layer(groups, P):
    xs_rows = [g[0] for g in groups]
    routed = [g[1:] for g in groups]
    d = xs_rows[0].shape[1]
    tm_e = MOE_TILE
    counts = [c[0, :N_EXPERTS].astype(jnp.int32) for _, _, c in routed]
    total = sum(counts)
    padded = ((total + tm_e - 1) // tm_e) * tm_e
    ends = jnp.cumsum(padded)
    starts = ends - padded
    n_slabs = sum(-(-x.shape[0] // MOE_ROW_TILE) for x in xs_rows) * N_EXPERTS
    n_assign = 2 * sum(x.shape[0] for x in xs_rows) + n_slabs * (SLAB_UNIT - 1)
    n_tiles = (n_assign + N_EXPERTS * (tm_e - 1)) // tm_e
    n_used = (ends[-1] // tm_e).astype(jnp.int32)
    tile_expert = jnp.sum((jnp.arange(n_tiles) * tm_e)[:, None] >= ends[None, :], axis=1).astype(jnp.int32)
    tile_expert = jnp.minimum(tile_expert, N_EXPERTS - 1)
    tile_expert = jnp.where(jnp.arange(n_tiles) < n_used, tile_expert,
                            tile_expert[jnp.maximum(n_used - 1, 0)])
    tile_rows = jnp.clip((starts + total)[tile_expert] - jnp.arange(n_tiles) * tm_e, 0, tm_e)
    tile_rows = jnp.where(jnp.arange(n_tiles) < n_used, tile_rows, 0).astype(jnp.int32)

    metas = []
    base = starts
    for (_, meta, _), cnt in zip(routed, counts):
        carry = meta[:, META_CARRY, :N_EXPERTS].astype(jnp.int32)
        sent = meta[:, META_COUNT, :N_EXPERTS].astype(jnp.int32)
        metas.append(jnp.concatenate([sent, base[None, :] + carry], axis=1)[:, None, :])
        base = base + cnt

    xs = jnp.zeros((n_tiles * tm_e, d), F32)
    for m, (route, _, _), x in zip(metas, routed, xs_rows):
        xs = _dispatch(m, route, x, xs, min(MOE_ROW_TILE, x.shape[0]))
    ys = _experts(tile_expert, tile_rows, xs, P['moe_w_gu'][0], P['moe_w_down'][0], tm_e, 512)
    ln11 = _pad_rows([P['ln_g'][1, 1], P['ln_b'][1, 1]], d)
    return [_combine(m, route, x, ys, ln11, min(MOE_ROW_TILE, x.shape[0]))
            for m, (route, _, _), x in zip(metas, routed, xs_rows)]


def _pad_rows(rows, d):
    a = jnp.stack(rows).astype(F32)
    return jnp.concatenate([a, jnp.zeros((SUBLANES - a.shape[0], d), F32)], axis=0)


def _rope_tables(pos, reps):
    inv_freq = ROPE_THETA ** (-jnp.arange(0, ROPE_DIM, 2, dtype=jnp.float32) / ROPE_DIM)
    ang = pos.astype(jnp.float32)[:, None] * inv_freq[None, :]
    cos = jnp.cos(ang)
    sin = jnp.sin(ang)
    t = pos.shape[0]
    half = ROPE_DIM // 2
    rest = HEAD_B - ROPE_DIM
    z_half = jnp.zeros((t, half), F32)
    z_rest = jnp.zeros((t, rest), F32)
    cos_h = jnp.concatenate([cos, cos, jnp.ones((t, rest), F32)], axis=1)
    sn_h = jnp.concatenate([-sin, z_half, z_rest], axis=1)
    sp_h = jnp.concatenate([z_half, sin, z_rest], axis=1)
    per_tile = LANES // HEAD_B
    return tuple(jnp.tile(a, (reps, per_tile)) for a in (cos_h, sn_h, sp_h))


def _trunk(x, shift_in, wkv_in, k_cache, v_cache, pos0, P):
    bn, t, d = x.shape
    n = bn * t
    h_a = d // HEAD_A
    pw = 2 * HEAD_A
    n_pairs = d // pw
    xf = x.reshape(n, d)

    tm_pre = min(256, t)
    tm_row = min(256, n)
    tm_ffn = min(1024, n)
    chunk = min(CHUNK, t)
    t_blk = min(512, t)

    tiles = jnp.arange(n // tm_pre) * tm_pre
    prev_rows = xf[jnp.maximum(tiles - 1, 0)]
    start_rows = shift_in[0][tiles // t]
    bnd = jnp.where(((tiles % t) == 0)[:, None], start_rows, prev_rows)
    bnd = jnp.broadcast_to(bnd[:, None, :], (n // tm_pre, SUBLANES, d))

    mu8 = jnp.concatenate([P['a_mu'][0], jnp.zeros((2, d), F32)], axis=0)
    vec_pre = _pad_rows([P['a_w0'][0], P['a_a0'][0], P['a_k_k'][0], P['a_k_a'][0],
                         P['a_r_k'][0].reshape(d)], d)
    lane_head = jnp.arange(d) // HEAD_A
    to_head = (lane_head[:, None] == jnp.arange(LANES)[None, :]).astype(BF16)
    from_head = to_head.T
    bf = lambda a: a.astype(BF16)
    w_rkv = P['a_w_rkv'][0]
    r, ld, k_h, v, kk, ka, g, bonus = _rwkv_pre(
        xf, bnd, mu8, vec_pre, bf(w_rkv[0]), bf(w_rkv[1]), bf(w_rkv[2]),
        bf(P['a_w1'][0]), bf(P['a_w2'][0]), bf(P['a_a1'][0]), bf(P['a_a2'][0]),
        bf(P['a_g1'][0]), bf(P['a_g2'][0]), to_head, from_head, tm_pre)

    s0 = jnp.swapaxes(wkv_in[0].astype(F32), -1, -2).reshape(bn, n_pairs, 2, HEAD_A, HEAD_A)
    seq3 = lambda a: a.reshape(bn, t, d)
    o, s_fin = _wkv_scan(seq3(r), seq3(ld), seq3(k_h), seq3(v), seq3(kk), seq3(ka), s0, chunk, t_blk)
    wkv_out = jnp.swapaxes(s_fin, -1, -2).reshape(bn, h_a, HEAD_A, HEAD_A)
    shift_out = x[:, -1]

    vec_post = _pad_rows([P['a_lnx_g'][0], P['a_lnx_b'][0], P['ln_g'][0, 0], P['ln_b'][0, 0]], d)
    x1 = _rwkv_post(o.reshape(n, d), bonus, g, xf, vec_post, to_head, from_head, bf(P['a_w_o'][0]), tm_row)

    ln01 = _pad_rows([P['ln_g'][0, 1], P['ln_b'][0, 1]], d)
    x2 = _ffn(x1, P['ffn_w_gu'][0], P['ffn_w_down'][0], ln01, tm_ffn, 512)

    pos = pos0 + jnp.arange(t, dtype=jnp.int32)
    tm_qkv = min(512, n)
    tables = _rope_tables(pos, max(tm_qkv // t, 1))
    kvw = KV_HEADS * HEAD_B
    q, k_new, v_new = _qkv_proj(x2, bf(P['b_w_q'][0]), bf(P['kv_w']), tables, tm_qkv)
    k_new = k_new.reshape(bn, t, kvw)
    v_new = v_new.reshape(bn, t, kvw)
    q = q.reshape(bn, t, d)
    sinks = P['b_sinks'][0].astype(F32)
    if k_cache is None:
        att = _attention(q, k_new, k_new, v_new, v_new, sinks, min(256, t), banded=True)
        k_out = k_new[:, -WINDOW:]
        v_out = v_new[:, -WINDOW:]
    else:
        kc = k_cache.astype(F32).reshape(bn, WINDOW, kvw)
        vc = v_cache.astype(F32).reshape(bn, WINDOW, kvw)
        att = _attention(q, kc, k_new, vc, v_new, sinks, t, banded=False)
        k_out = jnp.concatenate([kc, k_new], axis=1)[:, -WINDOW:]
        v_out = jnp.concatenate([vc, v_new], axis=1)[:, -WINDOW:]
    ln10 = _pad_rows([P['ln_g'][1, 0], P['ln_b'][1, 0]], d)
    router_pad = jnp.concatenate([P['moe_router'][0], jnp.zeros((d, LANES - N_EXPERTS), F32)], axis=1)
    routed = _proj_ln_router(att.reshape(n, d), x2, bf(P['b_w_o'][0]), ln10, router_pad,
                             min(MOE_ROW_TILE, n))

    return (routed, shift_out[None], wkv_out[None],
            k_out.reshape(bn, WINDOW, KV_HEADS, HEAD_B), v_out.reshape(bn, WINDOW, KV_HEADS, HEAD_B))


def kernel(x_prompt, x_sample, cache_shift_a, state_wkv_a, cache_k_b, cache_v_b, a_mu, a_w_rkv, a_w0, a_w1, a_w2, a_a0, a_a1, a_a2, a_g1, a_g2, a_k_k, a_k_a, a_r_k, a_lnx_g, a_lnx_b, a_w_o, kv_w, b_w_q, b_sinks, b_w_o, ln_g, ln_b, ffn_w_gu, ffn_w_down, moe_router, moe_w_gu, moe_w_down):
    P = {
        'a_mu': a_mu, 'a_w_rkv': a_w_rkv, 'a_w0': a_w0, 'a_w1': a_w1, 'a_w2': a_w2,
        'a_a0': a_a0, 'a_a1': a_a1, 'a_a2': a_a2, 'a_g1': a_g1, 'a_g2': a_g2,
        'a_k_k': a_k_k, 'a_k_a': a_k_a, 'a_r_k': a_r_k, 'a_lnx_g': a_lnx_g,
        'a_lnx_b': a_lnx_b, 'a_w_o': a_w_o, 'kv_w': kv_w, 'b_w_q': b_w_q,
        'b_sinks': b_sinks, 'b_w_o': b_w_o, 'ln_g': ln_g, 'ln_b': ln_b,
        'ffn_w_gu': ffn_w_gu, 'ffn_w_down': ffn_w_down, 'moe_router': moe_router,
        'moe_w_gu': moe_w_gu, 'moe_w_down': moe_w_down,
    }
    bp = x_prompt.shape[0]
    d = x_prompt.shape[2]
    h_a = d // HEAD_A
    zero_shift = jnp.zeros((1, bp, d), x_prompt.dtype)
    zero_wkv = jnp.zeros((1, bp, h_a, HEAD_A, HEAD_A), F32)
    x3_p, p_shift, p_wkv, p_k, p_v = _trunk(x_prompt, zero_shift, zero_wkv, None, None, 0, P)
    x3_s, s_shift, s_wkv, s_k, s_v = _trunk(x_sample, cache_shift_a, state_wkv_a,
                                            cache_k_b, cache_v_b, PAST_LEN, P)
    y_p, y_s = _moe_layer([x3_p, x3_s], P)
    return (y_p.reshape(x_prompt.shape), y_s.reshape(x_sample.shape),
            p_shift, p_wkv, p_k, p_v, s_shift, s_wkv, s_k, s_v)
```

```python
import functools

import jax
import jax.numpy as jnp
from jax import lax
from jax.experimental import pallas as pl
from jax.experimental.pallas import tpu as pltpu

F32 = jnp.float32
BF16 = jnp.bfloat16

DEPTH = 2
HEAD_A = 64
HEAD_B = 64
KV_HEADS = 4
CHUNK = 64
WINDOW = 128
PAST_LEN = 4096
ROPE_DIM = HEAD_B // 4
ROPE_THETA = 500000.0
ATTN_SCALE = HEAD_B ** -0.5
N_EXPERTS = 8
GN_EPS = 64e-5
LN_EPS = 1e-5
ALPHA = (2.0 * DEPTH) ** 0.25

LANES = 128
SUBLANES = 8
VMEM_LIMIT_BYTES = 56 * 1024 * 1024
POST_ROW_BLOCKS = 2
FFN_COLS = 512
SCAN_GROUP = 8
SCAN_PAIRS = 2
MOE_TILE = 1024
MOE_ROW_TILE = 512
MOE_TILE_PARTS = 4
SLAB_UNIT = SUBLANES


def _dot(a, b):
    return jnp.dot(a.astype(BF16), b.astype(BF16), preferred_element_type=F32)


def _dot_nt(a, b):
    return lax.dot_general(a.astype(BF16), b.astype(BF16), (((1,), (1,)), ((), ())),
                           preferred_element_type=F32)


def _dot_tn(a, b):
    return lax.dot_general(a.astype(BF16), b.astype(BF16), (((0,), (0,)), ((), ())),
                           preferred_element_type=F32)


def _split(x, n):
    parts = []
    rem = x
    for i in range(n):
        p = rem.astype(BF16)
        parts.append(p)
        if i + 1 < n:
            rem = rem - p.astype(F32)
    return parts


def _dot_exact_rhs(a, b_bf16, n):
    acc = None
    for p in _split(a, n):
        t = jnp.dot(p, b_bf16, preferred_element_type=F32)
        acc = t if acc is None else acc + t
    return acc


def _dot_exact_lhs(a_bf16, b, n):
    acc = None
    for p in _split(b, n):
        t = jnp.dot(a_bf16, p, preferred_element_type=F32)
        acc = t if acc is None else acc + t
    return acc


def _head_sum(x, to_head, from_head):
    return _dot_exact_rhs(_dot_exact_rhs(x, to_head, 2), from_head, 3)


def _sigmoid(z):
    return 1.0 / (1.0 + jnp.exp(-z))


def _layer_norm(z, g, b):
    mu = jnp.mean(z, axis=-1, keepdims=True)
    zc = z - mu
    var = jnp.mean(zc * zc, axis=-1, keepdims=True)
    return zc * lax.rsqrt(var + LN_EPS) * g + b


def _const_spec(shape):
    nd = len(shape)
    return pl.BlockSpec(shape, lambda *_: (0,) * nd)


def _params(sem):
    return pltpu.CompilerParams(dimension_semantics=sem, vmem_limit_bytes=VMEM_LIMIT_BYTES)


def _rwkv_pre_kernel(x_ref, bnd_ref, mu_ref, vec_ref, wr_ref, wk_ref, wv_ref, w1_ref, w2_ref,
                     a1_ref, a2_ref, g1_ref, g2_ref, th_ref, fh_ref,
                     r_out, ld_out, k_out, v_out, kk_out, ka_out, g_out, bonus_out):
    x = x_ref[...]
    rows = lax.broadcasted_iota(jnp.int32, x.shape, 0)
    xp = jnp.where(rows == 0, bnd_ref[0, 0:1, :], pltpu.roll(x, 1, 0))
    dx = xp - x

    def mix(s):
        return x + dx * mu_ref[s:s + 1, :]

    w0 = vec_ref[0:1, :]
    a0 = vec_ref[1:2, :]
    k_k = vec_ref[2:3, :]
    k_a = vec_ref[3:4, :]
    r_k = vec_ref[4:5, :]
    to_head = th_ref[...]
    from_head = fh_ref[...]

    lw = _dot(mix(3), w1_ref[...])
    la = _dot(mix(4), a1_ref[...])
    lg = _dot(mix(5), g1_ref[...])
    r = _dot(mix(0), wr_ref[...])
    k = _dot(mix(1), wk_ref[...])
    zw = -(w0 + _dot(jnp.tanh(lw), w2_ref[...]))
    a = _sigmoid(a0 + _dot(la, a2_ref[...]))
    g = _dot(_sigmoid(lg), g2_ref[...])
    v = _dot(mix(2), wv_ref[...])
    softplus = jnp.maximum(zw, 0.0) + jnp.log(1.0 + jnp.exp(-jnp.abs(zw)))
    ld = -jnp.exp(-softplus - 0.5)

    kk = k * k_k
    k_h = k * (1.0 + (a - 1.0) * k_a)
    ss_h = _dot_exact_rhs(kk * kk, to_head, 2)
    rk_h = _dot_exact_rhs(r * k_h * r_k, to_head, 2)
    ss = _dot_exact_rhs(ss_h, from_head, 3)
    bonus = _dot_exact_rhs(rk_h, from_head, 3) * v
    kk = kk / jnp.maximum(jnp.sqrt(ss), 1e-12)

    r_out[...] = r
    ld_out[...] = ld
    k_out[...] = k_h
    v_out[...] = v
    kk_out[...] = kk
    ka_out[...] = kk * a
    g_out[...] = g
    bonus_out[...] = bonus


def _rwkv_pre(x, bnd, mu8, vec8, wr, wk, wv, w1, w2, a1, a2, g1, g2, to_head, from_head, tm):
    n, d = x.shape
    row = pl.BlockSpec((tm, d), lambda i: (i, 0))
    ins = [row, pl.BlockSpec((1, SUBLANES, d), lambda i: (i, 0, 0))]
    ins += [_const_spec(a.shape)
            for a in (mu8, vec8, wr, wk, wv, w1, w2, a1, a2, g1, g2, to_head, from_head)]
    return pl.pallas_call(
        _rwkv_pre_kernel,
        grid=(n // tm,),
        in_specs=ins,
        out_specs=[row] * 8,
        out_shape=[jax.ShapeDtypeStruct((n, d), F32)] * 8,
        compiler_params=_params(("parallel",)),
        name="rwkv_pre",
    )(x, bnd, mu8, vec8, wr, wk, wv, w1, w2, a1, a2, g1, g2, to_head, from_head)


def _wkv_scan_kernel(r_ref, ld_ref, k_ref, v_ref, kk_ref, ka_ref, s0_ref, o_ref, st_ref, s_scr,
                     *, chunk, n_chunks, group):
    c = chunk
    pw = 2 * HEAD_A
    n_pp = s_scr.shape[0]
    t_idx = pl.program_id(2)

    @pl.when(t_idx == 0)
    def _():
        zeros_hh = jnp.zeros((HEAD_A, HEAD_A), F32)
        for pp in range(n_pp):
            s_scr[pp] = jnp.concatenate(
                [jnp.concatenate([s0_ref[pp, 0], zeros_hh], axis=1),
                 jnp.concatenate([zeros_hh, s0_ref[pp, 1]], axis=1)], axis=0)

    ri = lax.broadcasted_iota(jnp.int32, (c, c), 0)
    ci = lax.broadcasted_iota(jnp.int32, (c, c), 1)
    tri_incl = ri >= ci
    cum_mat = jnp.where(tri_incl, 1.0, 0.0).astype(BF16)
    ri4 = lax.broadcasted_iota(jnp.int32, (c, 4 * c), 0)
    ci4 = lax.broadcasted_iota(jnp.int32, (c, 4 * c), 1) & (c - 1)
    strict4 = ri4 > ci4
    incl4 = ri4 >= ci4
    first_blk = lax.broadcasted_iota(jnp.int32, (1, 2 * c), 1) < c
    eye_cat = (lax.broadcasted_iota(jnp.int32, (c, 2 * c), 0)
               == (lax.broadcasted_iota(jnp.int32, (c, 2 * c), 1) & (c - 1)))
    lane1 = lax.broadcasted_iota(jnp.int32, (1, pw), 1)
    head_a1 = lane1 < HEAD_A
    lane2 = lax.broadcasted_iota(jnp.int32, (1, 2 * pw), 1) & (pw - 1)
    head_a2 = lane2 < HEAD_A
    rs = lax.broadcasted_iota(jnp.int32, (pw, pw), 0)
    cs = lax.broadcasted_iota(jnp.int32, (pw, pw), 1)
    same_head = (rs < HEAD_A) == (cs < HEAD_A)
    eye = rs == cs
    zeros_cv = jnp.zeros((c, pw), F32)

    def group_maps(slices):
        each = lambda fn, *lists: [fn(*a) for a in zip(*lists)]
        ld = [ld_ref[ix] for ix in slices]
        cw = each(lambda x: _dot_exact_lhs(cum_mat, x, 3), ld)
        w_in = each(jnp.exp, cw)
        w_ex = each(lambda a, b: jnp.exp(a - b), cw, ld)
        w_inv = each(lambda a: jnp.exp(-a), cw)
        w_last = each(lambda a: a[c - 1:c, :], w_in)
        knt = [-(kk_ref[ix] * w) for ix, w in zip(slices, w_ex)]
        kat = [ka_ref[ix] * w for ix, w in zip(slices, w_inv)]
        kt = [k_ref[ix] * w for ix, w in zip(slices, w_inv)]
        rt = [r_ref[ix] * w for ix, w in zip(slices, w_in)]
        v = [v_ref[ix] for ix in slices]
        by_head = lambda z, is_a: jnp.concatenate(
            [jnp.where(is_a, z, 0.0), jnp.where(is_a, 0.0, z)], axis=0)
        lh = each(lambda a, b: jnp.concatenate([a, b], axis=0), knt, rt)
        rh = each(lambda a, b: jnp.concatenate(
            [by_head(a, head_a1), by_head(b, head_a1)], axis=0), kat, kt)
        v_st = each(lambda a: by_head(a, head_a1), v)

        full = each(_dot_nt, lh, rh)
        top = each(lambda a: jnp.where(strict4, a[:c, :], 0.0), full)
        bot = each(lambda a: jnp.where(incl4, a[c:, :], 0.0), full)
        n_cat = each(lambda a: a[:, :2 * c], top)
        akv = each(lambda a, b: _dot(a[:, 2 * c:], b), top, v_st)

        t_cat = each(lambda n: jnp.where(eye_cat, 1.0, 0.0) + n, n_cat)
        span = 2
        while span < c:
            n_cat = each(lambda n: _dot(n, by_head(n, first_blk)), n_cat)
            dt = each(lambda n, t0: _dot(n, by_head(t0, first_blk)), n_cat, t_cat)
            t_cat = each(lambda t0, d: t0 + d, t_cat, dt)
            span *= 2
        x = each(lambda t0, a, b: _dot(t0, by_head(jnp.concatenate([a, b], axis=1), head_a2)),
                 t_cat, knt, akv)

        v_wide = each(lambda a: jnp.concatenate([jnp.zeros_like(a), a], axis=1), v_st)
        qo_all = each(lambda b4, x0, vw: _dot(b4, jnp.concatenate([by_head(x0, head_a2), vw], axis=0)),
                      bot, x, v_wide)
        rhs2 = each(lambda a, b: jnp.concatenate(
            [a, jnp.concatenate([zeros_cv, b], axis=1)], axis=0), x, v)
        lt = each(lambda a, b, w: jnp.concatenate([a * w, b * w], axis=0), kat, kt, w_last)
        mb = each(_dot_tn, lt, rhs2)
        out = []
        for j in range(len(slices)):
            qo = qo_all[j]
            q = rt[j] + qo[:, :pw]
            m = jnp.where(eye, w_last[j], 0.0) + jnp.where(same_head, mb[j][:, :pw], 0.0)
            b = jnp.where(same_head, mb[j][:, pw:], 0.0)
            mq = jnp.concatenate([m, q], axis=0)
            out.append((mq.astype(BF16), b, qo[:, pw:]))
        return out

    def body(it, carry):
        items = [(j, pp) for j in range(group) for pp in range(n_pp)]
        slices = [(pl.ds(pl.multiple_of((it * group + j) * c, c), c), pl.ds(pp * pw, pw))
                  for j, pp in items]
        maps = group_maps(slices)
        s = [s_scr[pp] for pp in range(n_pp)]
        for (j, pp), ix, (mq, b, o0) in zip(items, slices, maps):
            res = _dot(mq, s[pp])
            s[pp] = res[:pw, :] + b
            o_ref[ix] = res[pw:, :] + o0
        for pp in range(n_pp):
            s_scr[pp] = s[pp]
        return carry

    lax.fori_loop(0, n_chunks // group, body, 0)

    @pl.when(t_idx == pl.num_programs(2) - 1)
    def _():
        for pp in range(n_pp):
            s = s_scr[pp]
            st_ref[pp, 0] = s[:HEAD_A, :HEAD_A]
            st_ref[pp, 1] = s[HEAD_A:, HEAD_A:]


def _wkv_scan(r, ld, k, v, kk, ka, s0, chunk, t_blk):
    bn, t, d = r.shape
    pw = 2 * HEAD_A
    n_pairs = d // pw
    n_chunks = t_blk // chunk
    n_pp = SCAN_PAIRS if n_chunks >= SCAN_GROUP else n_pairs
    seq = pl.BlockSpec((None, t_blk, n_pp * pw), lambda b, p, i: (b, i, p))
    st = pl.BlockSpec((None, n_pp, 2, HEAD_A, HEAD_A), lambda b, p, i: (b, p, 0, 0, 0))
    kern = functools.partial(_wkv_scan_kernel, chunk=chunk, n_chunks=n_chunks,
                             group=min(SCAN_GROUP, n_chunks))
    return pl.pallas_call(
        kern,
        grid=(bn, n_pairs // n_pp, t // t_blk),
        in_specs=[seq] * 6 + [st],
        out_specs=[seq, st],
        out_shape=[jax.ShapeDtypeStruct((bn, t, d), F32),
                   jax.ShapeDtypeStruct((bn, n_pairs, 2, HEAD_A, HEAD_A), F32)],
        scratch_shapes=[pltpu.VMEM((n_pp, pw, pw), F32)],
        compiler_params=_params(("parallel", "parallel", "arbitrary")),
        name="wkv_scan",
    )(r, ld, k, v, kk, ka, s0)


def _rwkv_post_kernel(o_ref, bonus_ref, g_ref, x_ref, vec_ref, th_ref, fh_ref, wo_ref, out_ref):
    to_head = th_ref[...]
    from_head = fh_ref[...]
    inv_n = 1.0 / HEAD_A
    tm = o_ref.shape[0]
    n_blk = POST_ROW_BLOCKS if tm % (POST_ROW_BLOCKS * SUBLANES) == 0 else 1
    blk = tm // n_blk
    sls = [pl.ds(b * blk, blk) for b in range(n_blk)]
    o = [o_ref[sl, :] for sl in sls]
    mean_h = [_dot_exact_rhs(a, to_head, 2) for a in o]
    mean = [_dot_exact_rhs(a, from_head, 3) * inv_n for a in mean_h]
    oc = [a - m for a, m in zip(o, mean)]
    var_h = [_dot_exact_rhs(a * a, to_head, 2) for a in oc]
    var = [_dot_exact_rhs(a, from_head, 3) * inv_n for a in var_h]
    y = [(c * lax.rsqrt(vr + GN_EPS) * vec_ref[0:1, :] + vec_ref[1:2, :] + bonus_ref[sl, :]) * g_ref[sl, :]
         for c, vr, sl in zip(oc, var, sls)]
    h = [_dot(a, wo_ref[...]) for a in y]
    for a, sl in zip(h, sls):
        out_ref[sl, :] = _layer_norm(ALPHA * x_ref[sl, :] + a, vec_ref[2:3, :], vec_ref[3:4, :])


def _rwkv_post(o, bonus, g, x, vec8, to_head, from_head, wo, tm):
    n, d = x.shape
    row = pl.BlockSpec((tm, d), lambda i: (i, 0))
    return pl.pallas_call(
        _rwkv_post_kernel,
        grid=(n // tm,),
        in_specs=[row] * 4 + [_const_spec(a.shape) for a in (vec8, to_head, from_head, wo)],
        out_specs=row,
        out_shape=jax.ShapeDtypeStruct((n, d), F32),
        compiler_params=_params(("parallel",)),
        name="rwkv_post",
    )(o, bonus, g, x, vec8, to_head, from_head, wo)


def _ffn_kernel(x_ref, wg_ref, wu_ref, wd_ref, ln_ref, out_ref, acc_ref, xb_ref):
    f = pl.program_id(1)

    @pl.when(f == 0)
    def _():
        acc_ref[...] = jnp.zeros_like(acc_ref)
        xb_ref[...] = x_ref[...].astype(BF16)

    xb = xb_ref[...]
    gate = jnp.dot(xb, wg_ref[...].astype(BF16), preferred_element_type=F32)
    up = jnp.dot(xb, wu_ref[...].astype(BF16), preferred_element_type=F32)
    h = gate * _sigmoid(gate) * up
    acc_ref[...] += jnp.dot(h.astype(BF16), wd_ref[...].astype(BF16), preferred_element_type=F32)

    @pl.when(f == pl.num_programs(1) - 1)
    def _():
        out_ref[...] = _layer_norm(ALPHA * x_ref[...] + acc_ref[...], ln_ref[0:1, :], ln_ref[1:2, :])


def _ffn(x, w_gu, w_down, ln8, tm, tf):
    n, d = x.shape
    d_ff = w_gu.shape[1] // 2
    nf = d_ff // tf
    return pl.pallas_call(
        _ffn_kernel,
        grid=(n // tm, nf),
        in_specs=[
            pl.BlockSpec((tm, d), lambda i, f: (i, 0)),
            pl.BlockSpec((d, tf), lambda i, f: (0, f)),
            pl.BlockSpec((d, tf), lambda i, f: (0, nf + f)),
            pl.BlockSpec((tf, d), lambda i, f: (f, 0)),
            _const_spec(ln8.shape),
        ],
        out_specs=pl.BlockSpec((tm, d), lambda i, f: (i, 0)),
        out_shape=jax.ShapeDtypeStruct((n, d), F32),
        scratch_shapes=[pltpu.VMEM((tm, d), F32), pltpu.VMEM((tm, d), BF16)],
        compiler_params=_params(("parallel", "arbitrary")),
        name="dense_ffn",
    )(x, w_gu, w_gu, w_down, ln8)


ROUTE_I1, ROUTE_I2, ROUTE_G1, ROUTE_G2, ROUTE_Q1, ROUTE_Q2 = range(6)
META_CARRY, META_COUNT = 0, 1


def _proj_ln_router_kernel(y_ref, x_ref, w_ref, ln_ref, rw_ref, out_ref, route_ref, meta_ref, count_ref,
                           carry_ref):
    h = _dot(y_ref[...], w_ref[...])
    x3 = _layer_norm(ALPHA * x_ref[...] + h, ln_ref[0:1, :], ln_ref[1:2, :])
    out_ref[...] = x3
    _route_rows(x3, rw_ref, route_ref, meta_ref, count_ref, carry_ref)


def _route_rows(x, w_ref, route_ref, meta_ref, count_ref, carry_ref):
    i = pl.program_id(0)

    @pl.when(i == 0)
    def _():
        carry_ref[...] = jnp.zeros_like(carry_ref)

    x_hi, x_lo = _split(x, 2)
    w_hi, w_lo = _split(w_ref[...], 2)
    logits = (jnp.dot(x_hi, w_hi, preferred_element_type=F32)
              + jnp.dot(x_lo, w_hi, preferred_element_type=F32)
              + jnp.dot(x_hi, w_lo, preferred_element_type=F32))
    tm = logits.shape[0]
    lane = lax.broadcasted_iota(jnp.int32, logits.shape, 1).astype(F32)
    neg = -jnp.inf
    logits = jnp.where(lane < N_EXPERTS, logits, neg)
    m1 = jnp.max(logits, axis=-1, keepdims=True)
    i1 = jnp.min(jnp.where(logits == m1, lane, float(LANES)), axis=-1, keepdims=True)
    rest = jnp.where(lane == i1, neg, logits)
    m2 = jnp.max(rest, axis=-1, keepdims=True)
    i2 = jnp.min(jnp.where(rest == m2, lane, float(LANES)), axis=-1, keepdims=True)
    e2 = jnp.exp(m2 - m1)
    den = 1.0 + e2

    sel1 = lane == i1
    sel2 = lane == i2
    onehot = jnp.where(sel1, 1.0, 0.0) + jnp.where(sel2, 1.0, 0.0)
    ri = lax.broadcasted_iota(jnp.int32, (tm, tm), 0)
    ci = lax.broadcasted_iota(jnp.int32, (tm, tm), 1)
    earlier = jnp.where(ri > ci, 1.0, 0.0).astype(BF16)
    in_tile = jnp.dot(earlier, onehot.astype(BF16), preferred_element_type=F32)
    q1 = jnp.sum(jnp.where(sel1, in_tile, 0.0), axis=-1, keepdims=True)
    q2 = jnp.sum(jnp.where(sel2, in_tile, 0.0), axis=-1, keepdims=True)
    tile_count = jnp.sum(onehot, axis=0, keepdims=True)

    route = jnp.zeros_like(logits)
    for col, val in ((ROUTE_I1, i1), (ROUTE_I2, i2), (ROUTE_G1, 1.0 / den), (ROUTE_G2, e2 / den),
                     (ROUTE_Q1, q1), (ROUTE_Q2, q2)):
        route = jnp.where(lane == float(col), val, route)
    route_ref[...] = route
    meta_row = lax.broadcasted_iota(jnp.int32, meta_ref.shape, 0)
    meta_ref[...] = jnp.where(meta_row == META_CARRY, carry_ref[0:1, :],
                              jnp.where(meta_row == META_COUNT, tile_count, 0.0))
    slab_rows = jnp.floor((tile_count + (SLAB_UNIT - 1)) * (1.0 / SLAB_UNIT)) * SLAB_UNIT
    carry_ref[0:1, :] = carry_ref[0:1, :] + slab_rows
    count_ref[...] = carry_ref[...]


def _proj_ln_router(y, x, w, ln8, router_pad, tm):
    n, d = x.shape
    row = pl.BlockSpec((tm, d), lambda i: (i, 0))
    return pl.pallas_call(
        _proj_ln_router_kernel,
        grid=(n // tm,),
        in_specs=[pl.BlockSpec((tm, y.shape[1]), lambda i: (i, 0)), row,
                  _const_spec(w.shape), _const_spec(ln8.shape), _const_spec(router_pad.shape)],
        out_specs=[row,
                   pl.BlockSpec((tm, LANES), lambda i: (i, 0)),
                   pl.BlockSpec((None, SUBLANES, LANES), lambda i: (i, 0, 0)),
                   _const_spec((SUBLANES, LANES))],
        out_shape=[jax.ShapeDtypeStruct((n, d), F32),
                   jax.ShapeDtypeStruct((n, LANES), F32),
                   jax.ShapeDtypeStruct((n // tm, SUBLANES, LANES), F32),
                   jax.ShapeDtypeStruct((SUBLANES, LANES), F32)],
        scratch_shapes=[pltpu.VMEM((SUBLANES, LANES), F32)],
        compiler_params=_params(("arbitrary",)),
        name="proj_ln_router",
    )(y, x, w, ln8, router_pad)


def _slab_plan(meta_ref):
    plan = []
    off = 0
    for e in range(N_EXPERTS):
        units = (meta_ref[0, e] + (SLAB_UNIT - 1)) // SLAB_UNIT
        plan.append((units, meta_ref[0, N_EXPERTS + e], off))
        off = off + units * SLAB_UNIT
    return plan


def _slab_buffer_rows(tm):
    worst = 2 * tm + N_EXPERTS * (SLAB_UNIT - 1)
    return ((worst + LANES - 1) // LANES) * LANES


def _slab_rows(expert, rank, plan):
    off = jnp.zeros_like(expert)
    for e, (_, _, e_off) in enumerate(plan):
        off = jnp.where(expert == e, e_off, off)
    return off + rank


def _slab_copies(plan, make_copy):
    total = 0
    for units, first_row, off in plan:
        def start(u, carry, first_row=first_row, off=off):
            make_copy(pl.multiple_of(off + u * SLAB_UNIT, SLAB_UNIT),
                      pl.multiple_of(first_row + u * SLAB_UNIT, SLAB_UNIT)).start()
            return carry
        lax.fori_loop(0, units, start, 0)
        total = total + units
    return total


def _dispatch_kernel(meta_ref, meta_p1_ref, meta_p2_ref, route_ref, x_ref, xs_in_ref, xs_ref, cbuf, sem):
    del xs_in_ref
    tm = x_ref.shape[0]
    i = pl.program_id(0)
    slot = i & 1
    plan = _slab_plan(meta_ref)
    n_copies = lambda ref: sum(units for units, _, _ in _slab_plan(ref))

    def make_copy_in(s):
        def make_copy(buf_row, sorted_row):
            return pltpu.make_async_copy(cbuf.at[s, pl.ds(buf_row, SLAB_UNIT), :],
                                         xs_ref.at[pl.ds(sorted_row, SLAB_UNIT), :], sem.at[s])
        return make_copy

    def drain(s, count):
        def wait(u, carry):
            make_copy_in(s)(0, 0).wait()
            return carry
        lax.fori_loop(0, count, wait, 0)

    @pl.when(i >= 2)
    def _():
        drain(slot, n_copies(meta_p2_ref))

    sel = (lax.broadcasted_iota(jnp.int32, (SUBLANES, LANES), 0)
           == lax.broadcasted_iota(jnp.int32, (SUBLANES, LANES), 1)).astype(BF16)
    route_t = None
    for piece in _split(route_ref[...], 3):
        t = lax.dot_general(sel, piece, (((1,), (1,)), ((), ())), preferred_element_type=F32)
        route_t = t if route_t is None else route_t + t
    route_t = route_t.astype(jnp.int32)
    row1 = _slab_rows(route_t[ROUTE_I1:ROUTE_I1 + 1, :], route_t[ROUTE_Q1:ROUTE_Q1 + 1, :], plan)
    row2 = _slab_rows(route_t[ROUTE_I2:ROUTE_I2 + 1, :], route_t[ROUTE_Q2:ROUTE_Q2 + 1, :], plan)

    r_iota = lax.broadcasted_iota(jnp.int32, (cbuf.shape[1], tm), 0)
    perm = jnp.where((r_iota == row1) | (r_iota == row2), 1.0, 0.0).astype(BF16)
    cbuf[slot] = jnp.dot(perm, x_ref[...].astype(BF16), preferred_element_type=F32)
    _slab_copies(plan, make_copy_in(slot))

    @pl.when(i == pl.num_programs(0) - 1)
    def _():
        @pl.when(i >= 1)
        def _():
            drain(1 - slot, n_copies(meta_p1_ref))
        drain(slot, n_copies(meta_ref))


def _dispatch(meta, route, x, xs, tm):
    n, d = x.shape
    meta_spec = lambda back: pl.BlockSpec((None, 1, 2 * N_EXPERTS),
                                          lambda i: (jnp.maximum(i - back, 0), 0, 0),
                                          memory_space=pltpu.SMEM)
    return pl.pallas_call(
        _dispatch_kernel,
        grid=(n // tm,),
        in_specs=[meta_spec(0), meta_spec(1), meta_spec(2),
                  pl.BlockSpec((tm, LANES), lambda i: (i, 0)),
                  pl.BlockSpec((tm, d), lambda i: (i, 0)),
                  pl.BlockSpec(memory_space=pl.ANY)],
        out_specs=pl.BlockSpec(memory_space=pl.ANY),
        out_shape=jax.ShapeDtypeStruct(xs.shape, xs.dtype),
        scratch_shapes=[pltpu.VMEM((2, _slab_buffer_rows(tm), d), F32), pltpu.SemaphoreType.DMA((2,))],
        input_output_aliases={5: 0},
        compiler_params=_params(("arbitrary",)),
        name="moe_dispatch",
    )(meta, meta, meta, route, x, xs)


def _experts_kernel(te_ref, rows_ref, x_ref, wg_ref, wu_ref, wd_ref, out_ref, acc_ref, xb_ref):
    del te_ref
    f = pl.program_id(1)
    rows = rows_ref[pl.program_id(0)]
    tm = x_ref.shape[0]

    @pl.when(f == 0)
    def _():
        acc_ref[...] = jnp.zeros_like(acc_ref)

    def swiglu_rows(n_rows):
        sl = pl.ds(0, n_rows)

        @pl.when(f == 0)
        def _():
            xb_ref[sl, :] = x_ref[sl, :].astype(BF16)

        xb = xb_ref[sl, :]
        gate = jnp.dot(xb, wg_ref[...].astype(BF16), preferred_element_type=F32)
        up = jnp.dot(xb, wu_ref[...].astype(BF16), preferred_element_type=F32)
        h = gate * _sigmoid(gate) * up
        acc_ref[sl, :] += jnp.dot(h.astype(BF16), wd_ref[...].astype(BF16), preferred_element_type=F32)

    part = tm // MOE_TILE_PARTS
    for k in range(1, MOE_TILE_PARTS + 1):
        @pl.when(jnp.logical_and(rows > (k - 1) * part, rows <= k * part))
        def _(k=k):
            swiglu_rows(k * part)

    @pl.when(f == pl.num_programs(1) - 1)
    def _():
        out_ref[...] = acc_ref[...]


def _experts(tile_expert, tile_rows, xs, w_gu, w_down, tm, tf):
    s_total, d = xs.shape
    d_ff = w_gu.shape[2] // 2
    nf = d_ff // tf

    def f_eff(i, f, rows):
        return jnp.where(rows[i] > 0, f, nf - 1)

    grid_spec = pltpu.PrefetchScalarGridSpec(
        num_scalar_prefetch=2,
        grid=(s_total // tm, nf),
        in_specs=[
            pl.BlockSpec((tm, d), lambda i, f, te, rows: (i, 0)),
            pl.BlockSpec((None, d, tf), lambda i, f, te, rows: (te[i], 0, f_eff(i, f, rows))),
            pl.BlockSpec((None, d, tf), lambda i, f, te, rows: (te[i], 0, nf + f_eff(i, f, rows))),
            pl.BlockSpec((None, tf, d), lambda i, f, te, rows: (te[i], f_eff(i, f, rows), 0)),
        ],
        out_specs=pl.BlockSpec((tm, d), lambda i, f, te, rows: (i, 0)),
        scratch_shapes=[pltpu.VMEM((tm, d), F32), pltpu.VMEM((tm, d), BF16)],
    )
    return pl.pallas_call(
        _experts_kernel,
        grid_spec=grid_spec,
        out_shape=jax.ShapeDtypeStruct((s_total, d), F32),
        compiler_params=_params(("arbitrary", "arbitrary")),
        name="moe_experts",
    )(tile_expert, tile_rows, xs, w_gu, w_gu, w_down)


def _combine_kernel(meta_ref, meta_next_ref, route_ref, x_ref, ys_ref, ln_ref, out_ref, cbuf, sem):
    tm = x_ref.shape[0]
    i = pl.program_id(0)
    slot = i & 1
    plan = _slab_plan(meta_ref)

    def gather(tile_plan, dst_slot):
        def make_copy(buf_row, sorted_row):
            return pltpu.make_async_copy(ys_ref.at[pl.ds(sorted_row, SLAB_UNIT), :],
                                         cbuf.at[dst_slot, pl.ds(buf_row, SLAB_UNIT), :],
                                         sem.at[dst_slot])
        return make_copy, lambda: _slab_copies(tile_plan, make_copy)

    @pl.when(i == 0)
    def _():
        cbuf[...] = jnp.zeros_like(cbuf)
        gather(plan, 0)[1]()

    @pl.when(i + 1 < pl.num_programs(0))
    def _():
        gather(_slab_plan(meta_next_ref), 1 - slot)[1]()

    make_copy = gather(plan, slot)[0]
    n_mine = sum(units for units, _, _ in plan)

    def wait(u, carry):
        make_copy(0, 0).wait()
        return carry

    lax.fori_loop(0, n_mine, wait, 0)

    route = route_ref[...]
    col = lambda k: route[:, k:k + 1]
    row1 = _slab_rows(col(ROUTE_I1).astype(jnp.int32), col(ROUTE_Q1).astype(jnp.int32), plan)
    row2 = _slab_rows(col(ROUTE_I2).astype(jnp.int32), col(ROUTE_Q2).astype(jnp.int32), plan)
    c_iota = lax.broadcasted_iota(jnp.int32, (tm, cbuf.shape[1]), 1)
    gate = (jnp.where(c_iota == row1, col(ROUTE_G1), 0.0)
            + jnp.where(c_iota == row2, col(ROUTE_G2), 0.0))
    g_hi, g_lo = _split(gate, 2)
    c_hi, c_lo = _split(cbuf[slot], 2)
    y = (jnp.dot(g_hi, c_hi, preferred_element_type=F32)
         + jnp.dot(g_hi, c_lo, preferred_element_type=F32)
         + jnp.dot(g_lo, c_hi, preferred_element_type=F32))
    out_ref[...] = _layer_norm(ALPHA * x_ref[...] + y, ln_ref[0:1, :], ln_ref[1:2, :])


def _combine(meta, route, x, ys, ln8, tm):
    n, d = x.shape
    last = n // tm - 1
    meta_spec = lambda shift: pl.BlockSpec((None, 1, 2 * N_EXPERTS),
                                           lambda i: (jnp.minimum(i + shift, last), 0, 0),
                                           memory_space=pltpu.SMEM)
    return pl.pallas_call(
        _combine_kernel,
        grid=(n // tm,),
        in_specs=[meta_spec(0), meta_spec(1),
                  pl.BlockSpec((tm, LANES), lambda i: (i, 0)),
                  pl.BlockSpec((tm, d), lambda i: (i, 0)),
                  pl.BlockSpec(memory_space=pl.ANY),
                  _const_spec(ln8.shape)],
        out_specs=pl.BlockSpec((tm, d), lambda i: (i, 0)),
        out_shape=jax.ShapeDtypeStruct((n, d), F32),
        scratch_shapes=[pltpu.VMEM((2, _slab_buffer_rows(tm), d), F32), pltpu.SemaphoreType.DMA((2,))],
        compiler_params=_params(("arbitrary",)),
        name="moe_combine",
    )(meta, meta, route, x, ys, ln8)


def _rope(y, cos_t, sin_next, sin_prev):
    n = y.shape[1]
    reps = n // LANES
    tile = lambda t: jnp.concatenate([t] * reps, axis=1) if reps > 1 else t
    half = ROPE_DIM // 2
    return (y * tile(cos_t)
            + pltpu.roll(y, n - half, 1) * tile(sin_next)
            + pltpu.roll(y, half, 1) * tile(sin_prev))


def _qkv_kernel(x_ref, wq_ref, wkv_ref, cos_ref, sn_ref, sp_ref, q_out, k_out, v_out):
    xb = x_ref[...].astype(BF16)
    tables = (cos_ref[...], sn_ref[...], sp_ref[...])
    q_out[...] = _rope(jnp.dot(xb, wq_ref[...], preferred_element_type=F32), *tables)
    kv = jnp.dot(xb, wkv_ref[...], preferred_element_type=F32)
    nk = k_out.shape[1]
    k_out[...] = _rope(kv[:, :nk], *tables)
    v_out[...] = kv[:, nk:]


def _qkv_proj(x, wq, wkv, tables, tm):
    n, d = x.shape
    kvw = wkv.shape[1] // 2
    t_tiles = tables[0].shape[0] // tm
    tab = pl.BlockSpec((tm, LANES), lambda i: (i % t_tiles, 0))
    widths = (wq.shape[1], kvw, kvw)
    return pl.pallas_call(
        _qkv_kernel,
        grid=(n // tm,),
        in_specs=[pl.BlockSpec((tm, d), lambda i: (i, 0)), _const_spec(wq.shape), _const_spec(wkv.shape),
                  tab, tab, tab],
        out_specs=[pl.BlockSpec((tm, ow), lambda i: (i, 0)) for ow in widths],
        out_shape=[jax.ShapeDtypeStruct((n, ow), F32) for ow in widths],
        compiler_params=_params(("parallel",)),
        name="qkv_proj",
    )(x, wq, wkv, *tables)


def _attn_kernel(sink_ref, q_ref, kp_ref, kc_ref, vp_ref, vc_ref, o_ref, *, banded):
    tq = q_ref.shape[0]
    n_prev = kp_ref.shape[0]
    tk = n_prev + kc_ref.shape[0]
    q = q_ref[...] * ATTN_SCALE
    kband = jnp.concatenate([kp_ref[...], kc_ref[...]], axis=0)
    vband = jnp.concatenate([vp_ref[...], vc_ref[...]], axis=0)
    n_heads = q.shape[1] // HEAD_B
    group = n_heads // KV_HEADS
    qc = CHUNK if banded else tq
    kc = WINDOW + CHUNK if banded else tk
    units = [(kh, ci) for kh in range(KV_HEADS) for ci in range(tq // qc)]
    if banded:
        kj = lax.broadcasted_iota(jnp.int32, (kc, group * qc), 0)
        band_start = pl.program_id(1) * tq - n_prev

    qlane = lax.broadcasted_iota(jnp.int32, (1, group * qc), 1)

    def scores(kh, ci):
        qs = jnp.concatenate([q[ci * qc:(ci + 1) * qc, (kh * group + j) * HEAD_B:(kh * group + j + 1) * HEAD_B]
                              for j in range(group)], axis=0)
        return _dot_nt(kband[ci * qc:ci * qc + kc, kh * HEAD_B:(kh + 1) * HEAD_B], qs)

    def softmax(kh, ci, s):
        sk = jnp.full((1, group * qc), sink_ref[kh * group], F32)
        for j in range(1, group):
            sk = jnp.where(qlane >= j * qc, sink_ref[kh * group + j], sk)
        if banded:
            s = jnp.where(band_start + ci * qc + kj >= 0, s, -jnp.inf)
        m = jnp.maximum(jnp.max(s, axis=0, keepdims=True), sk)
        p = jnp.exp(s - m)
        return p * (1.0 / (jnp.sum(p, axis=0, keepdims=True) + jnp.exp(sk - m)))

    def weighted(kh, ci, p):
        return _dot_tn(p, vband[ci * qc:ci * qc + kc, kh * HEAD_B:(kh + 1) * HEAD_B])

    chunks = range(tq // qc)
    outs = {}
    s_next = [scores(0, ci) for ci in chunks]
    for kh in range(KV_HEADS):
        s_cur = s_next
        if kh + 1 < KV_HEADS:
            s_next = [scores(kh + 1, ci) for ci in chunks]
        probs = [softmax(kh, ci, s_cur[ci]) for ci in chunks]
        for ci in chunks:
            o = weighted(kh, ci, probs[ci])
            for j in range(group):
                outs[(kh * group + j, ci)] = o[j * qc:(j + 1) * qc, :]
    o_ref[...] = jnp.concatenate(
        [jnp.concatenate([outs[(h, ci)] for ci in range(tq // qc)], axis=0) for h in range(n_heads)],
        axis=1)


def _attention(q, k_prev_src, k_cur_src, v_prev_src, v_cur_src, sinks, tq, banded):
    bn, t, d = q.shape
    kw = k_cur_src.shape[2]
    if banded:
        ratio = tq // WINDOW
        prev_map = lambda b, i: (b, jnp.maximum(i * ratio - 1, 0), 0)
    else:
        prev_map = lambda b, i: (b, 0, 0)
    prev = pl.BlockSpec((None, WINDOW, kw), prev_map)
    cur = pl.BlockSpec((None, tq, kw), lambda b, i: (b, i, 0))
    kern = functools.partial(_attn_kernel, banded=banded)
    return pl.pallas_call(
        kern,
        grid=(bn, t // tq),
        in_specs=[pl.BlockSpec(memory_space=pltpu.SMEM),
                  pl.BlockSpec((None, tq, d), lambda b, i: (b, i, 0)), prev, cur, prev, cur],
        out_specs=pl.BlockSpec((None, tq, d), lambda b, i: (b, i, 0)),
        out_shape=jax.ShapeDtypeStruct((bn, t, d), F32),
        compiler_params=_params(("parallel", "parallel")),
        name="swa_attn",
    )(sinks, q, k_prev_src, k_cur_src, v_prev_src, v_cur_src)


def _moe_layer(groups, P):
    xs_rows = [g[0] for g in groups]
    routed = [g[1:] for g in groups]
    d = xs_rows[0].shape[1]
    tm_e = MOE_TILE
    counts = [c[0, :N_EXPERTS].astype(jnp.int32) for _, _, c in routed]
    total = sum(counts)
    padded = ((total + tm_e - 1) // tm_e) * tm_e
    ends = jnp.cumsum(padded)
    starts = ends - padded
    n_slabs = sum(-(-x.shape[0] // MOE_ROW_TILE) for x in xs_rows) * N_EXPERTS
    n_assign = 2 * sum(x.shape[0] for x in xs_rows) + n_slabs * (SLAB_UNIT - 1)
    n_tiles = (n_assign + N_EXPERTS * (tm_e - 1)) // tm_e
    n_used = (ends[-1] // tm_e).astype(jnp.int32)
    tile_expert = jnp.sum((jnp.arange(n_tiles) * tm_e)[:, None] >= ends[None, :], axis=1).astype(jnp.int32)
    tile_expert = jnp.minimum(tile_expert, N_EXPERTS - 1)
    tile_expert = jnp.where(jnp.arange(n_tiles) < n_used, tile_expert,
                            tile_expert[jnp.maximum(n_used - 1, 0)])
    tile_rows = jnp.clip((starts + total)[tile_expert] - jnp.arange(n_tiles) * tm_e, 0, tm_e)
    tile_rows = jnp.where(jnp.arange(n_tiles) < n_used, tile_rows, 0).astype(jnp.int32)

    metas = []
    base = starts
    for (_, meta, _), cnt in zip(routed, counts):
        carry = meta[:, META_CARRY, :N_EXPERTS].astype(jnp.int32)
        sent = meta[:, META_COUNT, :N_EXPERTS].astype(jnp.int32)
        metas.append(jnp.concatenate([sent, base[None, :] + carry], axis=1)[:, None, :])
        base = base + cnt

    xs = jnp.zeros((n_tiles * tm_e, d), F32)
    for m, (route, _, _), x in zip(metas, routed, xs_rows):
        xs = _dispatch(m, route, x, xs, min(MOE_ROW_TILE, x.shape[0]))
    ys = _experts(tile_expert, tile_rows, xs, P['moe_w_gu'][0], P['moe_w_down'][0], tm_e, FFN_COLS)
    ln11 = _pad_rows([P['ln_g'][1, 1], P['ln_b'][1, 1]], d)
    return [_combine(m, route, x, ys, ln11, min(MOE_ROW_TILE, x.shape[0]))
            for m, (route, _, _), x in zip(metas, routed, xs_rows)]


def _pad_rows(rows, d):
    a = jnp.stack(rows).astype(F32)
    return jnp.concatenate([a, jnp.zeros((SUBLANES - a.shape[0], d), F32)], axis=0)


def _rope_tables(pos, reps):
    inv_freq = ROPE_THETA ** (-jnp.arange(0, ROPE_DIM, 2, dtype=jnp.float32) / ROPE_DIM)
    ang = pos.astype(jnp.float32)[:, None] * inv_freq[None, :]
    cos = jnp.cos(ang)
    sin = jnp.sin(ang)
    t = pos.shape[0]
    half = ROPE_DIM // 2
    rest = HEAD_B - ROPE_DIM
    z_half = jnp.zeros((t, half), F32)
    z_rest = jnp.zeros((t, rest), F32)
    cos_h = jnp.concatenate([cos, cos, jnp.ones((t, rest), F32)], axis=1)
    sn_h = jnp.concatenate([-sin, z_half, z_rest], axis=1)
    sp_h = jnp.concatenate([z_half, sin, z_rest], axis=1)
    per_tile = LANES // HEAD_B
    return tuple(jnp.tile(a, (reps, per_tile)) for a in (cos_h, sn_h, sp_h))


def _trunk(x, shift_in, wkv_in, k_cache, v_cache, pos0, P):
    bn, t, d = x.shape
    n = bn * t
    h_a = d // HEAD_A
    pw = 2 * HEAD_A
    n_pairs = d // pw
    xf = x.reshape(n, d)

    tm_pre = min(256, t)
    tm_row = min(256, n)
    tm_ffn = min(1024, n)
    chunk = min(CHUNK, t)
    t_blk = min(512, t)

    tiles = jnp.arange(n // tm_pre) * tm_pre
    prev_rows = xf[jnp.maximum(tiles - 1, 0)]
    start_rows = shift_in[0][tiles // t]
    bnd = jnp.where(((tiles % t) == 0)[:, None], start_rows, prev_rows)
    bnd = jnp.broadcast_to(bnd[:, None, :], (n // tm_pre, SUBLANES, d))

    mu8 = jnp.concatenate([P['a_mu'][0], jnp.zeros((2, d), F32)], axis=0)
    vec_pre = _pad_rows([P['a_w0'][0], P['a_a0'][0], P['a_k_k'][0], P['a_k_a'][0],
                         P['a_r_k'][0].reshape(d)], d)
    lane_head = jnp.arange(d) // HEAD_A
    to_head = (lane_head[:, None] == jnp.arange(LANES)[None, :]).astype(BF16)
    from_head = to_head.T
    bf = lambda a: a.astype(BF16)
    w_rkv = P['a_w_rkv'][0]
    r, ld, k_h, v, kk, ka, g, bonus = _rwkv_pre(
        xf, bnd, mu8, vec_pre, bf(w_rkv[0]), bf(w_rkv[1]), bf(w_rkv[2]),
        bf(P['a_w1'][0]), bf(P['a_w2'][0]), bf(P['a_a1'][0]), bf(P['a_a2'][0]),
        bf(P['a_g1'][0]), bf(P['a_g2'][0]), to_head, from_head, tm_pre)

    s0 = jnp.swapaxes(wkv_in[0].astype(F32), -1, -2).reshape(bn, n_pairs, 2, HEAD_A, HEAD_A)
    seq3 = lambda a: a.reshape(bn, t, d)
    o, s_fin = _wkv_scan(seq3(r), seq3(ld), seq3(k_h), seq3(v), seq3(kk), seq3(ka), s0, chunk, t_blk)
    wkv_out = jnp.swapaxes(s_fin, -1, -2).reshape(bn, h_a, HEAD_A, HEAD_A)
    shift_out = x[:, -1]

    vec_post = _pad_rows([P['a_lnx_g'][0], P['a_lnx_b'][0], P['ln_g'][0, 0], P['ln_b'][0, 0]], d)
    x1 = _rwkv_post(o.reshape(n, d), bonus, g, xf, vec_post, to_head, from_head, bf(P['a_w_o'][0]),
                    min(512, n))

    ln01 = _pad_rows([P['ln_g'][0, 1], P['ln_b'][0, 1]], d)
    x2 = _ffn(x1, P['ffn_w_gu'][0], P['ffn_w_down'][0], ln01, tm_ffn, FFN_COLS)

    pos = pos0 + jnp.arange(t, dtype=jnp.int32)
    tm_qkv = min(512, n)
    tables = _rope_tables(pos, max(tm_qkv // t, 1))
    kvw = KV_HEADS * HEAD_B
    q, k_new, v_new = _qkv_proj(x2, bf(P['b_w_q'][0]), bf(P['kv_w']), tables, tm_qkv)
    k_new = k_new.reshape(bn, t, kvw)
    v_new = v_new.reshape(bn, t, kvw)
    q = q.reshape(bn, t, d)
    sinks = P['b_sinks'][0].astype(F32)
    if k_cache is None:
        att = _attention(q, k_new, k_new, v_new, v_new, sinks, min(256, t), banded=True)
        k_out = k_new[:, -WINDOW:]
        v_out = v_new[:, -WINDOW:]
    else:
        kc = k_cache.astype(F32).reshape(bn, WINDOW, kvw)
        vc = v_cache.astype(F32).reshape(bn, WINDOW, kvw)
        att = _attention(q, kc, k_new, vc, v_new, sinks, t, banded=False)
        k_out = jnp.concatenate([kc, k_new], axis=1)[:, -WINDOW:]
        v_out = jnp.concatenate([vc, v_new], axis=1)[:, -WINDOW:]
    ln10 = _pad_rows([P['ln_g'][1, 0], P['ln_b'][1, 0]], d)
    router_pad = jnp.concatenate([P['moe_router'][0], jnp.zeros((d, LANES - N_EXPERTS), F32)], axis=1)
    routed = _proj_ln_router(att.reshape(n, d), x2, bf(P['b_w_o'][0]), ln10, router_pad,
                             min(MOE_ROW_TILE, n))

    return (routed, shift_out[None], wkv_out[None],
            k_out.reshape(bn, WINDOW, KV_HEADS, HEAD_B), v_out.reshape(bn, WINDOW, KV_HEADS, HEAD_B))


def kernel(x_prompt, x_sample, cache_shift_a, state_wkv_a, cache_k_b, cache_v_b, a_mu, a_w_rkv, a_w0, a_w1, a_w2, a_a0, a_a1, a_a2, a_g1, a_g2, a_k_k, a_k_a, a_r_k, a_lnx_g, a_lnx_b, a_w_o, kv_w, b_w_q, b_sinks, b_w_o, ln_g, ln_b, ffn_w_gu, ffn_w_down, moe_router, moe_w_gu, moe_w_down):
    P = {
        'a_mu': a_mu, 'a_w_rkv': a_w_rkv, 'a_w0': a_w0, 'a_w1': a_w1, 'a_w2': a_w2,
        'a_a0': a_a0, 'a_a1': a_a1, 'a_a2': a_a2, 'a_g1': a_g1, 'a_g2': a_g2,
        'a_k_k': a_k_k, 'a_k_a': a_k_a, 'a_r_k': a_r_k, 'a_lnx_g': a_lnx_g,
        'a_lnx_b': a_lnx_b, 'a_w_o': a_w_o, 'kv_w': kv_w, 'b_w_q': b_w_q,
        'b_sinks': b_sinks, 'b_w_o': b_w_o, 'ln_g': ln_g, 'ln_b': ln_b,
        'ffn_w_gu': ffn_w_gu, 'ffn_w_down': ffn_w_down, 'moe_router': moe_router,
        'moe_w_gu': moe_w_gu, 'moe_w_down': moe_w_down,
    }
    bp = x_prompt.shape[0]
    d = x_prompt.shape[2]
    h_a = d // HEAD_A
    zero_shift = jnp.zeros((1, bp, d), x_prompt.dtype)
    zero_wkv = jnp.zeros((1, bp, h_a, HEAD_A, HEAD_A), F32)
    x3_p, p_shift, p_wkv, p_k, p_v = _trunk(x_prompt, zero_shift, zero_wkv, None, None, 0, P)
    x3_s, s_shift, s_wkv, s_k, s_v = _trunk(x_sample, cache_shift_a, state_wkv_a,
                                            cache_k_b, cache_v_b, PAST_LEN, P)
    y_p, y_s = _moe_layer([x3_p, x3_s], P)
    return (y_p.reshape(x_prompt.shape), y_s.reshape(x_sample.shape),
            p_shift, p_wkv, p_k, p_v, s_shift, s_wkv, s_k, s_v)
```

```python
import functools

import jax
import jax.numpy as jnp
from jax import lax
from jax.experimental import pallas as pl
from jax.experimental.pallas import tpu as pltpu

F32 = jnp.float32
BF16 = jnp.bfloat16

DEPTH = 2
HEAD_A = 64
HEAD_B = 64
KV_HEADS = 4
CHUNK = 64
WINDOW = 128
PAST_LEN = 4096
ROPE_DIM = HEAD_B // 4
ROPE_THETA = 500000.0
ATTN_SCALE = HEAD_B ** -0.5
N_EXPERTS = 8
GN_EPS = 64e-5
LN_EPS = 1e-5
ALPHA = (2.0 * DEPTH) ** 0.25

LANES = 128
SUBLANES = 8
VMEM_LIMIT_BYTES = 56 * 1024 * 1024
POST_ROW_BLOCKS = 2
FFN_COLS = 512
SCAN_GROUP = 8
SCAN_PAIRS = 2
MOE_TILE = 1024
MOE_ROW_TILE = 512
MOE_TILE_PARTS = 4
SLAB_UNIT = SUBLANES
ZERO_ROWS = 128


def _dot(a, b):
    return jnp.dot(a.astype(BF16), b.astype(BF16), preferred_element_type=F32)


def _dot_nt(a, b):
    return lax.dot_general(a.astype(BF16), b.astype(BF16), (((1,), (1,)), ((), ())),
                           preferred_element_type=F32)


def _dot_tn(a, b):
    return lax.dot_general(a.astype(BF16), b.astype(BF16), (((0,), (0,)), ((), ())),
                           preferred_element_type=F32)


def _split(x, n):
    parts = []
    rem = x
    for i in range(n):
        p = rem.astype(BF16)
        parts.append(p)
        if i + 1 < n:
            rem = rem - p.astype(F32)
    return parts


def _dot_exact_rhs(a, b_bf16, n):
    acc = None
    for p in _split(a, n):
        t = jnp.dot(p, b_bf16, preferred_element_type=F32)
        acc = t if acc is None else acc + t
    return acc


def _dot_exact_lhs(a_bf16, b, n):
    acc = None
    for p in _split(b, n):
        t = jnp.dot(a_bf16, p, preferred_element_type=F32)
        acc = t if acc is None else acc + t
    return acc


def _head_sum(x, to_head, from_head):
    return _dot_exact_rhs(_dot_exact_rhs(x, to_head, 2), from_head, 3)


def _sigmoid(z):
    return 1.0 / (1.0 + jnp.exp(-z))


def _layer_norm(z, g, b):
    mu = jnp.mean(z, axis=-1, keepdims=True)
    zc = z - mu
    var = jnp.mean(zc * zc, axis=-1, keepdims=True)
    return zc * lax.rsqrt(var + LN_EPS) * g + b


def _const_spec(shape):
    nd = len(shape)
    return pl.BlockSpec(shape, lambda *_: (0,) * nd)


def _params(sem):
    return pltpu.CompilerParams(dimension_semantics=sem, vmem_limit_bytes=VMEM_LIMIT_BYTES)


def _rwkv_pre_kernel(x_ref, bnd_ref, mu_ref, vec_ref, wr_ref, wk_ref, wv_ref, w1_ref, w2_ref,
                     a1_ref, a2_ref, g1_ref, g2_ref, th_ref, fh_ref,
                     r_out, ld_out, k_out, v_out, kk_out, ka_out, g_out, bonus_out):
    x = x_ref[...]
    rows = lax.broadcasted_iota(jnp.int32, x.shape, 0)
    xp = jnp.where(rows == 0, bnd_ref[0, 0:1, :], pltpu.roll(x, 1, 0))
    dx = xp - x

    def mix(s):
        return x + dx * mu_ref[s:s + 1, :]

    w0 = vec_ref[0:1, :]
    a0 = vec_ref[1:2, :]
    k_k = vec_ref[2:3, :]
    k_a = vec_ref[3:4, :]
    r_k = vec_ref[4:5, :]
    to_head = th_ref[...]
    from_head = fh_ref[...]

    lw = _dot(mix(3), w1_ref[...])
    la = _dot(mix(4), a1_ref[...])
    lg = _dot(mix(5), g1_ref[...])
    r = _dot(mix(0), wr_ref[...])
    k = _dot(mix(1), wk_ref[...])
    zw = -(w0 + _dot(jnp.tanh(lw), w2_ref[...]))
    a = _sigmoid(a0 + _dot(la, a2_ref[...]))
    g = _dot(_sigmoid(lg), g2_ref[...])
    v = _dot(mix(2), wv_ref[...])
    softplus = jnp.maximum(zw, 0.0) + jnp.log(1.0 + jnp.exp(-jnp.abs(zw)))
    ld = -jnp.exp(-softplus - 0.5)

    kk = k * k_k
    k_h = k * (1.0 + (a - 1.0) * k_a)
    ss_h = _dot_exact_rhs(kk * kk, to_head, 2)
    rk_h = _dot_exact_rhs(r * k_h * r_k, to_head, 2)
    ss = _dot_exact_rhs(ss_h, from_head, 3)
    bonus = _dot_exact_rhs(rk_h, from_head, 3) * v
    kk = kk / jnp.maximum(jnp.sqrt(ss), 1e-12)

    r_out[...] = r
    ld_out[...] = ld
    k_out[...] = k_h
    v_out[...] = v
    kk_out[...] = kk
    ka_out[...] = kk * a
    g_out[...] = g
    bonus_out[...] = bonus


def _rwkv_pre(x, bnd, mu8, vec8, wr, wk, wv, w1, w2, a1, a2, g1, g2, to_head, from_head, tm):
    n, d = x.shape
    row = pl.BlockSpec((tm, d), lambda i: (i, 0))
    ins = [row, pl.BlockSpec((1, SUBLANES, d), lambda i: (i, 0, 0))]
    ins += [_const_spec(a.shape)
            for a in (mu8, vec8, wr, wk, wv, w1, w2, a1, a2, g1, g2, to_head, from_head)]
    return pl.pallas_call(
        _rwkv_pre_kernel,
        grid=(n // tm,),
        in_specs=ins,
        out_specs=[row] * 8,
        out_shape=[jax.ShapeDtypeStruct((n, d), F32)] * 8,
        compiler_params=_params(("parallel",)),
        name="rwkv_pre",
    )(x, bnd, mu8, vec8, wr, wk, wv, w1, w2, a1, a2, g1, g2, to_head, from_head)


def _wkv_scan_kernel(r_ref, ld_ref, k_ref, v_ref, kk_ref, ka_ref, s0_ref, o_ref, st_ref, s_scr,
                     *, chunk, n_chunks, group):
    c = chunk
    pw = 2 * HEAD_A
    n_pp = s_scr.shape[0]
    t_idx = pl.program_id(2)

    @pl.when(t_idx == 0)
    def _():
        zeros_hh = jnp.zeros((HEAD_A, HEAD_A), F32)
        for pp in range(n_pp):
            s_scr[pp] = jnp.concatenate(
                [jnp.concatenate([s0_ref[pp, 0], zeros_hh], axis=1),
                 jnp.concatenate([zeros_hh, s0_ref[pp, 1]], axis=1)], axis=0)

    ri = lax.broadcasted_iota(jnp.int32, (c, c), 0)
    ci = lax.broadcasted_iota(jnp.int32, (c, c), 1)
    tri_incl = ri >= ci
    cum_mat = jnp.where(tri_incl, 1.0, 0.0).astype(BF16)
    ri4 = lax.broadcasted_iota(jnp.int32, (c, 4 * c), 0)
    ci4 = lax.broadcasted_iota(jnp.int32, (c, 4 * c), 1) & (c - 1)
    strict4 = ri4 > ci4
    incl4 = ri4 >= ci4
    first_blk = lax.broadcasted_iota(jnp.int32, (1, 2 * c), 1) < c
    eye_cat = (lax.broadcasted_iota(jnp.int32, (c, 2 * c), 0)
               == (lax.broadcasted_iota(jnp.int32, (c, 2 * c), 1) & (c - 1)))
    lane1 = lax.broadcasted_iota(jnp.int32, (1, pw), 1)
    head_a1 = lane1 < HEAD_A
    lane2 = lax.broadcasted_iota(jnp.int32, (1, 2 * pw), 1) & (pw - 1)
    head_a2 = lane2 < HEAD_A
    rs = lax.broadcasted_iota(jnp.int32, (pw, pw), 0)
    cs = lax.broadcasted_iota(jnp.int32, (pw, pw), 1)
    same_head = (rs < HEAD_A) == (cs < HEAD_A)
    eye = rs == cs
    zeros_cv = jnp.zeros((c, pw), F32)

    def group_maps(slices):
        each = lambda fn, *lists: [fn(*a) for a in zip(*lists)]
        ld = [ld_ref[ix] for ix in slices]
        cw = each(lambda x: _dot_exact_lhs(cum_mat, x, 3), ld)
        w_in = each(jnp.exp, cw)
        w_ex = each(lambda a, b: jnp.exp(a - b), cw, ld)
        w_inv = each(lambda a: jnp.exp(-a), cw)
        w_last = each(lambda a: a[c - 1:c, :], w_in)
        knt = [-(kk_ref[ix] * w) for ix, w in zip(slices, w_ex)]
        kat = [ka_ref[ix] * w for ix, w in zip(slices, w_inv)]
        kt = [k_ref[ix] * w for ix, w in zip(slices, w_inv)]
        rt = [r_ref[ix] * w for ix, w in zip(slices, w_in)]
        v = [v_ref[ix] for ix in slices]
        by_head = lambda z, is_a: jnp.concatenate(
            [jnp.where(is_a, z, 0.0), jnp.where(is_a, 0.0, z)], axis=0)
        lh = each(lambda a, b: jnp.concatenate([a, b], axis=0), knt, rt)
        rh = each(lambda a, b: jnp.concatenate(
            [by_head(a, head_a1), by_head(b, head_a1)], axis=0), kat, kt)
        v_st = each(lambda a: by_head(a, head_a1), v)

        full = each(_dot_nt, lh, rh)
        top = each(lambda a: jnp.where(strict4, a[:c, :], 0.0), full)
        bot = each(lambda a: jnp.where(incl4, a[c:, :], 0.0), full)
        n_cat = each(lambda a: a[:, :2 * c], top)
        akv = each(lambda a, b: _dot(a[:, 2 * c:], b), top, v_st)

        t_cat = each(lambda n: jnp.where(eye_cat, 1.0, 0.0) + n, n_cat)
        span = 2
        while span < c:
            n_cat = each(lambda n: _dot(n, by_head(n, first_blk)), n_cat)
            dt = each(lambda n, t0: _dot(n, by_head(t0, first_blk)), n_cat, t_cat)
            t_cat = each(lambda t0, d: t0 + d, t_cat, dt)
            span *= 2
        x = each(lambda t0, a, b: _dot(t0, by_head(jnp.concatenate([a, b], axis=1), head_a2)),
                 t_cat, knt, akv)

        v_wide = each(lambda a: jnp.concatenate([jnp.zeros_like(a), a], axis=1), v_st)
        qo_all = each(lambda b4, x0, vw: _dot(b4, jnp.concatenate([by_head(x0, head_a2), vw], axis=0)),
                      bot, x, v_wide)
        rhs2 = each(lambda a, b: jnp.concatenate(
            [a, jnp.concatenate([zeros_cv, b], axis=1)], axis=0), x, v)
        lt = each(lambda a, b, w: jnp.concatenate([a * w, b * w], axis=0), kat, kt, w_last)
        mb = each(_dot_tn, lt, rhs2)
        out = []
        for j in range(len(slices)):
            qo = qo_all[j]
            q = rt[j] + qo[:, :pw]
            m = jnp.where(eye, w_last[j], 0.0) + jnp.where(same_head, mb[j][:, :pw], 0.0)
            b = jnp.where(same_head, mb[j][:, pw:], 0.0)
            mq = jnp.concatenate([m, q], axis=0)
            out.append((mq.astype(BF16), b, qo[:, pw:]))
        return out

    def body(it, carry):
        items = [(j, pp) for j in range(group) for pp in range(n_pp)]
        slices = [(pl.ds(pl.multiple_of((it * group + j) * c, c), c), pl.ds(pp * pw, pw))
                  for j, pp in items]
        maps = group_maps(slices)
        s = [s_scr[pp] for pp in range(n_pp)]
        for (j, pp), ix, (mq, b, o0) in zip(items, slices, maps):
            res = _dot(mq, s[pp])
            s[pp] = res[:pw, :] + b
            o_ref[ix] = res[pw:, :] + o0
        for pp in range(n_pp):
            s_scr[pp] = s[pp]
        return carry

    lax.fori_loop(0, n_chunks // group, body, 0)

    @pl.when(t_idx == pl.num_programs(2) - 1)
    def _():
        for pp in range(n_pp):
            s = s_scr[pp]
            st_ref[pp, 0] = s[:HEAD_A, :HEAD_A]
            st_ref[pp, 1] = s[HEAD_A:, HEAD_A:]


def _wkv_scan(r, ld, k, v, kk, ka, s0, chunk, t_blk):
    bn, t, d = r.shape
    pw = 2 * HEAD_A
    n_pairs = d // pw
    n_chunks = t_blk // chunk
    n_pp = SCAN_PAIRS if n_chunks >= SCAN_GROUP else n_pairs
    seq = pl.BlockSpec((None, t_blk, n_pp * pw), lambda b, p, i: (b, i, p))
    st = pl.BlockSpec((None, n_pp, 2, HEAD_A, HEAD_A), lambda b, p, i: (b, p, 0, 0, 0))
    kern = functools.partial(_wkv_scan_kernel, chunk=chunk, n_chunks=n_chunks,
                             group=min(SCAN_GROUP, n_chunks))
    return pl.pallas_call(
        kern,
        grid=(bn, n_pairs // n_pp, t // t_blk),
        in_specs=[seq] * 6 + [st],
        out_specs=[seq, st],
        out_shape=[jax.ShapeDtypeStruct((bn, t, d), F32),
                   jax.ShapeDtypeStruct((bn, n_pairs, 2, HEAD_A, HEAD_A), F32)],
        scratch_shapes=[pltpu.VMEM((n_pp, pw, pw), F32)],
        compiler_params=_params(("parallel", "parallel", "arbitrary")),
        name="wkv_scan",
    )(r, ld, k, v, kk, ka, s0)


def _rwkv_post_kernel(o_ref, bonus_ref, g_ref, x_ref, vec_ref, th_ref, fh_ref, wo_ref, out_ref):
    to_head = th_ref[...]
    from_head = fh_ref[...]
    inv_n = 1.0 / HEAD_A
    tm = o_ref.shape[0]
    n_blk = POST_ROW_BLOCKS if tm % (POST_ROW_BLOCKS * SUBLANES) == 0 else 1
    blk = tm // n_blk
    sls = [pl.ds(b * blk, blk) for b in range(n_blk)]
    o = [o_ref[sl, :] for sl in sls]
    mean_h = [_dot_exact_rhs(a, to_head, 2) for a in o]
    mean = [_dot_exact_rhs(a, from_head, 3) * inv_n for a in mean_h]
    oc = [a - m for a, m in zip(o, mean)]
    var_h = [_dot_exact_rhs(a * a, to_head, 2) for a in oc]
    var = [_dot_exact_rhs(a, from_head, 3) * inv_n for a in var_h]
    y = [(c * lax.rsqrt(vr + GN_EPS) * vec_ref[0:1, :] + vec_ref[1:2, :] + bonus_ref[sl, :]) * g_ref[sl, :]
         for c, vr, sl in zip(oc, var, sls)]
    h = [_dot(a, wo_ref[...]) for a in y]
    for a, sl in zip(h, sls):
        out_ref[sl, :] = _layer_norm(ALPHA * x_ref[sl, :] + a, vec_ref[2:3, :], vec_ref[3:4, :])


def _rwkv_post(o, bonus, g, x, vec8, to_head, from_head, wo, tm):
    n, d = x.shape
    row = pl.BlockSpec((tm, d), lambda i: (i, 0))
    return pl.pallas_call(
        _rwkv_post_kernel,
        grid=(n // tm,),
        in_specs=[row] * 4 + [_const_spec(a.shape) for a in (vec8, to_head, from_head, wo)],
        out_specs=row,
        out_shape=jax.ShapeDtypeStruct((n, d), F32),
        compiler_params=_params(("parallel",)),
        name="rwkv_post",
    )(o, bonus, g, x, vec8, to_head, from_head, wo)


def _ffn_kernel(x_ref, wg_ref, wu_ref, wd_ref, ln_ref, out_ref, acc_ref, xb_ref):
    f = pl.program_id(1)

    @pl.when(f == 0)
    def _():
        acc_ref[...] = jnp.zeros_like(acc_ref)
        xb_ref[...] = x_ref[...].astype(BF16)

    xb = xb_ref[...]
    gate = jnp.dot(xb, wg_ref[...].astype(BF16), preferred_element_type=F32)
    up = jnp.dot(xb, wu_ref[...].astype(BF16), preferred_element_type=F32)
    h = gate * _sigmoid(gate) * up
    acc_ref[...] += jnp.dot(h.astype(BF16), wd_ref[...].astype(BF16), preferred_element_type=F32)

    @pl.when(f == pl.num_programs(1) - 1)
    def _():
        out_ref[...] = _layer_norm(ALPHA * x_ref[...] + acc_ref[...], ln_ref[0:1, :], ln_ref[1:2, :])


def _ffn(x, w_gu, w_down, ln8, tm, tf):
    n, d = x.shape
    d_ff = w_gu.shape[1] // 2
    nf = d_ff // tf
    return pl.pallas_call(
        _ffn_kernel,
        grid=(n // tm, nf),
        in_specs=[
            pl.BlockSpec((tm, d), lambda i, f: (i, 0)),
            pl.BlockSpec((d, tf), lambda i, f: (0, f)),
            pl.BlockSpec((d, tf), lambda i, f: (0, nf + f)),
            pl.BlockSpec((tf, d), lambda i, f: (f, 0)),
            _const_spec(ln8.shape),
        ],
        out_specs=pl.BlockSpec((tm, d), lambda i, f: (i, 0)),
        out_shape=jax.ShapeDtypeStruct((n, d), F32),
        scratch_shapes=[pltpu.VMEM((tm, d), F32), pltpu.VMEM((tm, d), BF16)],
        compiler_params=_params(("parallel", "arbitrary")),
        name="dense_ffn",
    )(x, w_gu, w_gu, w_down, ln8)


ROUTE_I1, ROUTE_I2, ROUTE_G1, ROUTE_G2, ROUTE_Q1, ROUTE_Q2 = range(6)
META_CARRY, META_COUNT = 0, 1


def _proj_ln_router_kernel(y_ref, x_ref, w_ref, ln_ref, rw_ref, out_ref, route_ref, meta_ref, count_ref,
                           carry_ref):
    h = _dot(y_ref[...], w_ref[...])
    x3 = _layer_norm(ALPHA * x_ref[...] + h, ln_ref[0:1, :], ln_ref[1:2, :])
    out_ref[...] = x3
    _route_rows(x3, rw_ref, route_ref, meta_ref, count_ref, carry_ref)


def _route_rows(x, w_ref, route_ref, meta_ref, count_ref, carry_ref):
    i = pl.program_id(0)

    @pl.when(i == 0)
    def _():
        carry_ref[...] = jnp.zeros_like(carry_ref)

    x_hi, x_lo = _split(x, 2)
    w_hi, w_lo = _split(w_ref[...], 2)
    logits = (jnp.dot(x_hi, w_hi, preferred_element_type=F32)
              + jnp.dot(x_lo, w_hi, preferred_element_type=F32)
              + jnp.dot(x_hi, w_lo, preferred_element_type=F32))
    tm = logits.shape[0]
    lane = lax.broadcasted_iota(jnp.int32, logits.shape, 1).astype(F32)
    neg = -jnp.inf
    logits = jnp.where(lane < N_EXPERTS, logits, neg)
    m1 = jnp.max(logits, axis=-1, keepdims=True)
    i1 = jnp.min(jnp.where(logits == m1, lane, float(LANES)), axis=-1, keepdims=True)
    rest = jnp.where(lane == i1, neg, logits)
    m2 = jnp.max(rest, axis=-1, keepdims=True)
    i2 = jnp.min(jnp.where(rest == m2, lane, float(LANES)), axis=-1, keepdims=True)
    e2 = jnp.exp(m2 - m1)
    den = 1.0 + e2

    sel1 = lane == i1
    sel2 = lane == i2
    onehot = jnp.where(sel1, 1.0, 0.0) + jnp.where(sel2, 1.0, 0.0)
    ri = lax.broadcasted_iota(jnp.int32, (tm, tm), 0)
    ci = lax.broadcasted_iota(jnp.int32, (tm, tm), 1)
    earlier = jnp.where(ri > ci, 1.0, 0.0).astype(BF16)
    in_tile = jnp.dot(earlier, onehot.astype(BF16), preferred_element_type=F32)
    q1 = jnp.sum(jnp.where(sel1, in_tile, 0.0), axis=-1, keepdims=True)
    q2 = jnp.sum(jnp.where(sel2, in_tile, 0.0), axis=-1, keepdims=True)
    tile_count = jnp.sum(onehot, axis=0, keepdims=True)

    route = jnp.zeros_like(logits)
    for col, val in ((ROUTE_I1, i1), (ROUTE_I2, i2), (ROUTE_G1, 1.0 / den), (ROUTE_G2, e2 / den),
                     (ROUTE_Q1, q1), (ROUTE_Q2, q2)):
        route = jnp.where(lane == float(col), val, route)
    route_ref[...] = route
    meta_row = lax.broadcasted_iota(jnp.int32, meta_ref.shape, 0)
    meta_ref[...] = jnp.where(meta_row == META_CARRY, carry_ref[0:1, :],
                              jnp.where(meta_row == META_COUNT, tile_count, 0.0))
    slab_rows = jnp.floor((tile_count + (SLAB_UNIT - 1)) * (1.0 / SLAB_UNIT)) * SLAB_UNIT
    carry_ref[0:1, :] = carry_ref[0:1, :] + slab_rows
    count_ref[...] = carry_ref[...]


def _proj_ln_router(y, x, w, ln8, router_pad, tm):
    n, d = x.shape
    row = pl.BlockSpec((tm, d), lambda i: (i, 0))
    return pl.pallas_call(
        _proj_ln_router_kernel,
        grid=(n // tm,),
        in_specs=[pl.BlockSpec((tm, y.shape[1]), lambda i: (i, 0)), row,
                  _const_spec(w.shape), _const_spec(ln8.shape), _const_spec(router_pad.shape)],
        out_specs=[row,
                   pl.BlockSpec((tm, LANES), lambda i: (i, 0)),
                   pl.BlockSpec((None, SUBLANES, LANES), lambda i: (i, 0, 0)),
                   _const_spec((SUBLANES, LANES))],
        out_shape=[jax.ShapeDtypeStruct((n, d), F32),
                   jax.ShapeDtypeStruct((n, LANES), F32),
                   jax.ShapeDtypeStruct((n // tm, SUBLANES, LANES), F32),
                   jax.ShapeDtypeStruct((SUBLANES, LANES), F32)],
        scratch_shapes=[pltpu.VMEM((SUBLANES, LANES), F32)],
        compiler_params=_params(("arbitrary",)),
        name="proj_ln_router",
    )(y, x, w, ln8, router_pad)


def _slab_plan(meta_ref):
    plan = []
    off = 0
    for e in range(N_EXPERTS):
        units = (meta_ref[0, e] + (SLAB_UNIT - 1)) // SLAB_UNIT
        plan.append((units, meta_ref[0, N_EXPERTS + e], off))
        off = off + units * SLAB_UNIT
    return plan


def _slab_buffer_rows(tm):
    worst = 2 * tm + N_EXPERTS * (SLAB_UNIT - 1)
    return ((worst + LANES - 1) // LANES) * LANES


def _slab_rows(expert, rank, plan):
    off = jnp.zeros_like(expert)
    for e, (_, _, e_off) in enumerate(plan):
        off = jnp.where(expert == e, e_off, off)
    return off + rank


def _slab_copies(plan, make_copy):
    total = 0
    for units, first_row, off in plan:
        def start(u, carry, first_row=first_row, off=off):
            make_copy(pl.multiple_of(off + u * SLAB_UNIT, SLAB_UNIT),
                      pl.multiple_of(first_row + u * SLAB_UNIT, SLAB_UNIT)).start()
            return carry
        lax.fori_loop(0, units, start, 0)
        total = total + units
    return total


def _dispatch_kernel(meta_ref, fill_ref, route_ref, x_ref, *rest):
    xs_ref, cbuf, zrows, sem = rest[-4:]
    tm = x_ref.shape[0]
    plan = _slab_plan(meta_ref)

    @pl.when(pl.program_id(0) == 0)
    def _():
        zrows[...] = jnp.zeros_like(zrows)

        def zero_copy(row, n_rows):
            return pltpu.make_async_copy(zrows.at[pl.ds(0, n_rows), :], xs_ref.at[pl.ds(row, n_rows), :],
                                         sem.at[1])

        def zero_range(first, pieces, n_rows):
            def start(u, carry):
                zero_copy(pl.multiple_of(first + u * n_rows, n_rows), n_rows).start()
                return carry

            def wait(u, carry):
                zero_copy(0, n_rows).wait()
                return carry

            lax.fori_loop(0, pieces, start, 0)
            lax.fori_loop(0, pieces, wait, 0)

        for e in range(N_EXPERTS):
            zero_range(fill_ref[0, e], fill_ref[0, N_EXPERTS + e], SLAB_UNIT)
        zero_range(fill_ref[0, 2 * N_EXPERTS], fill_ref[0, 2 * N_EXPERTS + 1], zrows.shape[0])

    sel = (lax.broadcasted_iota(jnp.int32, (SUBLANES, LANES), 0)
           == lax.broadcasted_iota(jnp.int32, (SUBLANES, LANES), 1)).astype(BF16)
    route_t = None
    for piece in _split(route_ref[...], 3):
        t = lax.dot_general(sel, piece, (((1,), (1,)), ((), ())), preferred_element_type=F32)
        route_t = t if route_t is None else route_t + t
    route_t = route_t.astype(jnp.int32)
    row1 = _slab_rows(route_t[ROUTE_I1:ROUTE_I1 + 1, :], route_t[ROUTE_Q1:ROUTE_Q1 + 1, :], plan)
    row2 = _slab_rows(route_t[ROUTE_I2:ROUTE_I2 + 1, :], route_t[ROUTE_Q2:ROUTE_Q2 + 1, :], plan)

    r_iota = lax.broadcasted_iota(jnp.int32, (cbuf.shape[0], tm), 0)
    perm = jnp.where((r_iota == row1) | (r_iota == row2), 1.0, 0.0).astype(BF16)
    cbuf[...] = jnp.dot(perm, x_ref[...].astype(BF16), preferred_element_type=F32)

    def make_copy(buf_row, sorted_row):
        return pltpu.make_async_copy(cbuf.at[pl.ds(buf_row, SLAB_UNIT), :],
                                     xs_ref.at[pl.ds(sorted_row, SLAB_UNIT), :], sem.at[0])

    n_started = _slab_copies(plan, make_copy)

    def wait(u, carry):
        make_copy(0, 0).wait()
        return carry

    lax.fori_loop(0, n_started, wait, 0)


def _dispatch(meta, fill, route, x, xs, xs_rows, tm):
    n, d = x.shape
    smem = lambda a, imap: pl.BlockSpec((None,) + a.shape[1:], imap, memory_space=pltpu.SMEM)
    in_specs = [smem(meta, lambda i: (i, 0, 0)), smem(fill, lambda i: (0, 0, 0)),
                pl.BlockSpec((tm, LANES), lambda i: (i, 0)),
                pl.BlockSpec((tm, d), lambda i: (i, 0))]
    args = [meta, fill, route, x]
    aliases = {}
    if xs is not None:
        in_specs.append(pl.BlockSpec(memory_space=pl.ANY))
        args.append(xs)
        aliases = {len(args) - 1: 0}
    return pl.pallas_call(
        _dispatch_kernel,
        grid=(n // tm,),
        in_specs=in_specs,
        out_specs=pl.BlockSpec(memory_space=pl.ANY),
        out_shape=jax.ShapeDtypeStruct((xs_rows, d), F32),
        scratch_shapes=[pltpu.VMEM((_slab_buffer_rows(tm), d), F32), pltpu.VMEM((ZERO_ROWS, d), F32),
                        pltpu.SemaphoreType.DMA((2,))],
        input_output_aliases=aliases,
        compiler_params=_params(("arbitrary",)),
        name="moe_dispatch",
    )(*args)


def _experts_kernel(te_ref, rows_ref, x_ref, wg_ref, wu_ref, wd_ref, out_ref, acc_ref, xb_ref):
    del te_ref
    f = pl.program_id(1)
    rows = rows_ref[pl.program_id(0)]
    tm = x_ref.shape[0]

    @pl.when(f == 0)
    def _():
        acc_ref[...] = jnp.zeros_like(acc_ref)

    def swiglu_rows(n_rows):
        sl = pl.ds(0, n_rows)

        @pl.when(f == 0)
        def _():
            xb_ref[sl, :] = x_ref[sl, :].astype(BF16)

        xb = xb_ref[sl, :]
        gate = jnp.dot(xb, wg_ref[...].astype(BF16), preferred_element_type=F32)
        up = jnp.dot(xb, wu_ref[...].astype(BF16), preferred_element_type=F32)
        h = gate * _sigmoid(gate) * up
        acc_ref[sl, :] += jnp.dot(h.astype(BF16), wd_ref[...].astype(BF16), preferred_element_type=F32)

    part = tm // MOE_TILE_PARTS
    for k in range(1, MOE_TILE_PARTS + 1):
        @pl.when(jnp.logical_and(rows > (k - 1) * part, rows <= k * part))
        def _(k=k):
            swiglu_rows(k * part)

    @pl.when(f == pl.num_programs(1) - 1)
    def _():
        out_ref[...] = acc_ref[...]


def _experts(tile_expert, tile_rows, xs, w_gu, w_down, tm, tf):
    s_total, d = xs.shape
    d_ff = w_gu.shape[2] // 2
    nf = d_ff // tf

    def f_eff(i, f, rows):
        return jnp.where(rows[i] > 0, f, nf - 1)

    grid_spec = pltpu.PrefetchScalarGridSpec(
        num_scalar_prefetch=2,
        grid=(s_total // tm, nf),
        in_specs=[
            pl.BlockSpec((tm, d), lambda i, f, te, rows: (jnp.where(rows[i] > 0, i, 0), 0)),
            pl.BlockSpec((None, d, tf), lambda i, f, te, rows: (te[i], 0, f_eff(i, f, rows))),
            pl.BlockSpec((None, d, tf), lambda i, f, te, rows: (te[i], 0, nf + f_eff(i, f, rows))),
            pl.BlockSpec((None, tf, d), lambda i, f, te, rows: (te[i], f_eff(i, f, rows), 0)),
        ],
        out_specs=pl.BlockSpec((tm, d), lambda i, f, te, rows: (i, 0)),
        scratch_shapes=[pltpu.VMEM((tm, d), F32), pltpu.VMEM((tm, d), BF16)],
    )
    return pl.pallas_call(
        _experts_kernel,
        grid_spec=grid_spec,
        out_shape=jax.ShapeDtypeStruct((s_total, d), F32),
        compiler_params=_params(("arbitrary", "arbitrary")),
        name="moe_experts",
    )(tile_expert, tile_rows, xs, w_gu, w_gu, w_down)


def _combine_kernel(meta_ref, meta_next_ref, route_ref, x_ref, ys_ref, ln_ref, out_ref, cbuf, sem):
    tm = x_ref.shape[0]
    i = pl.program_id(0)
    slot = i & 1
    plan = _slab_plan(meta_ref)

    def gather(tile_plan, dst_slot):
        def make_copy(buf_row, sorted_row):
            return pltpu.make_async_copy(ys_ref.at[pl.ds(sorted_row, SLAB_UNIT), :],
                                         cbuf.at[dst_slot, pl.ds(buf_row, SLAB_UNIT), :],
                                         sem.at[dst_slot])
        return make_copy, lambda: _slab_copies(tile_plan, make_copy)

    @pl.when(i == 0)
    def _():
        cbuf[...] = jnp.zeros_like(cbuf)
        gather(plan, 0)[1]()

    @pl.when(i + 1 < pl.num_programs(0))
    def _():
        gather(_slab_plan(meta_next_ref), 1 - slot)[1]()

    make_copy = gather(plan, slot)[0]
    n_mine = sum(units for units, _, _ in plan)

    def wait(u, carry):
        make_copy(0, 0).wait()
        return carry

    lax.fori_loop(0, n_mine, wait, 0)

    route = route_ref[...]
    col = lambda k: route[:, k:k + 1]
    row1 = _slab_rows(col(ROUTE_I1).astype(jnp.int32), col(ROUTE_Q1).astype(jnp.int32), plan)
    row2 = _slab_rows(col(ROUTE_I2).astype(jnp.int32), col(ROUTE_Q2).astype(jnp.int32), plan)
    c_iota = lax.broadcasted_iota(jnp.int32, (tm, cbuf.shape[1]), 1)
    gate = (jnp.where(c_iota == row1, col(ROUTE_G1), 0.0)
            + jnp.where(c_iota == row2, col(ROUTE_G2), 0.0))
    g_hi, g_lo = _split(gate, 2)
    c_hi, c_lo = _split(cbuf[slot], 2)
    y = (jnp.dot(g_hi, c_hi, preferred_element_type=F32)
         + jnp.dot(g_hi, c_lo, preferred_element_type=F32)
         + jnp.dot(g_lo, c_hi, preferred_element_type=F32))
    out_ref[...] = _layer_norm(ALPHA * x_ref[...] + y, ln_ref[0:1, :], ln_ref[1:2, :])


def _combine(meta, route, x, ys, ln8, tm):
    n, d = x.shape
    last = n // tm - 1
    meta_spec = lambda shift: pl.BlockSpec((None, 1, 2 * N_EXPERTS),
                                           lambda i: (jnp.minimum(i + shift, last), 0, 0),
                                           memory_space=pltpu.SMEM)
    return pl.pallas_call(
        _combine_kernel,
        grid=(n // tm,),
        in_specs=[meta_spec(0), meta_spec(1),
                  pl.BlockSpec((tm, LANES), lambda i: (i, 0)),
                  pl.BlockSpec((tm, d), lambda i: (i, 0)),
                  pl.BlockSpec(memory_space=pl.ANY),
                  _const_spec(ln8.shape)],
        out_specs=pl.BlockSpec((tm, d), lambda i: (i, 0)),
        out_shape=jax.ShapeDtypeStruct((n, d), F32),
        scratch_shapes=[pltpu.VMEM((2, _slab_buffer_rows(tm), d), F32), pltpu.SemaphoreType.DMA((2,))],
        compiler_params=_params(("arbitrary",)),
        name="moe_combine",
    )(meta, meta, route, x, ys, ln8)


def _rope(y, cos_t, sin_next, sin_prev):
    n = y.shape[1]
    reps = n // LANES
    tile = lambda t: jnp.concatenate([t] * reps, axis=1) if reps > 1 else t
    half = ROPE_DIM // 2
    return (y * tile(cos_t)
            + pltpu.roll(y, n - half, 1) * tile(sin_next)
            + pltpu.roll(y, half, 1) * tile(sin_prev))


def _qkv_kernel(x_ref, wq_ref, wkv_ref, cos_ref, sn_ref, sp_ref, q_out, k_out, v_out):
    xb = x_ref[...].astype(BF16)
    tables = (cos_ref[...], sn_ref[...], sp_ref[...])
    q_out[...] = _rope(jnp.dot(xb, wq_ref[...], preferred_element_type=F32), *tables)
    kv = jnp.dot(xb, wkv_ref[...], preferred_element_type=F32)
    nk = k_out.shape[1]
    k_out[...] = _rope(kv[:, :nk], *tables)
    v_out[...] = kv[:, nk:]


def _qkv_proj(x, wq, wkv, tables, tm):
    n, d = x.shape
    kvw = wkv.shape[1] // 2
    t_tiles = tables[0].shape[0] // tm
    tab = pl.BlockSpec((tm, LANES), lambda i: (i % t_tiles, 0))
    widths = (wq.shape[1], kvw, kvw)
    return pl.pallas_call(
        _qkv_kernel,
        grid=(n // tm,),
        in_specs=[pl.BlockSpec((tm, d), lambda i: (i, 0)), _const_spec(wq.shape), _const_spec(wkv.shape),
                  tab, tab, tab],
        out_specs=[pl.BlockSpec((tm, ow), lambda i: (i, 0)) for ow in widths],
        out_shape=[jax.ShapeDtypeStruct((n, ow), F32) for ow in widths],
        compiler_params=_params(("parallel",)),
        name="qkv_proj",
    )(x, wq, wkv, *tables)


def _attn_kernel(sink_ref, q_ref, kp_ref, kc_ref, vp_ref, vc_ref, o_ref, *, banded):
    tq = q_ref.shape[0]
    n_prev = kp_ref.shape[0]
    tk = n_prev + kc_ref.shape[0]
    q = q_ref[...] * ATTN_SCALE
    kband = jnp.concatenate([kp_ref[...], kc_ref[...]], axis=0)
    vband = jnp.concatenate([vp_ref[...], vc_ref[...]], axis=0)
    n_heads = q.shape[1] // HEAD_B
    group = n_heads // KV_HEADS
    qc = CHUNK if banded else tq
    kc = WINDOW + CHUNK if banded else tk
    units = [(kh, ci) for kh in range(KV_HEADS) for ci in range(tq // qc)]
    if banded:
        kj = lax.broadcasted_iota(jnp.int32, (kc, group * qc), 0)
        band_start = pl.program_id(1) * tq - n_prev

    qlane = lax.broadcasted_iota(jnp.int32, (1, group * qc), 1)

    def scores(kh, ci):
        qs = jnp.concatenate([q[ci * qc:(ci + 1) * qc, (kh * group + j) * HEAD_B:(kh * group + j + 1) * HEAD_B]
                              for j in range(group)], axis=0)
        return _dot_nt(kband[ci * qc:ci * qc + kc, kh * HEAD_B:(kh + 1) * HEAD_B], qs)

    def softmax(kh, ci, s):
        sk = jnp.full((1, group * qc), sink_ref[kh * group], F32)
        for j in range(1, group):
            sk = jnp.where(qlane >= j * qc, sink_ref[kh * group + j], sk)
        if banded:
            s = jnp.where(band_start + ci * qc + kj >= 0, s, -jnp.inf)
        m = jnp.maximum(jnp.max(s, axis=0, keepdims=True), sk)
        p = jnp.exp(s - m)
        return p * (1.0 / (jnp.sum(p, axis=0, keepdims=True) + jnp.exp(sk - m)))

    def weighted(kh, ci, p):
        return _dot_tn(p, vband[ci * qc:ci * qc + kc, kh * HEAD_B:(kh + 1) * HEAD_B])

    chunks = range(tq // qc)
    outs = {}
    s_next = [scores(0, ci) for ci in chunks]
    for kh in range(KV_HEADS):
        s_cur = s_next
        if kh + 1 < KV_HEADS:
            s_next = [scores(kh + 1, ci) for ci in chunks]
        probs = [softmax(kh, ci, s_cur[ci]) for ci in chunks]
        for ci in chunks:
            o = weighted(kh, ci, probs[ci])
            for j in range(group):
                outs[(kh * group + j, ci)] = o[j * qc:(j + 1) * qc, :]
    o_ref[...] = jnp.concatenate(
        [jnp.concatenate([outs[(h, ci)] for ci in range(tq // qc)], axis=0) for h in range(n_heads)],
        axis=1)


def _attention(q, k_prev_src, k_cur_src, v_prev_src, v_cur_src, sinks, tq, banded):
    bn, t, d = q.shape
    kw = k_cur_src.shape[2]
    if banded:
        ratio = tq // WINDOW
        prev_map = lambda b, i: (b, jnp.maximum(i * ratio - 1, 0), 0)
    else:
        prev_map = lambda b, i: (b, 0, 0)
    prev = pl.BlockSpec((None, WINDOW, kw), prev_map)
    cur = pl.BlockSpec((None, tq, kw), lambda b, i: (b, i, 0))
    kern = functools.partial(_attn_kernel, banded=banded)
    return pl.pallas_call(
        kern,
        grid=(bn, t // tq),
        in_specs=[pl.BlockSpec(memory_space=pltpu.SMEM),
                  pl.BlockSpec((None, tq, d), lambda b, i: (b, i, 0)), prev, cur, prev, cur],
        out_specs=pl.BlockSpec((None, tq, d), lambda b, i: (b, i, 0)),
        out_shape=jax.ShapeDtypeStruct((bn, t, d), F32),
        compiler_params=_params(("parallel", "parallel")),
        name="swa_attn",
    )(sinks, q, k_prev_src, k_cur_src, v_prev_src, v_cur_src)


def _moe_layer(groups, P):
    xs_rows = [g[0] for g in groups]
    routed = [g[1:] for g in groups]
    d = xs_rows[0].shape[1]
    tm_e = MOE_TILE
    counts = [c[0, :N_EXPERTS].astype(jnp.int32) for _, _, c in routed]
    total = sum(counts)
    padded = ((total + tm_e - 1) // tm_e) * tm_e
    ends = jnp.cumsum(padded)
    starts = ends - padded
    n_slabs = sum(-(-x.shape[0] // MOE_ROW_TILE) for x in xs_rows) * N_EXPERTS
    n_assign = 2 * sum(x.shape[0] for x in xs_rows) + n_slabs * (SLAB_UNIT - 1)
    n_tiles = (n_assign + N_EXPERTS * (tm_e - 1)) // tm_e
    n_used = (ends[-1] // tm_e).astype(jnp.int32)
    tile_expert = jnp.sum((jnp.arange(n_tiles) * tm_e)[:, None] >= ends[None, :], axis=1).astype(jnp.int32)
    tile_expert = jnp.minimum(tile_expert, N_EXPERTS - 1)
    tile_expert = jnp.where(jnp.arange(n_tiles) < n_used, tile_expert,
                            tile_expert[jnp.maximum(n_used - 1, 0)])
    tile_rows = jnp.clip((starts + total)[tile_expert] - jnp.arange(n_tiles) * tm_e, 0, tm_e)
    tile_rows = jnp.where(jnp.arange(n_tiles) < n_used, tile_rows, 0).astype(jnp.int32)

    metas = []
    base = starts
    for (_, meta, _), cnt in zip(routed, counts):
        carry = meta[:, META_CARRY, :N_EXPERTS].astype(jnp.int32)
        sent = meta[:, META_COUNT, :N_EXPERTS].astype(jnp.int32)
        metas.append(jnp.concatenate([sent, base[None, :] + carry], axis=1)[:, None, :])
        base = base + cnt

    first_end = starts + counts[0]
    tail = jnp.stack([ends[-1], (n_tiles * tm_e - ends[-1]) // ZERO_ROWS])
    fill = jnp.concatenate([first_end, (ends - first_end) // SLAB_UNIT, tail]).astype(jnp.int32)
    xs = None
    for m, (route, _, _), x in zip(metas, routed, xs_rows):
        xs = _dispatch(m, fill.reshape(1, 1, -1), route, x, xs, n_tiles * tm_e,
                       min(MOE_ROW_TILE, x.shape[0]))
        fill = jnp.zeros_like(fill)
    ys = _experts(tile_expert, tile_rows, xs, P['moe_w_gu'][0], P['moe_w_down'][0], tm_e, FFN_COLS)
    ln11 = _pad_rows([P['ln_g'][1, 1], P['ln_b'][1, 1]], d)
    return [_combine(m, route, x, ys, ln11, min(MOE_ROW_TILE, x.shape[0]))
            for m, (route, _, _), x in zip(metas, routed, xs_rows)]


def _pad_rows(rows, d):
    a = jnp.stack(rows).astype(F32)
    return jnp.concatenate([a, jnp.zeros((SUBLANES - a.shape[0], d), F32)], axis=0)


def _rope_tables(pos, reps):
    inv_freq = ROPE_THETA ** (-jnp.arange(0, ROPE_DIM, 2, dtype=jnp.float32) / ROPE_DIM)
    ang = pos.astype(jnp.float32)[:, None] * inv_freq[None, :]
    cos = jnp.cos(ang)
    sin = jnp.sin(ang)
    t = pos.shape[0]
    half = ROPE_DIM // 2
    rest = HEAD_B - ROPE_DIM
    z_half = jnp.zeros((t, half), F32)
    z_rest = jnp.zeros((t, rest), F32)
    cos_h = jnp.concatenate([cos, cos, jnp.ones((t, rest), F32)], axis=1)
    sn_h = jnp.concatenate([-sin, z_half, z_rest], axis=1)
    sp_h = jnp.concatenate([z_half, sin, z_rest], axis=1)
    per_tile = LANES // HEAD_B
    return tuple(jnp.tile(a, (reps, per_tile)) for a in (cos_h, sn_h, sp_h))


def _trunk(x, shift_in, wkv_in, k_cache, v_cache, pos0, P):
    bn, t, d = x.shape
    n = bn * t
    h_a = d // HEAD_A
    pw = 2 * HEAD_A
    n_pairs = d // pw
    xf = x.reshape(n, d)

    tm_pre = min(256, t)
    tm_row = min(256, n)
    tm_ffn = min(1024, n)
    chunk = min(CHUNK, t)
    t_blk = min(1024, t)

    tiles = jnp.arange(n // tm_pre) * tm_pre
    prev_rows = xf[jnp.maximum(tiles - 1, 0)]
    start_rows = shift_in[0][tiles // t]
    bnd = jnp.where(((tiles % t) == 0)[:, None], start_rows, prev_rows)
    bnd = jnp.broadcast_to(bnd[:, None, :], (n // tm_pre, SUBLANES, d))

    mu8 = jnp.concatenate([P['a_mu'][0], jnp.zeros((2, d), F32)], axis=0)
    vec_pre = _pad_rows([P['a_w0'][0], P['a_a0'][0], P['a_k_k'][0], P['a_k_a'][0],
                         P['a_r_k'][0].reshape(d)], d)
    lane_head = jnp.arange(d) // HEAD_A
    to_head = (lane_head[:, None] == jnp.arange(LANES)[None, :]).astype(BF16)
    from_head = to_head.T
    bf = lambda a: a.astype(BF16)
    w_rkv = P['a_w_rkv'][0]
    r, ld, k_h, v, kk, ka, g, bonus = _rwkv_pre(
        xf, bnd, mu8, vec_pre, bf(w_rkv[0]), bf(w_rkv[1]), bf(w_rkv[2]),
        bf(P['a_w1'][0]), bf(P['a_w2'][0]), bf(P['a_a1'][0]), bf(P['a_a2'][0]),
        bf(P['a_g1'][0]), bf(P['a_g2'][0]), to_head, from_head, tm_pre)

    s0 = jnp.swapaxes(wkv_in[0].astype(F32), -1, -2).reshape(bn, n_pairs, 2, HEAD_A, HEAD_A)
    seq3 = lambda a: a.reshape(bn, t, d)
    o, s_fin = _wkv_scan(seq3(r), seq3(ld), seq3(k_h), seq3(v), seq3(kk), seq3(ka), s0, chunk, t_blk)
    wkv_out = jnp.swapaxes(s_fin, -1, -2).reshape(bn, h_a, HEAD_A, HEAD_A)
    shift_out = x[:, -1]

    vec_post = _pad_rows([P['a_lnx_g'][0], P['a_lnx_b'][0], P['ln_g'][0, 0], P['ln_b'][0, 0]], d)
    x1 = _rwkv_post(o.reshape(n, d), bonus, g, xf, vec_post, to_head, from_head, bf(P['a_w_o'][0]),
                    min(512, n))

    ln01 = _pad_rows([P['ln_g'][0, 1], P['ln_b'][0, 1]], d)
    x2 = _ffn(x1, P['ffn_w_gu'][0], P['ffn_w_down'][0], ln01, tm_ffn, FFN_COLS)

    pos = pos0 + jnp.arange(t, dtype=jnp.int32)
    tm_qkv = min(512, n)
    tables = _rope_tables(pos, max(tm_qkv // t, 1))
    kvw = KV_HEADS * HEAD_B
    q, k_new, v_new = _qkv_proj(x2, bf(P['b_w_q'][0]), bf(P['kv_w']), tables, tm_qkv)
    k_new = k_new.reshape(bn, t, kvw)
    v_new = v_new.reshape(bn, t, kvw)
    q = q.reshape(bn, t, d)
    sinks = P['b_sinks'][0].astype(F32)
    if k_cache is None:
        att = _attention(q, k_new, k_new, v_new, v_new, sinks, min(256, t), banded=True)
        k_out = k_new[:, -WINDOW:]
        v_out = v_new[:, -WINDOW:]
    else:
        kc = k_cache.astype(F32).reshape(bn, WINDOW, kvw)
        vc = v_cache.astype(F32).reshape(bn, WINDOW, kvw)
        att = _attention(q, kc, k_new, vc, v_new, sinks, t, banded=False)
        k_out = jnp.concatenate([kc, k_new], axis=1)[:, -WINDOW:]
        v_out = jnp.concatenate([vc, v_new], axis=1)[:, -WINDOW:]
    ln10 = _pad_rows([P['ln_g'][1, 0], P['ln_b'][1, 0]], d)
    router_pad = jnp.concatenate([P['moe_router'][0], jnp.zeros((d, LANES - N_EXPERTS), F32)], axis=1)
    routed = _proj_ln_router(att.reshape(n, d), x2, bf(P['b_w_o'][0]), ln10, router_pad,
                             min(MOE_ROW_TILE, n))

    return (routed, shift_out[None], wkv_out[None],
            k_out.reshape(bn, WINDOW, KV_HEADS, HEAD_B), v_out.reshape(bn, WINDOW, KV_HEADS, HEAD_B))


def kernel(x_prompt, x_sample, cache_shift_a, state_wkv_a, cache_k_b, cache_v_b, a_mu, a_w_rkv, a_w0, a_w1, a_w2, a_a0, a_a1, a_a2, a_g1, a_g2, a_k_k, a_k_a, a_r_k, a_lnx_g, a_lnx_b, a_w_o, kv_w, b_w_q, b_sinks, b_w_o, ln_g, ln_b, ffn_w_gu, ffn_w_down, moe_router, moe_w_gu, moe_w_down):
    P = {
        'a_mu': a_mu, 'a_w_rkv': a_w_rkv, 'a_w0': a_w0, 'a_w1': a_w1, 'a_w2': a_w2,
        'a_a0': a_a0, 'a_a1': a_a1, 'a_a2': a_a2, 'a_g1': a_g1, 'a_g2': a_g2,
        'a_k_k': a_k_k, 'a_k_a': a_k_a, 'a_r_k': a_r_k, 'a_lnx_g': a_lnx_g,
        'a_lnx_b': a_lnx_b, 'a_w_o': a_w_o, 'kv_w': kv_w, 'b_w_q': b_w_q,
        'b_sinks': b_sinks, 'b_w_o': b_w_o, 'ln_g': ln_g, 'ln_b': ln_b,
        'ffn_w_gu': ffn_w_gu, 'ffn_w_down': ffn_w_down, 'moe_router': moe_router,
        'moe_w_gu': moe_w_gu, 'moe_w_down': moe_w_down,
    }
    bp = x_prompt.shape[0]
    d = x_prompt.shape[2]
    h_a = d // HEAD_A
    zero_shift = jnp.zeros((1, bp, d), x_prompt.dtype)
    zero_wkv = jnp.zeros((1, bp, h_a, HEAD_A, HEAD_A), F32)
    x3_p, p_shift, p_wkv, p_k, p_v = _trunk(x_prompt, zero_shift, zero_wkv, None, None, 0, P)
    x3_s, s_shift, s_wkv, s_k, s_v = _trunk(x_sample, cache_shift_a, state_wkv_a,
                                            cache_k_b, cache_v_b, PAST_LEN, P)
    y_p, y_s = _moe_layer([x3_p, x3_s], P)
    return (y_p.reshape(x_prompt.shape), y_s.reshape(x_sample.shape),
            p_shift, p_wkv, p_k, p_v, s_shift, s_wkv, s_k, s_v)
```

```python
import functools

import jax
import jax.numpy as jnp
from jax import lax
from jax.experimental import pallas as pl
from jax.experimental.pallas import tpu as pltpu

F32 = jnp.float32
BF16 = jnp.bfloat16

DEPTH = 2
HEAD_A = 64
HEAD_B = 64
KV_HEADS = 4
CHUNK = 64
WINDOW = 128
PAST_LEN = 4096
ROPE_DIM = HEAD_B // 4
ROPE_THETA = 500000.0
ATTN_SCALE = HEAD_B ** -0.5
N_EXPERTS = 8
GN_EPS = 64e-5
LN_EPS = 1e-5
ALPHA = (2.0 * DEPTH) ** 0.25

LANES = 128
SUBLANES = 8
VMEM_LIMIT_BYTES = 56 * 1024 * 1024
POST_ROW_BLOCKS = 2
FFN_COLS = 512
SCAN_GROUP = 4
SCAN_PAIRS = 8
SCAN_BLOCK_FRAMES = 4096
MOE_TILE = 1024
MOE_ROW_TILE = 512
MOE_TILE_PARTS = 4
SLAB_UNIT = SUBLANES
ZERO_ROWS = 128


def _dot(a, b):
    return jnp.dot(a.astype(BF16), b.astype(BF16), preferred_element_type=F32)


def _dot_nt(a, b):
    return lax.dot_general(a.astype(BF16), b.astype(BF16), (((1,), (1,)), ((), ())),
                           preferred_element_type=F32)


def _dot_tn(a, b):
    return lax.dot_general(a.astype(BF16), b.astype(BF16), (((0,), (0,)), ((), ())),
                           preferred_element_type=F32)


def _split(x, n):
    parts = []
    rem = x
    for i in range(n):
        p = rem.astype(BF16)
        parts.append(p)
        if i + 1 < n:
            rem = rem - p.astype(F32)
    return parts


def _dot_exact_rhs(a, b_bf16, n):
    acc = None
    for p in _split(a, n):
        t = jnp.dot(p, b_bf16, preferred_element_type=F32)
        acc = t if acc is None else acc + t
    return acc


def _dot_exact_lhs(a_bf16, b, n):
    acc = None
    for p in _split(b, n):
        t = jnp.dot(a_bf16, p, preferred_element_type=F32)
        acc = t if acc is None else acc + t
    return acc


def _head_sum(x, to_head, from_head):
    return _dot_exact_rhs(_dot_exact_rhs(x, to_head, 2), from_head, 3)


def _sigmoid(z):
    return 1.0 / (1.0 + jnp.exp(-z))


def _layer_norm(z, g, b):
    mu = jnp.mean(z, axis=-1, keepdims=True)
    zc = z - mu
    var = jnp.mean(zc * zc, axis=-1, keepdims=True)
    return zc * lax.rsqrt(var + LN_EPS) * g + b


def _const_spec(shape):
    nd = len(shape)
    return pl.BlockSpec(shape, lambda *_: (0,) * nd)


def _params(sem):
    return pltpu.CompilerParams(dimension_semantics=sem, vmem_limit_bytes=VMEM_LIMIT_BYTES)


def _rwkv_pre_kernel(x_ref, bnd_ref, mu_ref, vec_ref, wr_ref, wk_ref, wv_ref, w1_ref, w2_ref,
                     a1_ref, a2_ref, g1_ref, g2_ref, th_ref, fh_ref,
                     r_out, ld_out, k_out, v_out, kk_out, ka_out, g_out, bonus_out):
    x = x_ref[...]
    rows = lax.broadcasted_iota(jnp.int32, x.shape, 0)
    xp = jnp.where(rows == 0, bnd_ref[0, 0:1, :], pltpu.roll(x, 1, 0))
    dx = xp - x

    def mix(s):
        return x + dx * mu_ref[s:s + 1, :]

    w0 = vec_ref[0:1, :]
    a0 = vec_ref[1:2, :]
    k_k = vec_ref[2:3, :]
    k_a = vec_ref[3:4, :]
    r_k = vec_ref[4:5, :]
    to_head = th_ref[...]
    from_head = fh_ref[...]

    lw = _dot(mix(3), w1_ref[...])
    la = _dot(mix(4), a1_ref[...])
    lg = _dot(mix(5), g1_ref[...])
    r = _dot(mix(0), wr_ref[...])
    k = _dot(mix(1), wk_ref[...])
    zw = -(w0 + _dot(jnp.tanh(lw), w2_ref[...]))
    a = _sigmoid(a0 + _dot(la, a2_ref[...]))
    g = _dot(_sigmoid(lg), g2_ref[...])
    v = _dot(mix(2), wv_ref[...])
    softplus = jnp.maximum(zw, 0.0) + jnp.log(1.0 + jnp.exp(-jnp.abs(zw)))
    ld = -jnp.exp(-softplus - 0.5)

    kk = k * k_k
    k_h = k * (1.0 + (a - 1.0) * k_a)
    ss_h = _dot_exact_rhs(kk * kk, to_head, 2)
    rk_h = _dot_exact_rhs(r * k_h * r_k, to_head, 2)
    ss = _dot_exact_rhs(ss_h, from_head, 3)
    bonus = _dot_exact_rhs(rk_h, from_head, 3) * v
    kk = kk / jnp.maximum(jnp.sqrt(ss), 1e-12)

    r_out[...] = r
    ld_out[...] = ld
    k_out[...] = k_h
    v_out[...] = v
    kk_out[...] = kk
    ka_out[...] = kk * a
    g_out[...] = g
    bonus_out[...] = bonus


def _rwkv_pre(x, bnd, mu8, vec8, wr, wk, wv, w1, w2, a1, a2, g1, g2, to_head, from_head, tm):
    n, d = x.shape
    row = pl.BlockSpec((tm, d), lambda i: (i, 0))
    ins = [row, pl.BlockSpec((1, SUBLANES, d), lambda i: (i, 0, 0))]
    ins += [_const_spec(a.shape)
            for a in (mu8, vec8, wr, wk, wv, w1, w2, a1, a2, g1, g2, to_head, from_head)]
    return pl.pallas_call(
        _rwkv_pre_kernel,
        grid=(n // tm,),
        in_specs=ins,
        out_specs=[row] * 8,
        out_shape=[jax.ShapeDtypeStruct((n, d), F32)] * 8,
        compiler_params=_params(("parallel",)),
        name="rwkv_pre",
    )(x, bnd, mu8, vec8, wr, wk, wv, w1, w2, a1, a2, g1, g2, to_head, from_head)


def _wkv_scan_kernel(r_ref, ld_ref, k_ref, v_ref, kk_ref, ka_ref, s0_ref, o_ref, st_ref, s_scr,
                     *, chunk, n_chunks, group):
    c = chunk
    pw = 2 * HEAD_A
    n_pp = s_scr.shape[0]
    t_idx = pl.program_id(2)

    @pl.when(t_idx == 0)
    def _():
        zeros_hh = jnp.zeros((HEAD_A, HEAD_A), F32)
        for pp in range(n_pp):
            s_scr[pp] = jnp.concatenate(
                [jnp.concatenate([s0_ref[pp, 0], zeros_hh], axis=1),
                 jnp.concatenate([zeros_hh, s0_ref[pp, 1]], axis=1)], axis=0)

    ri = lax.broadcasted_iota(jnp.int32, (c, c), 0)
    ci = lax.broadcasted_iota(jnp.int32, (c, c), 1)
    tri_incl = ri >= ci
    cum_mat = jnp.where(tri_incl, 1.0, 0.0).astype(BF16)
    ri4 = lax.broadcasted_iota(jnp.int32, (c, 4 * c), 0)
    ci4 = lax.broadcasted_iota(jnp.int32, (c, 4 * c), 1) & (c - 1)
    strict4 = ri4 > ci4
    incl4 = ri4 >= ci4
    first_blk = lax.broadcasted_iota(jnp.int32, (1, 2 * c), 1) < c
    eye_cat = (lax.broadcasted_iota(jnp.int32, (c, 2 * c), 0)
               == (lax.broadcasted_iota(jnp.int32, (c, 2 * c), 1) & (c - 1)))
    lane1 = lax.broadcasted_iota(jnp.int32, (1, pw), 1)
    head_a1 = lane1 < HEAD_A
    lane2 = lax.broadcasted_iota(jnp.int32, (1, 2 * pw), 1) & (pw - 1)
    head_a2 = lane2 < HEAD_A
    rs = lax.broadcasted_iota(jnp.int32, (pw, pw), 0)
    cs = lax.broadcasted_iota(jnp.int32, (pw, pw), 1)
    same_head = (rs < HEAD_A) == (cs < HEAD_A)
    eye = rs == cs
    zeros_cv = jnp.zeros((c, pw), F32)

    def group_maps(slices):
        each = lambda fn, *lists: [fn(*a) for a in zip(*lists)]
        ld = [ld_ref[ix] for ix in slices]
        cw = each(lambda x: _dot_exact_lhs(cum_mat, x, 3), ld)
        w_in = each(jnp.exp, cw)
        w_ex = each(lambda a, b: jnp.exp(a - b), cw, ld)
        w_inv = each(lambda a: jnp.exp(-a), cw)
        w_last = each(lambda a: a[c - 1:c, :], w_in)
        knt = [-(kk_ref[ix] * w) for ix, w in zip(slices, w_ex)]
        kat = [ka_ref[ix] * w for ix, w in zip(slices, w_inv)]
        kt = [k_ref[ix] * w for ix, w in zip(slices, w_inv)]
        rt = [r_ref[ix] * w for ix, w in zip(slices, w_in)]
        v = [v_ref[ix] for ix in slices]
        by_head = lambda z, is_a: jnp.concatenate(
            [jnp.where(is_a, z, 0.0), jnp.where(is_a, 0.0, z)], axis=0)
        lh = each(lambda a, b: jnp.concatenate([a, b], axis=0), knt, rt)
        rh = each(lambda a, b: jnp.concatenate(
            [by_head(a, head_a1), by_head(b, head_a1)], axis=0), kat, kt)
        v_st = each(lambda a: by_head(a, head_a1), v)

        full = each(_dot_nt, lh, rh)
        top = each(lambda a: jnp.where(strict4, a[:c, :], 0.0), full)
        bot = each(lambda a: jnp.where(incl4, a[c:, :], 0.0), full)
        n_cat = each(lambda a: a[:, :2 * c], top)
        akv = each(lambda a, b: _dot(a[:, 2 * c:], b), top, v_st)

        t_cat = each(lambda n: jnp.where(eye_cat, 1.0, 0.0) + n, n_cat)
        span = 2
        while span < c:
            n_cat = each(lambda n: _dot(n, by_head(n, first_blk)), n_cat)
            dt = each(lambda n, t0: _dot(n, by_head(t0, first_blk)), n_cat, t_cat)
            t_cat = each(lambda t0, d: t0 + d, t_cat, dt)
            span *= 2
        x = each(lambda t0, a, b: _dot(t0, by_head(jnp.concatenate([a, b], axis=1), head_a2)),
                 t_cat, knt, akv)

        v_wide = each(lambda a: jnp.concatenate([jnp.zeros_like(a), a], axis=1), v_st)
        qo_all = each(lambda b4, x0, vw: _dot(b4, jnp.concatenate([by_head(x0, head_a2), vw], axis=0)),
                      bot, x, v_wide)
        rhs2 = each(lambda a, b: jnp.concatenate(
            [a, jnp.concatenate([zeros_cv, b], axis=1)], axis=0), x, v)
        lt = each(lambda a, b, w: jnp.concatenate([a * w, b * w], axis=0), kat, kt, w_last)
        mb = each(_dot_tn, lt, rhs2)
        out = []
        for j in range(len(slices)):
            qo = qo_all[j]
            q = rt[j] + qo[:, :pw]
            m = jnp.where(eye, w_last[j], 0.0) + jnp.where(same_head, mb[j][:, :pw], 0.0)
            b = jnp.where(same_head, mb[j][:, pw:], 0.0)
            mq = jnp.concatenate([m, q], axis=0)
            out.append((mq.astype(BF16), b, qo[:, pw:]))
        return out

    def body(it, carry):
        items = [(j, pp) for j in range(group) for pp in range(n_pp)]
        slices = [(pl.ds(pl.multiple_of((it * group + j) * c, c), c), pl.ds(pp * pw, pw))
                  for j, pp in items]
        maps = group_maps(slices)
        s = [s_scr[pp] for pp in range(n_pp)]
        for (j, pp), ix, (mq, b, o0) in zip(items, slices, maps):
            res = _dot(mq, s[pp])
            s[pp] = res[:pw, :] + b
            o_ref[ix] = res[pw:, :] + o0
        for pp in range(n_pp):
            s_scr[pp] = s[pp]
        return carry

    lax.fori_loop(0, n_chunks // group, body, 0)

    @pl.when(t_idx == pl.num_programs(2) - 1)
    def _():
        for pp in range(n_pp):
            s = s_scr[pp]
            st_ref[pp, 0] = s[:HEAD_A, :HEAD_A]
            st_ref[pp, 1] = s[HEAD_A:, HEAD_A:]


def _wkv_scan(r, ld, k, v, kk, ka, s0, chunk, t_blk):
    bn, t, d = r.shape
    pw = 2 * HEAD_A
    n_pairs = d // pw
    n_chunks = t_blk // chunk
    n_pp = SCAN_PAIRS if n_chunks >= SCAN_GROUP else n_pairs
    seq = pl.BlockSpec((None, t_blk, n_pp * pw), lambda b, p, i: (b, i, p))
    st = pl.BlockSpec((None, n_pp, 2, HEAD_A, HEAD_A), lambda b, p, i: (b, p, 0, 0, 0))
    kern = functools.partial(_wkv_scan_kernel, chunk=chunk, n_chunks=n_chunks,
                             group=min(SCAN_GROUP, n_chunks))
    return pl.pallas_call(
        kern,
        grid=(bn, n_pairs // n_pp, t // t_blk),
        in_specs=[seq] * 6 + [st],
        out_specs=[seq, st],
        out_shape=[jax.ShapeDtypeStruct((bn, t, d), F32),
                   jax.ShapeDtypeStruct((bn, n_pairs, 2, HEAD_A, HEAD_A), F32)],
        scratch_shapes=[pltpu.VMEM((n_pp, pw, pw), F32)],
        compiler_params=_params(("parallel", "parallel", "arbitrary")),
        name="wkv_scan",
    )(r, ld, k, v, kk, ka, s0)


def _rwkv_post_kernel(o_ref, bonus_ref, g_ref, x_ref, vec_ref, th_ref, fh_ref, wo_ref, out_ref):
    to_head = th_ref[...]
    from_head = fh_ref[...]
    inv_n = 1.0 / HEAD_A
    tm = o_ref.shape[0]
    n_blk = POST_ROW_BLOCKS if tm % (POST_ROW_BLOCKS * SUBLANES) == 0 else 1
    blk = tm // n_blk
    sls = [pl.ds(b * blk, blk) for b in range(n_blk)]
    o = [o_ref[sl, :] for sl in sls]
    mean_h = [_dot_exact_rhs(a, to_head, 2) for a in o]
    mean = [_dot_exact_rhs(a, from_head, 3) * inv_n for a in mean_h]
    oc = [a - m for a, m in zip(o, mean)]
    var_h = [_dot_exact_rhs(a * a, to_head, 2) for a in oc]
    var = [_dot_exact_rhs(a, from_head, 3) * inv_n for a in var_h]
    y = [(c * lax.rsqrt(vr + GN_EPS) * vec_ref[0:1, :] + vec_ref[1:2, :] + bonus_ref[sl, :]) * g_ref[sl, :]
         for c, vr, sl in zip(oc, var, sls)]
    h = [_dot(a, wo_ref[...]) for a in y]
    for a, sl in zip(h, sls):
        out_ref[sl, :] = _layer_norm(ALPHA * x_ref[sl, :] + a, vec_ref[2:3, :], vec_ref[3:4, :])


def _rwkv_post(o, bonus, g, x, vec8, to_head, from_head, wo, tm):
    n, d = x.shape
    row = pl.BlockSpec((tm, d), lambda i: (i, 0))
    return pl.pallas_call(
        _rwkv_post_kernel,
        grid=(n // tm,),
        in_specs=[row] * 4 + [_const_spec(a.shape) for a in (vec8, to_head, from_head, wo)],
        out_specs=row,
        out_shape=jax.ShapeDtypeStruct((n, d), F32),
        compiler_params=_params(("parallel",)),
        name="rwkv_post",
    )(o, bonus, g, x, vec8, to_head, from_head, wo)


def _ffn_kernel(x_ref, wg_ref, wu_ref, wd_ref, ln_ref, out_ref, acc_ref, xb_ref):
    f = pl.program_id(1)

    @pl.when(f == 0)
    def _():
        acc_ref[...] = jnp.zeros_like(acc_ref)
        xb_ref[...] = x_ref[...].astype(BF16)

    xb = xb_ref[...]
    gate = jnp.dot(xb, wg_ref[...].astype(BF16), preferred_element_type=F32)
    up = jnp.dot(xb, wu_ref[...].astype(BF16), preferred_element_type=F32)
    h = gate * _sigmoid(gate) * up
    acc_ref[...] += jnp.dot(h.astype(BF16), wd_ref[...].astype(BF16), preferred_element_type=F32)

    @pl.when(f == pl.num_programs(1) - 1)
    def _():
        out_ref[...] = _layer_norm(ALPHA * x_ref[...] + acc_ref[...], ln_ref[0:1, :], ln_ref[1:2, :])


def _ffn(x, w_gu, w_down, ln8, tm, tf):
    n, d = x.shape
    d_ff = w_gu.shape[1] // 2
    nf = d_ff // tf
    return pl.pallas_call(
        _ffn_kernel,
        grid=(n // tm, nf),
        in_specs=[
            pl.BlockSpec((tm, d), lambda i, f: (i, 0)),
            pl.BlockSpec((d, tf), lambda i, f: (0, f)),
            pl.BlockSpec((d, tf), lambda i, f: (0, nf + f)),
            pl.BlockSpec((tf, d), lambda i, f: (f, 0)),
            _const_spec(ln8.shape),
        ],
        out_specs=pl.BlockSpec((tm, d), lambda i, f: (i, 0)),
        out_shape=jax.ShapeDtypeStruct((n, d), F32),
        scratch_shapes=[pltpu.VMEM((tm, d), F32), pltpu.VMEM((tm, d), BF16)],
        compiler_params=_params(("parallel", "arbitrary")),
        name="dense_ffn",
    )(x, w_gu, w_gu, w_down, ln8)


ROUTE_I1, ROUTE_I2, ROUTE_G1, ROUTE_G2, ROUTE_Q1, ROUTE_Q2 = range(6)
META_CARRY, META_COUNT = 0, 1


def _proj_ln_router_kernel(y_ref, x_ref, w_ref, ln_ref, rw_ref, out_ref, route_ref, meta_ref, count_ref,
                           carry_ref):
    h = _dot(y_ref[...], w_ref[...])
    x3 = _layer_norm(ALPHA * x_ref[...] + h, ln_ref[0:1, :], ln_ref[1:2, :])
    out_ref[...] = x3
    _route_rows(x3, rw_ref, route_ref, meta_ref, count_ref, carry_ref)


def _route_rows(x, w_ref, route_ref, meta_ref, count_ref, carry_ref):
    i = pl.program_id(0)

    @pl.when(i == 0)
    def _():
        carry_ref[...] = jnp.zeros_like(carry_ref)

    x_hi, x_lo = _split(x, 2)
    w_hi, w_lo = _split(w_ref[...], 2)
    logits = (jnp.dot(x_hi, w_hi, preferred_element_type=F32)
              + jnp.dot(x_lo, w_hi, preferred_element_type=F32)
              + jnp.dot(x_hi, w_lo, preferred_element_type=F32))
    tm = logits.shape[0]
    lane = lax.broadcasted_iota(jnp.int32, logits.shape, 1).astype(F32)
    neg = -jnp.inf
    logits = jnp.where(lane < N_EXPERTS, logits, neg)
    m1 = jnp.max(logits, axis=-1, keepdims=True)
    i1 = jnp.min(jnp.where(logits == m1, lane, float(LANES)), axis=-1, keepdims=True)
    rest = jnp.where(lane == i1, neg, logits)
    m2 = jnp.max(rest, axis=-1, keepdims=True)
    i2 = jnp.min(jnp.where(rest == m2, lane, float(LANES)), axis=-1, keepdims=True)
    e2 = jnp.exp(m2 - m1)
    den = 1.0 + e2

    sel1 = lane == i1
    sel2 = lane == i2
    onehot = jnp.where(sel1, 1.0, 0.0) + jnp.where(sel2, 1.0, 0.0)
    ri = lax.broadcasted_iota(jnp.int32, (tm, tm), 0)
    ci = lax.broadcasted_iota(jnp.int32, (tm, tm), 1)
    earlier = jnp.where(ri > ci, 1.0, 0.0).astype(BF16)
    in_tile = jnp.dot(earlier, onehot.astype(BF16), preferred_element_type=F32)
    q1 = jnp.sum(jnp.where(sel1, in_tile, 0.0), axis=-1, keepdims=True)
    q2 = jnp.sum(jnp.where(sel2, in_tile, 0.0), axis=-1, keepdims=True)
    tile_count = jnp.sum(onehot, axis=0, keepdims=True)

    route = jnp.zeros_like(logits)
    for col, val in ((ROUTE_I1, i1), (ROUTE_I2, i2), (ROUTE_G1, 1.0 / den), (ROUTE_G2, e2 / den),
                     (ROUTE_Q1, q1), (ROUTE_Q2, q2)):
        route = jnp.where(lane == float(col), val, route)
    route_ref[...] = route
    meta_row = lax.broadcasted_iota(jnp.int32, meta_ref.shape, 0)
    meta_ref[...] = jnp.where(meta_row == META_CARRY, carry_ref[0:1, :],
                              jnp.where(meta_row == META_COUNT, tile_count, 0.0))
    slab_rows = jnp.floor((tile_count + (SLAB_UNIT - 1)) * (1.0 / SLAB_UNIT)) * SLAB_UNIT
    carry_ref[0:1, :] = carry_ref[0:1, :] + slab_rows
    count_ref[...] = carry_ref[...]


def _proj_ln_router(y, x, w, ln8, router_pad, tm):
    n, d = x.shape
    row = pl.BlockSpec((tm, d), lambda i: (i, 0))
    return pl.pallas_call(
        _proj_ln_router_kernel,
        grid=(n // tm,),
        in_specs=[pl.BlockSpec((tm, y.shape[1]), lambda i: (i, 0)), row,
                  _const_spec(w.shape), _const_spec(ln8.shape), _const_spec(router_pad.shape)],
        out_specs=[row,
                   pl.BlockSpec((tm, LANES), lambda i: (i, 0)),
                   pl.BlockSpec((None, SUBLANES, LANES), lambda i: (i, 0, 0)),
                   _const_spec((SUBLANES, LANES))],
        out_shape=[jax.ShapeDtypeStruct((n, d), F32),
                   jax.ShapeDtypeStruct((n, LANES), F32),
                   jax.ShapeDtypeStruct((n // tm, SUBLANES, LANES), F32),
                   jax.ShapeDtypeStruct((SUBLANES, LANES), F32)],
        scratch_shapes=[pltpu.VMEM((SUBLANES, LANES), F32)],
        compiler_params=_params(("arbitrary",)),
        name="proj_ln_router",
    )(y, x, w, ln8, router_pad)


def _slab_plan(meta_ref):
    plan = []
    off = 0
    for e in range(N_EXPERTS):
        units = (meta_ref[0, e] + (SLAB_UNIT - 1)) // SLAB_UNIT
        plan.append((units, meta_ref[0, N_EXPERTS + e], off))
        off = off + units * SLAB_UNIT
    return plan


def _slab_buffer_rows(tm):
    worst = 2 * tm + N_EXPERTS * (SLAB_UNIT - 1)
    return ((worst + LANES - 1) // LANES) * LANES


def _slab_rows(expert, rank, plan):
    off = jnp.zeros_like(expert)
    for e, (_, _, e_off) in enumerate(plan):
        off = jnp.where(expert == e, e_off, off)
    return off + rank


def _slab_copies(plan, make_copy):
    total = 0
    for units, first_row, off in plan:
        def start(u, carry, first_row=first_row, off=off):
            make_copy(pl.multiple_of(off + u * SLAB_UNIT, SLAB_UNIT),
                      pl.multiple_of(first_row + u * SLAB_UNIT, SLAB_UNIT)).start()
            return carry
        lax.fori_loop(0, units, start, 0)
        total = total + units
    return total


def _dispatch_kernel(meta_ref, fill_ref, route_ref, x_ref, *rest):
    xs_ref, cbuf, zrows, sem = rest[-4:]
    tm = x_ref.shape[0]
    plan = _slab_plan(meta_ref)

    @pl.when(pl.program_id(0) == 0)
    def _():
        zrows[...] = jnp.zeros_like(zrows)

        def zero_copy(row, n_rows):
            return pltpu.make_async_copy(zrows.at[pl.ds(0, n_rows), :], xs_ref.at[pl.ds(row, n_rows), :],
                                         sem.at[1])

        def zero_range(first, pieces, n_rows):
            def start(u, carry):
                zero_copy(pl.multiple_of(first + u * n_rows, n_rows), n_rows).start()
                return carry

            def wait(u, carry):
                zero_copy(0, n_rows).wait()
                return carry

            lax.fori_loop(0, pieces, start, 0)
            lax.fori_loop(0, pieces, wait, 0)

        for e in range(N_EXPERTS):
            zero_range(fill_ref[0, e], fill_ref[0, N_EXPERTS + e], SLAB_UNIT)
        zero_range(fill_ref[0, 2 * N_EXPERTS], fill_ref[0, 2 * N_EXPERTS + 1], zrows.shape[0])

    sel = (lax.broadcasted_iota(jnp.int32, (SUBLANES, LANES), 0)
           == lax.broadcasted_iota(jnp.int32, (SUBLANES, LANES), 1)).astype(BF16)
    route_t = None
    for piece in _split(route_ref[...], 3):
        t = lax.dot_general(sel, piece, (((1,), (1,)), ((), ())), preferred_element_type=F32)
        route_t = t if route_t is None else route_t + t
    route_t = route_t.astype(jnp.int32)
    row1 = _slab_rows(route_t[ROUTE_I1:ROUTE_I1 + 1, :], route_t[ROUTE_Q1:ROUTE_Q1 + 1, :], plan)
    row2 = _slab_rows(route_t[ROUTE_I2:ROUTE_I2 + 1, :], route_t[ROUTE_Q2:ROUTE_Q2 + 1, :], plan)

    r_iota = lax.broadcasted_iota(jnp.int32, (cbuf.shape[0], tm), 0)
    perm = jnp.where((r_iota == row1) | (r_iota == row2), 1.0, 0.0).astype(BF16)
    cbuf[...] = jnp.dot(perm, x_ref[...].astype(BF16), preferred_element_type=F32)

    def make_copy(buf_row, sorted_row):
        return pltpu.make_async_copy(cbuf.at[pl.ds(buf_row, SLAB_UNIT), :],
                                     xs_ref.at[pl.ds(sorted_row, SLAB_UNIT), :], sem.at[0])

    n_started = _slab_copies(plan, make_copy)

    def wait(u, carry):
        make_copy(0, 0).wait()
        return carry

    lax.fori_loop(0, n_started, wait, 0)


def _dispatch(meta, fill, route, x, xs, xs_rows, tm):
    n, d = x.shape
    smem = lambda a, imap: pl.BlockSpec((None,) + a.shape[1:], imap, memory_space=pltpu.SMEM)
    in_specs = [smem(meta, lambda i: (i, 0, 0)), smem(fill, lambda i: (0, 0, 0)),
                pl.BlockSpec((tm, LANES), lambda i: (i, 0)),
                pl.BlockSpec((tm, d), lambda i: (i, 0))]
    args = [meta, fill, route, x]
    aliases = {}
    if xs is not None:
        in_specs.append(pl.BlockSpec(memory_space=pl.ANY))
        args.append(xs)
        aliases = {len(args) - 1: 0}
    return pl.pallas_call(
        _dispatch_kernel,
        grid=(n // tm,),
        in_specs=in_specs,
        out_specs=pl.BlockSpec(memory_space=pl.ANY),
        out_shape=jax.ShapeDtypeStruct((xs_rows, d), F32),
        scratch_shapes=[pltpu.VMEM((_slab_buffer_rows(tm), d), F32), pltpu.VMEM((ZERO_ROWS, d), F32),
                        pltpu.SemaphoreType.DMA((2,))],
        input_output_aliases=aliases,
        compiler_params=_params(("arbitrary",)),
        name="moe_dispatch",
    )(*args)


def _experts_kernel(te_ref, rows_ref, x_ref, wg_ref, wu_ref, wd_ref, out_ref, acc_ref, xb_ref):
    del te_ref
    f = pl.program_id(1)
    rows = rows_ref[pl.program_id(0)]
    tm = x_ref.shape[0]

    @pl.when(f == 0)
    def _():
        acc_ref[...] = jnp.zeros_like(acc_ref)

    def swiglu_rows(n_rows):
        sl = pl.ds(0, n_rows)

        @pl.when(f == 0)
        def _():
            xb_ref[sl, :] = x_ref[sl, :].astype(BF16)

        xb = xb_ref[sl, :]
        gate = jnp.dot(xb, wg_ref[...].astype(BF16), preferred_element_type=F32)
        up = jnp.dot(xb, wu_ref[...].astype(BF16), preferred_element_type=F32)
        h = gate * _sigmoid(gate) * up
        acc_ref[sl, :] += jnp.dot(h.astype(BF16), wd_ref[...].astype(BF16), preferred_element_type=F32)

    part = tm // MOE_TILE_PARTS
    for k in range(1, MOE_TILE_PARTS + 1):
        @pl.when(jnp.logical_and(rows > (k - 1) * part, rows <= k * part))
        def _(k=k):
            swiglu_rows(k * part)

    @pl.when(f == pl.num_programs(1) - 1)
    def _():
        out_ref[...] = acc_ref[...]


def _experts(tile_expert, tile_rows, xs, w_gu, w_down, tm, tf):
    s_total, d = xs.shape
    d_ff = w_gu.shape[2] // 2
    nf = d_ff // tf

    def f_eff(i, f, rows):
        return jnp.where(rows[i] > 0, f, nf - 1)

    grid_spec = pltpu.PrefetchScalarGridSpec(
        num_scalar_prefetch=2,
        grid=(s_total // tm, nf),
        in_specs=[
            pl.BlockSpec((tm, d), lambda i, f, te, rows: (jnp.where(rows[i] > 0, i, 0), 0)),
            pl.BlockSpec((None, d, tf), lambda i, f, te, rows: (te[i], 0, f_eff(i, f, rows))),
            pl.BlockSpec((None, d, tf), lambda i, f, te, rows: (te[i], 0, nf + f_eff(i, f, rows))),
            pl.BlockSpec((None, tf, d), lambda i, f, te, rows: (te[i], f_eff(i, f, rows), 0)),
        ],
        out_specs=pl.BlockSpec((tm, d), lambda i, f, te, rows: (i, 0)),
        scratch_shapes=[pltpu.VMEM((tm, d), F32), pltpu.VMEM((tm, d), BF16)],
    )
    return pl.pallas_call(
        _experts_kernel,
        grid_spec=grid_spec,
        out_shape=jax.ShapeDtypeStruct((s_total, d), F32),
        compiler_params=_params(("arbitrary", "arbitrary")),
        name="moe_experts",
    )(tile_expert, tile_rows, xs, w_gu, w_gu, w_down)


def _combine_kernel(meta_ref, meta_next_ref, route_ref, x_ref, ys_ref, ln_ref, out_ref, cbuf, sem):
    tm = x_ref.shape[0]
    i = pl.program_id(0)
    slot = i & 1
    plan = _slab_plan(meta_ref)

    def gather(tile_plan, dst_slot):
        def make_copy(buf_row, sorted_row):
            return pltpu.make_async_copy(ys_ref.at[pl.ds(sorted_row, SLAB_UNIT), :],
                                         cbuf.at[dst_slot, pl.ds(buf_row, SLAB_UNIT), :],
                                         sem.at[dst_slot])
        return make_copy, lambda: _slab_copies(tile_plan, make_copy)

    @pl.when(i == 0)
    def _():
        cbuf[...] = jnp.zeros_like(cbuf)
        gather(plan, 0)[1]()

    @pl.when(i + 1 < pl.num_programs(0))
    def _():
        gather(_slab_plan(meta_next_ref), 1 - slot)[1]()

    make_copy = gather(plan, slot)[0]
    n_mine = sum(units for units, _, _ in plan)

    def wait(u, carry):
        make_copy(0, 0).wait()
        return carry

    lax.fori_loop(0, n_mine, wait, 0)

    route = route_ref[...]
    col = lambda k: route[:, k:k + 1]
    row1 = _slab_rows(col(ROUTE_I1).astype(jnp.int32), col(ROUTE_Q1).astype(jnp.int32), plan)
    row2 = _slab_rows(col(ROUTE_I2).astype(jnp.int32), col(ROUTE_Q2).astype(jnp.int32), plan)
    c_iota = lax.broadcasted_iota(jnp.int32, (tm, cbuf.shape[1]), 1)
    gate = (jnp.where(c_iota == row1, col(ROUTE_G1), 0.0)
            + jnp.where(c_iota == row2, col(ROUTE_G2), 0.0))
    g_hi, g_lo = _split(gate, 2)
    c_hi, c_lo = _split(cbuf[slot], 2)
    y = (jnp.dot(g_hi, c_hi, preferred_element_type=F32)
         + jnp.dot(g_hi, c_lo, preferred_element_type=F32)
         + jnp.dot(g_lo, c_hi, preferred_element_type=F32))
    out_ref[...] = _layer_norm(ALPHA * x_ref[...] + y, ln_ref[0:1, :], ln_ref[1:2, :])


def _combine(meta, route, x, ys, ln8, tm):
    n, d = x.shape
    last = n // tm - 1
    meta_spec = lambda shift: pl.BlockSpec((None, 1, 2 * N_EXPERTS),
                                           lambda i: (jnp.minimum(i + shift, last), 0, 0),
                                           memory_space=pltpu.SMEM)
    return pl.pallas_call(
        _combine_kernel,
        grid=(n // tm,),
        in_specs=[meta_spec(0), meta_spec(1),
                  pl.BlockSpec((tm, LANES), lambda i: (i, 0)),
                  pl.BlockSpec((tm, d), lambda i: (i, 0)),
                  pl.BlockSpec(memory_space=pl.ANY),
                  _const_spec(ln8.shape)],
        out_specs=pl.BlockSpec((tm, d), lambda i: (i, 0)),
        out_shape=jax.ShapeDtypeStruct((n, d), F32),
        scratch_shapes=[pltpu.VMEM((2, _slab_buffer_rows(tm), d), F32), pltpu.SemaphoreType.DMA((2,))],
        compiler_params=_params(("arbitrary",)),
        name="moe_combine",
    )(meta, meta, route, x, ys, ln8)


def _rope(y, cos_t, sin_next, sin_prev):
    n = y.shape[1]
    reps = n // LANES
    tile = lambda t: jnp.concatenate([t] * reps, axis=1) if reps > 1 else t
    half = ROPE_DIM // 2
    return (y * tile(cos_t)
            + pltpu.roll(y, n - half, 1) * tile(sin_next)
            + pltpu.roll(y, half, 1) * tile(sin_prev))


def _qkv_kernel(x_ref, wq_ref, wkv_ref, cos_ref, sn_ref, sp_ref, q_out, k_out, v_out):
    xb = x_ref[...].astype(BF16)
    tables = (cos_ref[...], sn_ref[...], sp_ref[...])
    q_out[...] = _rope(jnp.dot(xb, wq_ref[...], preferred_element_type=F32), *tables)
    kv = jnp.dot(xb, wkv_ref[...], preferred_element_type=F32)
    nk = k_out.shape[1]
    k_out[...] = _rope(kv[:, :nk], *tables)
    v_out[...] = kv[:, nk:]


def _qkv_proj(x, wq, wkv, tables, tm):
    n, d = x.shape
    kvw = wkv.shape[1] // 2
    t_tiles = tables[0].shape[0] // tm
    tab = pl.BlockSpec((tm, LANES), lambda i: (i % t_tiles, 0))
    widths = (wq.shape[1], kvw, kvw)
    return pl.pallas_call(
        _qkv_kernel,
        grid=(n // tm,),
        in_specs=[pl.BlockSpec((tm, d), lambda i: (i, 0)), _const_spec(wq.shape), _const_spec(wkv.shape),
                  tab, tab, tab],
        out_specs=[pl.BlockSpec((tm, ow), lambda i: (i, 0)) for ow in widths],
        out_shape=[jax.ShapeDtypeStruct((n, ow), F32) for ow in widths],
        compiler_params=_params(("parallel",)),
        name="qkv_proj",
    )(x, wq, wkv, *tables)


def _attn_kernel(sink_ref, q_ref, kp_ref, kc_ref, vp_ref, vc_ref, o_ref, *, banded):
    tq = q_ref.shape[0]
    n_prev = kp_ref.shape[0]
    tk = n_prev + kc_ref.shape[0]
    q = q_ref[...] * ATTN_SCALE
    kband = jnp.concatenate([kp_ref[...], kc_ref[...]], axis=0)
    vband = jnp.concatenate([vp_ref[...], vc_ref[...]], axis=0)
    n_heads = q.shape[1] // HEAD_B
    group = n_heads // KV_HEADS
    qc = CHUNK if banded else tq
    kc = WINDOW + CHUNK if banded else tk
    units = [(kh, ci) for kh in range(KV_HEADS) for ci in range(tq // qc)]
    if banded:
        kj = lax.broadcasted_iota(jnp.int32, (kc, group * qc), 0)
        band_start = pl.program_id(1) * tq - n_prev

    qlane = lax.broadcasted_iota(jnp.int32, (1, group * qc), 1)

    def scores(kh, ci):
        qs = jnp.concatenate([q[ci * qc:(ci + 1) * qc, (kh * group + j) * HEAD_B:(kh * group + j + 1) * HEAD_B]
                              for j in range(group)], axis=0)
        return _dot_nt(kband[ci * qc:ci * qc + kc, kh * HEAD_B:(kh + 1) * HEAD_B], qs)

    def softmax(kh, ci, s):
        sk = jnp.full((1, group * qc), sink_ref[kh * group], F32)
        for j in range(1, group):
            sk = jnp.where(qlane >= j * qc, sink_ref[kh * group + j], sk)
        if banded:
            s = jnp.where(band_start + ci * qc + kj >= 0, s, -jnp.inf)
        m = jnp.maximum(jnp.max(s, axis=0, keepdims=True), sk)
        p = jnp.exp(s - m)
        return p * (1.0 / (jnp.sum(p, axis=0, keepdims=True) + jnp.exp(sk - m)))

    def weighted(kh, ci, p):
        return _dot_tn(p, vband[ci * qc:ci * qc + kc, kh * HEAD_B:(kh + 1) * HEAD_B])

    chunks = range(tq // qc)
    outs = {}
    s_next = [scores(0, ci) for ci in chunks]
    for kh in range(KV_HEADS):
        s_cur = s_next
        if kh + 1 < KV_HEADS:
            s_next = [scores(kh + 1, ci) for ci in chunks]
        probs = [softmax(kh, ci, s_cur[ci]) for ci in chunks]
        for ci in chunks:
            o = weighted(kh, ci, probs[ci])
            for j in range(group):
                outs[(kh * group + j, ci)] = o[j * qc:(j + 1) * qc, :]
    o_ref[...] = jnp.concatenate(
        [jnp.concatenate([outs[(h, ci)] for ci in range(tq // qc)], axis=0) for h in range(n_heads)],
        axis=1)


def _attention(q, k_prev_src, k_cur_src, v_prev_src, v_cur_src, sinks, tq, banded):
    bn, t, d = q.shape
    kw = k_cur_src.shape[2]
    if banded:
        ratio = tq // WINDOW
        prev_map = lambda b, i: (b, jnp.maximum(i * ratio - 1, 0), 0)
    else:
        prev_map = lambda b, i: (b, 0, 0)
    prev = pl.BlockSpec((None, WINDOW, kw), prev_map)
    cur = pl.BlockSpec((None, tq, kw), lambda b, i: (b, i, 0))
    kern = functools.partial(_attn_kernel, banded=banded)
    return pl.pallas_call(
        kern,
        grid=(bn, t // tq),
        in_specs=[pl.BlockSpec(memory_space=pltpu.SMEM),
                  pl.BlockSpec((None, tq, d), lambda b, i: (b, i, 0)), prev, cur, prev, cur],
        out_specs=pl.BlockSpec((None, tq, d), lambda b, i: (b, i, 0)),
        out_shape=jax.ShapeDtypeStruct((bn, t, d), F32),
        compiler_params=_params(("parallel", "parallel")),
        name="swa_attn",
    )(sinks, q, k_prev_src, k_cur_src, v_prev_src, v_cur_src)


def _moe_layer(groups, P):
    xs_rows = [g[0] for g in groups]
    routed = [g[1:] for g in groups]
    d = xs_rows[0].shape[1]
    tm_e = MOE_TILE
    counts = [c[0, :N_EXPERTS].astype(jnp.int32) for _, _, c in routed]
    total = sum(counts)
    padded = ((total + tm_e - 1) // tm_e) * tm_e
    ends = jnp.cumsum(padded)
    starts = ends - padded
    n_slabs = sum(-(-x.shape[0] // MOE_ROW_TILE) for x in xs_rows) * N_EXPERTS
    n_assign = 2 * sum(x.shape[0] for x in xs_rows) + n_slabs * (SLAB_UNIT - 1)
    n_tiles = (n_assign + N_EXPERTS * (tm_e - 1)) // tm_e
    n_used = (ends[-1] // tm_e).astype(jnp.int32)
    tile_expert = jnp.sum((jnp.arange(n_tiles) * tm_e)[:, None] >= ends[None, :], axis=1).astype(jnp.int32)
    tile_expert = jnp.minimum(tile_expert, N_EXPERTS - 1)
    tile_expert = jnp.where(jnp.arange(n_tiles) < n_used, tile_expert,
                            tile_expert[jnp.maximum(n_used - 1, 0)])
    tile_rows = jnp.clip((starts + total)[tile_expert] - jnp.arange(n_tiles) * tm_e, 0, tm_e)
    tile_rows = jnp.where(jnp.arange(n_tiles) < n_used, tile_rows, 0).astype(jnp.int32)

    metas = []
    base = starts
    for (_, meta, _), cnt in zip(routed, counts):
        carry = meta[:, META_CARRY, :N_EXPERTS].astype(jnp.int32)
        sent = meta[:, META_COUNT, :N_EXPERTS].astype(jnp.int32)
        metas.append(jnp.concatenate([sent, base[None, :] + carry], axis=1)[:, None, :])
        base = base + cnt

    first_end = starts + counts[0]
    tail = jnp.stack([ends[-1], (n_tiles * tm_e - ends[-1]) // ZERO_ROWS])
    fill = jnp.concatenate([first_end, (ends - first_end) // SLAB_UNIT, tail]).astype(jnp.int32)
    xs = None
    for m, (route, _, _), x in zip(metas, routed, xs_rows):
        xs = _dispatch(m, fill.reshape(1, 1, -1), route, x, xs, n_tiles * tm_e,
                       min(MOE_ROW_TILE, x.shape[0]))
        fill = jnp.zeros_like(fill)
    ys = _experts(tile_expert, tile_rows, xs, P['moe_w_gu'][0], P['moe_w_down'][0], tm_e, FFN_COLS)
    ln11 = _pad_rows([P['ln_g'][1, 1], P['ln_b'][1, 1]], d)
    return [_combine(m, route, x, ys, ln11, min(MOE_ROW_TILE, x.shape[0]))
            for m, (route, _, _), x in zip(metas, routed, xs_rows)]


def _pad_rows(rows, d):
    a = jnp.stack(rows).astype(F32)
    return jnp.concatenate([a, jnp.zeros((SUBLANES - a.shape[0], d), F32)], axis=0)


def _rope_tables(pos, reps):
    inv_freq = ROPE_THETA ** (-jnp.arange(0, ROPE_DIM, 2, dtype=jnp.float32) / ROPE_DIM)
    ang = pos.astype(jnp.float32)[:, None] * inv_freq[None, :]
    cos = jnp.cos(ang)
    sin = jnp.sin(ang)
    t = pos.shape[0]
    half = ROPE_DIM // 2
    rest = HEAD_B - ROPE_DIM
    z_half = jnp.zeros((t, half), F32)
    z_rest = jnp.zeros((t, rest), F32)
    cos_h = jnp.concatenate([cos, cos, jnp.ones((t, rest), F32)], axis=1)
    sn_h = jnp.concatenate([-sin, z_half, z_rest], axis=1)
    sp_h = jnp.concatenate([z_half, sin, z_rest], axis=1)
    per_tile = LANES // HEAD_B
    return tuple(jnp.tile(a, (reps, per_tile)) for a in (cos_h, sn_h, sp_h))


def _trunk(x, shift_in, wkv_in, k_cache, v_cache, pos0, P):
    bn, t, d = x.shape
    n = bn * t
    h_a = d // HEAD_A
    pw = 2 * HEAD_A
    n_pairs = d // pw
    xf = x.reshape(n, d)

    tm_pre = min(256, t)
    tm_row = min(256, n)
    tm_ffn = min(1024, n)
    chunk = min(CHUNK, t)
    t_blk = min(SCAN_BLOCK_FRAMES // SCAN_PAIRS, t)

    tiles = jnp.arange(n // tm_pre) * tm_pre
    prev_rows = xf[jnp.maximum(tiles - 1, 0)]
    start_rows = shift_in[0][tiles // t]
    bnd = jnp.where(((tiles % t) == 0)[:, None], start_rows, prev_rows)
    bnd = jnp.broadcast_to(bnd[:, None, :], (n // tm_pre, SUBLANES, d))

    mu8 = jnp.concatenate([P['a_mu'][0], jnp.zeros((2, d), F32)], axis=0)
    vec_pre = _pad_rows([P['a_w0'][0], P['a_a0'][0], P['a_k_k'][0], P['a_k_a'][0],
                         P['a_r_k'][0].reshape(d)], d)
    lane_head = jnp.arange(d) // HEAD_A
    to_head = (lane_head[:, None] == jnp.arange(LANES)[None, :]).astype(BF16)
    from_head = to_head.T
    bf = lambda a: a.astype(BF16)
    w_rkv = P['a_w_rkv'][0]
    r, ld, k_h, v, kk, ka, g, bonus = _rwkv_pre(
        xf, bnd, mu8, vec_pre, bf(w_rkv[0]), bf(w_rkv[1]), bf(w_rkv[2]),
        bf(P['a_w1'][0]), bf(P['a_w2'][0]), bf(P['a_a1'][0]), bf(P['a_a2'][0]),
        bf(P['a_g1'][0]), bf(P['a_g2'][0]), to_head, from_head, tm_pre)

    s0 = jnp.swapaxes(wkv_in[0].astype(F32), -1, -2).reshape(bn, n_pairs, 2, HEAD_A, HEAD_A)
    seq3 = lambda a: a.reshape(bn, t, d)
    o, s_fin = _wkv_scan(seq3(r), seq3(ld), seq3(k_h), seq3(v), seq3(kk), seq3(ka), s0, chunk, t_blk)
    wkv_out = jnp.swapaxes(s_fin, -1, -2).reshape(bn, h_a, HEAD_A, HEAD_A)
    shift_out = x[:, -1]

    vec_post = _pad_rows([P['a_lnx_g'][0], P['a_lnx_b'][0], P['ln_g'][0, 0], P['ln_b'][0, 0]], d)
    x1 = _rwkv_post(o.reshape(n, d), bonus, g, xf, vec_post, to_head, from_head, bf(P['a_w_o'][0]),
                    min(512, n))

    ln01 = _pad_rows([P['ln_g'][0, 1], P['ln_b'][0, 1]], d)
    x2 = _ffn(x1, P['ffn_w_gu'][0], P['ffn_w_down'][0], ln01, tm_ffn, FFN_COLS)

    pos = pos0 + jnp.arange(t, dtype=jnp.int32)
    tm_qkv = min(512, n)
    tables = _rope_tables(pos, max(tm_qkv // t, 1))
    kvw = KV_HEADS * HEAD_B
    q, k_new, v_new = _qkv_proj(x2, bf(P['b_w_q'][0]), bf(P['kv_w']), tables, tm_qkv)
    k_new = k_new.reshape(bn, t, kvw)
    v_new = v_new.reshape(bn, t, kvw)
    q = q.reshape(bn, t, d)
    sinks = P['b_sinks'][0].astype(F32)
    if k_cache is None:
        att = _attention(q, k_new, k_new, v_new, v_new, sinks, min(256, t), banded=True)
        k_out = k_new[:, -WINDOW:]
        v_out = v_new[:, -WINDOW:]
    else:
        kc = k_cache.astype(F32).reshape(bn, WINDOW, kvw)
        vc = v_cache.astype(F32).reshape(bn, WINDOW, kvw)
        att = _attention(q, kc, k_new, vc, v_new, sinks, t, banded=False)
        k_out = jnp.concatenate([kc, k_new], axis=1)[:, -WINDOW:]
        v_out = jnp.concatenate([vc, v_new], axis=1)[:, -WINDOW:]
    ln10 = _pad_rows([P['ln_g'][1, 0], P['ln_b'][1, 0]], d)
    router_pad = jnp.concatenate([P['moe_router'][0], jnp.zeros((d, LANES - N_EXPERTS), F32)], axis=1)
    routed = _proj_ln_router(att.reshape(n, d), x2, bf(P['b_w_o'][0]), ln10, router_pad,
                             min(MOE_ROW_TILE, n))

    return (routed, shift_out[None], wkv_out[None],
            k_out.reshape(bn, WINDOW, KV_HEADS, HEAD_B), v_out.reshape(bn, WINDOW, KV_HEADS, HEAD_B))


def kernel(x_prompt, x_sample, cache_shift_a, state_wkv_a, cache_k_b, cache_v_b, a_mu, a_w_rkv, a_w0, a_w1, a_w2, a_a0, a_a1, a_a2, a_g1, a_g2, a_k_k, a_k_a, a_r_k, a_lnx_g, a_lnx_b, a_w_o, kv_w, b_w_q, b_sinks, b_w_o, ln_g, ln_b, ffn_w_gu, ffn_w_down, moe_router, moe_w_gu, moe_w_down):
    P = {
        'a_mu': a_mu, 'a_w_rkv': a_w_rkv, 'a_w0': a_w0, 'a_w1': a_w1, 'a_w2': a_w2,
        'a_a0': a_a0, 'a_a1': a_a1, 'a_a2': a_a2, 'a_g1': a_g1, 'a_g2': a_g2,
        'a_k_k': a_k_k, 'a_k_a': a_k_a, 'a_r_k': a_r_k, 'a_lnx_g': a_lnx_g,
        'a_lnx_b': a_lnx_b, 'a_w_o': a_w_o, 'kv_w': kv_w, 'b_w_q': b_w_q,
        'b_sinks': b_sinks, 'b_w_o': b_w_o, 'ln_g': ln_g, 'ln_b': ln_b,
        'ffn_w_gu': ffn_w_gu, 'ffn_w_down': ffn_w_down, 'moe_router': moe_router,
        'moe_w_gu': moe_w_gu, 'moe_w_down': moe_w_down,
    }
    bp = x_prompt.shape[0]
    d = x_prompt.shape[2]
    h_a = d // HEAD_A
    zero_shift = jnp.zeros((1, bp, d), x_prompt.dtype)
    zero_wkv = jnp.zeros((1, bp, h_a, HEAD_A, HEAD_A), F32)
    x3_p, p_shift, p_wkv, p_k, p_v = _trunk(x_prompt, zero_shift, zero_wkv, None, None, 0, P)
    x3_s, s_shift, s_wkv, s_k, s_v = _trunk(x_sample, cache_shift_a, state_wkv_a,
                                            cache_k_b, cache_v_b, PAST_LEN, P)
    y_p, y_s = _moe_layer([x3_p, x3_s], P)
    return (y_p.reshape(x_prompt.shape), y_s.reshape(x_sample.shape),
            p_shift, p_wkv, p_k, p_v, s_shift, s_wkv, s_k, s_v)
```

```python
import functools

import jax
import jax.numpy as jnp
from jax import lax
from jax.experimental import pallas as pl
from jax.experimental.pallas import tpu as pltpu

F32 = jnp.float32
BF16 = jnp.bfloat16

DEPTH = 2
HEAD_A = 64
HEAD_B = 64
KV_HEADS = 4
CHUNK = 64
WINDOW = 128
PAST_LEN = 4096
ROPE_DIM = HEAD_B // 4
ROPE_THETA = 500000.0
ATTN_SCALE = HEAD_B ** -0.5
N_EXPERTS = 8
GN_EPS = 64e-5
LN_EPS = 1e-5
ALPHA = (2.0 * DEPTH) ** 0.25

LANES = 128
SUBLANES = 8
VMEM_LIMIT_BYTES = 56 * 1024 * 1024
POST_ROW_BLOCKS = 2
FFN_COLS = 512
SCAN_GROUP = 4
SCAN_PAIRS = 8
SCAN_BLOCK_FRAMES = 4096
MOE_TILE = 1024
MOE_ROW_TILE = 512
MOE_TILE_PARTS = 4
SLAB_UNIT = SUBLANES
ZERO_ROWS = 128


def _dot(a, b):
    return jnp.dot(a.astype(BF16), b.astype(BF16), preferred_element_type=F32)


def _dot_nt(a, b):
    return lax.dot_general(a.astype(BF16), b.astype(BF16), (((1,), (1,)), ((), ())),
                           preferred_element_type=F32)


def _dot_tn(a, b):
    return lax.dot_general(a.astype(BF16), b.astype(BF16), (((0,), (0,)), ((), ())),
                           preferred_element_type=F32)


def _split(x, n):
    parts = []
    rem = x
    for i in range(n):
        p = rem.astype(BF16)
        parts.append(p)
        if i + 1 < n:
            rem = rem - p.astype(F32)
    return parts


def _dot_exact_rhs(a, b_bf16, n):
    acc = None
    for p in _split(a, n):
        t = jnp.dot(p, b_bf16, preferred_element_type=F32)
        acc = t if acc is None else acc + t
    return acc


def _dot_exact_lhs(a_bf16, b, n):
    acc = None
    for p in _split(b, n):
        t = jnp.dot(a_bf16, p, preferred_element_type=F32)
        acc = t if acc is None else acc + t
    return acc


def _head_sum(x, to_head, from_head):
    return _dot_exact_rhs(_dot_exact_rhs(x, to_head, 2), from_head, 3)


def _sigmoid(z):
    return 1.0 / (1.0 + jnp.exp(-z))


def _layer_norm(z, g, b):
    mu = jnp.mean(z, axis=-1, keepdims=True)
    zc = z - mu
    var = jnp.mean(zc * zc, axis=-1, keepdims=True)
    return zc * lax.rsqrt(var + LN_EPS) * g + b


def _const_spec(shape):
    nd = len(shape)
    return pl.BlockSpec(shape, lambda *_: (0,) * nd)


def _params(sem):
    return pltpu.CompilerParams(dimension_semantics=sem, vmem_limit_bytes=VMEM_LIMIT_BYTES)


def _rwkv_pre_kernel(x_ref, bnd_ref, mu_ref, vec_ref, wr_ref, wk_ref, wv_ref, w1_ref, w2_ref,
                     a1_ref, a2_ref, g1_ref, g2_ref, th_ref, fh_ref,
                     r_out, ld_out, k_out, v_out, kk_out, ka_out, g_out, bonus_out):
    x = x_ref[...]
    rows = lax.broadcasted_iota(jnp.int32, x.shape, 0)
    xp = jnp.where(rows == 0, bnd_ref[0, 0:1, :], pltpu.roll(x, 1, 0))
    dx = xp - x

    def mix(s):
        return x + dx * mu_ref[s:s + 1, :]

    w0 = vec_ref[0:1, :]
    a0 = vec_ref[1:2, :]
    k_k = vec_ref[2:3, :]
    k_a = vec_ref[3:4, :]
    r_k = vec_ref[4:5, :]
    to_head = th_ref[...]
    from_head = fh_ref[...]

    lw = _dot(mix(3), w1_ref[...])
    la = _dot(mix(4), a1_ref[...])
    lg = _dot(mix(5), g1_ref[...])
    r = _dot(mix(0), wr_ref[...])
    k = _dot(mix(1), wk_ref[...])
    zw = -(w0 + _dot(jnp.tanh(lw), w2_ref[...]))
    a = _sigmoid(a0 + _dot(la, a2_ref[...]))
    g = _dot(_sigmoid(lg), g2_ref[...])
    v = _dot(mix(2), wv_ref[...])
    softplus = jnp.maximum(zw, 0.0) + jnp.log(1.0 + jnp.exp(-jnp.abs(zw)))
    ld = -jnp.exp(-softplus - 0.5)

    kk = k * k_k
    k_h = k * (1.0 + (a - 1.0) * k_a)
    ss_h = _dot_exact_rhs(kk * kk, to_head, 2)
    rk_h = _dot_exact_rhs(r * k_h * r_k, to_head, 2)
    ss = _dot_exact_rhs(ss_h, from_head, 3)
    bonus = _dot_exact_rhs(rk_h, from_head, 3) * v
    kk = kk / jnp.maximum(jnp.sqrt(ss), 1e-12)

    r_out[...] = r
    ld_out[...] = ld
    k_out[...] = k_h
    v_out[...] = v
    kk_out[...] = kk
    ka_out[...] = kk * a
    g_out[...] = g
    bonus_out[...] = bonus


def _rwkv_pre(x, bnd, mu8, vec8, wr, wk, wv, w1, w2, a1, a2, g1, g2, to_head, from_head, tm):
    n, d = x.shape
    row = pl.BlockSpec((tm, d), lambda i: (i, 0))
    ins = [row, pl.BlockSpec((1, SUBLANES, d), lambda i: (i, 0, 0))]
    ins += [_const_spec(a.shape)
            for a in (mu8, vec8, wr, wk, wv, w1, w2, a1, a2, g1, g2, to_head, from_head)]
    return pl.pallas_call(
        _rwkv_pre_kernel,
        grid=(n // tm,),
        in_specs=ins,
        out_specs=[row] * 8,
        out_shape=[jax.ShapeDtypeStruct((n, d), F32)] * 8,
        compiler_params=_params(("parallel",)),
        name="rwkv_pre",
    )(x, bnd, mu8, vec8, wr, wk, wv, w1, w2, a1, a2, g1, g2, to_head, from_head)


def _wkv_scan_kernel(r_ref, ld_ref, k_ref, v_ref, kk_ref, ka_ref, s0_ref, o_ref, st_ref, s_scr,
                     *, chunk, n_chunks, group):
    c = chunk
    pw = 2 * HEAD_A
    n_pp = s_scr.shape[0]
    t_idx = pl.program_id(2)

    @pl.when(t_idx == 0)
    def _():
        zeros_hh = jnp.zeros((HEAD_A, HEAD_A), F32)
        for pp in range(n_pp):
            s_scr[pp] = jnp.concatenate(
                [jnp.concatenate([s0_ref[pp, 0], zeros_hh], axis=1),
                 jnp.concatenate([zeros_hh, s0_ref[pp, 1]], axis=1)], axis=0)

    ri = lax.broadcasted_iota(jnp.int32, (c, c), 0)
    ci = lax.broadcasted_iota(jnp.int32, (c, c), 1)
    tri_incl = ri >= ci
    cum_mat = jnp.where(tri_incl, 1.0, 0.0).astype(BF16)
    ri4 = lax.broadcasted_iota(jnp.int32, (c, 4 * c), 0)
    ci4 = lax.broadcasted_iota(jnp.int32, (c, 4 * c), 1) & (c - 1)
    strict4 = ri4 > ci4
    incl4 = ri4 >= ci4
    first_blk = lax.broadcasted_iota(jnp.int32, (1, 2 * c), 1) < c
    eye_cat = (lax.broadcasted_iota(jnp.int32, (c, 2 * c), 0)
               == (lax.broadcasted_iota(jnp.int32, (c, 2 * c), 1) & (c - 1)))
    lane1 = lax.broadcasted_iota(jnp.int32, (1, pw), 1)
    head_a1 = lane1 < HEAD_A
    lane2 = lax.broadcasted_iota(jnp.int32, (1, 2 * pw), 1) & (pw - 1)
    head_a2 = lane2 < HEAD_A
    rs = lax.broadcasted_iota(jnp.int32, (pw, pw), 0)
    cs = lax.broadcasted_iota(jnp.int32, (pw, pw), 1)
    same_head = (rs < HEAD_A) == (cs < HEAD_A)
    eye = rs == cs
    zeros_cv = jnp.zeros((c, pw), F32)

    def group_maps(slices):
        each = lambda fn, *lists: [fn(*a) for a in zip(*lists)]
        ld = [ld_ref[ix] for ix in slices]
        cw = each(lambda x: _dot_exact_lhs(cum_mat, x, 3), ld)
        w_in = each(jnp.exp, cw)
        w_ex = each(lambda a, b: jnp.exp(a - b), cw, ld)
        w_inv = each(lambda a: jnp.exp(-a), cw)
        w_last = each(lambda a: a[c - 1:c, :], w_in)
        knt = [-(kk_ref[ix] * w) for ix, w in zip(slices, w_ex)]
        kat = [ka_ref[ix] * w for ix, w in zip(slices, w_inv)]
        kt = [k_ref[ix] * w for ix, w in zip(slices, w_inv)]
        rt = [r_ref[ix] * w for ix, w in zip(slices, w_in)]
        v = [v_ref[ix] for ix in slices]
        by_head = lambda z, is_a: jnp.concatenate(
            [jnp.where(is_a, z, 0.0), jnp.where(is_a, 0.0, z)], axis=0)
        lh = each(lambda a, b: jnp.concatenate([a, b], axis=0), knt, rt)
        rh = each(lambda a, b: jnp.concatenate(
            [by_head(a, head_a1), by_head(b, head_a1)], axis=0), kat, kt)
        v_st = each(lambda a: by_head(a, head_a1), v)

        full = each(_dot_nt, lh, rh)
        top = each(lambda a: jnp.where(strict4, a[:c, :], 0.0), full)
        bot = each(lambda a: jnp.where(incl4, a[c:, :], 0.0), full)
        n_cat = each(lambda a: a[:, :2 * c], top)
        akv = each(lambda a, b: _dot(a[:, 2 * c:], b), top, v_st)

        t_cat = each(lambda n: jnp.where(eye_cat, 1.0, 0.0) + n, n_cat)
        span = 2
        while span < c:
            n_cat = each(lambda n: _dot(n, by_head(n, first_blk)), n_cat)
            dt = each(lambda n, t0: _dot(n, by_head(t0, first_blk)), n_cat, t_cat)
            t_cat = each(lambda t0, d: t0 + d, t_cat, dt)
            span *= 2
        x = each(lambda t0, a, b: _dot(t0, by_head(jnp.concatenate([a, b], axis=1), head_a2)),
                 t_cat, knt, akv)

        v_wide = each(lambda a: jnp.concatenate([jnp.zeros_like(a), a], axis=1), v_st)
        qo_all = each(lambda b4, x0, vw: _dot(b4, jnp.concatenate([by_head(x0, head_a2), vw], axis=0)),
                      bot, x, v_wide)
        rhs2 = each(lambda a, b: jnp.concatenate(
            [a, jnp.concatenate([zeros_cv, b], axis=1)], axis=0), x, v)
        lt = each(lambda a, b, w: jnp.concatenate([a * w, b * w], axis=0), kat, kt, w_last)
        mb = each(_dot_tn, lt, rhs2)
        out = []
        for j in range(len(slices)):
            qo = qo_all[j]
            q = rt[j] + qo[:, :pw]
            m = jnp.where(eye, w_last[j], 0.0) + jnp.where(same_head, mb[j][:, :pw], 0.0)
            b = jnp.where(same_head, mb[j][:, pw:], 0.0)
            mq = jnp.concatenate([m, q], axis=0)
            out.append((mq.astype(BF16), b, qo[:, pw:]))
        return out

    def body(it, carry):
        items = [(j, pp) for j in range(group) for pp in range(n_pp)]
        slices = [(pl.ds(pl.multiple_of((it * group + j) * c, c), c), pl.ds(pp * pw, pw))
                  for j, pp in items]
        maps = group_maps(slices)
        s = [s_scr[pp] for pp in range(n_pp)]
        for (j, pp), ix, (mq, b, o0) in zip(items, slices, maps):
            res = _dot(mq, s[pp])
            s[pp] = res[:pw, :] + b
            o_ref[ix] = res[pw:, :] + o0
        for pp in range(n_pp):
            s_scr[pp] = s[pp]
        return carry

    lax.fori_loop(0, n_chunks // group, body, 0)

    @pl.when(t_idx == pl.num_programs(2) - 1)
    def _():
        for pp in range(n_pp):
            s = s_scr[pp]
            st_ref[pp, 0] = s[:HEAD_A, :HEAD_A]
            st_ref[pp, 1] = s[HEAD_A:, HEAD_A:]


def _wkv_scan(r, ld, k, v, kk, ka, s0, chunk, t_blk):
    bn, t, d = r.shape
    pw = 2 * HEAD_A
    n_pairs = d // pw
    n_chunks = t_blk // chunk
    n_pp = SCAN_PAIRS if n_chunks >= SCAN_GROUP else n_pairs
    seq = pl.BlockSpec((None, t_blk, n_pp * pw), lambda b, p, i: (b, i, p))
    st = pl.BlockSpec((None, n_pp, 2, HEAD_A, HEAD_A), lambda b, p, i: (b, p, 0, 0, 0))
    kern = functools.partial(_wkv_scan_kernel, chunk=chunk, n_chunks=n_chunks,
                             group=min(SCAN_GROUP, n_chunks))
    return pl.pallas_call(
        kern,
        grid=(bn, n_pairs // n_pp, t // t_blk),
        in_specs=[seq] * 6 + [st],
        out_specs=[seq, st],
        out_shape=[jax.ShapeDtypeStruct((bn, t, d), F32),
                   jax.ShapeDtypeStruct((bn, n_pairs, 2, HEAD_A, HEAD_A), F32)],
        scratch_shapes=[pltpu.VMEM((n_pp, pw, pw), F32)],
        compiler_params=_params(("parallel", "parallel", "arbitrary")),
        name="wkv_scan",
    )(r, ld, k, v, kk, ka, s0)


def _rwkv_post_kernel(o_ref, bonus_ref, g_ref, x_ref, vec_ref, th_ref, fh_ref, wo_ref, out_ref):
    to_head = th_ref[...]
    from_head = fh_ref[...]
    inv_n = 1.0 / HEAD_A
    tm = o_ref.shape[0]
    n_blk = POST_ROW_BLOCKS if tm % (POST_ROW_BLOCKS * SUBLANES) == 0 else 1
    blk = tm // n_blk
    sls = [pl.ds(b * blk, blk) for b in range(n_blk)]
    o = [o_ref[sl, :] for sl in sls]
    mean_h = [_dot_exact_rhs(a, to_head, 2) for a in o]
    mean = [_dot_exact_rhs(a, from_head, 3) * inv_n for a in mean_h]
    oc = [a - m for a, m in zip(o, mean)]
    var_h = [_dot_exact_rhs(a * a, to_head, 2) for a in oc]
    var = [_dot_exact_rhs(a, from_head, 3) * inv_n for a in var_h]
    y = [(c * lax.rsqrt(vr + GN_EPS) * vec_ref[0:1, :] + vec_ref[1:2, :] + bonus_ref[sl, :]) * g_ref[sl, :]
         for c, vr, sl in zip(oc, var, sls)]
    h = [_dot(a, wo_ref[...]) for a in y]
    for a, sl in zip(h, sls):
        out_ref[sl, :] = _layer_norm(ALPHA * x_ref[sl, :] + a, vec_ref[2:3, :], vec_ref[3:4, :])


def _rwkv_post(o, bonus, g, x, vec8, to_head, from_head, wo, tm):
    n, d = x.shape
    row = pl.BlockSpec((tm, d), lambda i: (i, 0))
    return pl.pallas_call(
        _rwkv_post_kernel,
        grid=(n // tm,),
        in_specs=[row] * 4 + [_const_spec(a.shape) for a in (vec8, to_head, from_head, wo)],
        out_specs=row,
        out_shape=jax.ShapeDtypeStruct((n, d), F32),
        compiler_params=_params(("parallel",)),
        name="rwkv_post",
    )(o, bonus, g, x, vec8, to_head, from_head, wo)


def _ffn_kernel(x_ref, wg_ref, wu_ref, wd_ref, ln_ref, out_ref, acc_ref, xb_ref):
    f = pl.program_id(1)

    @pl.when(f == 0)
    def _():
        acc_ref[...] = jnp.zeros_like(acc_ref)
        xb_ref[...] = x_ref[...].astype(BF16)

    xb = xb_ref[...]
    gate = jnp.dot(xb, wg_ref[...].astype(BF16), preferred_element_type=F32)
    up = jnp.dot(xb, wu_ref[...].astype(BF16), preferred_element_type=F32)
    h = gate * _sigmoid(gate) * up
    acc_ref[...] += jnp.dot(h.astype(BF16), wd_ref[...].astype(BF16), preferred_element_type=F32)

    @pl.when(f == pl.num_programs(1) - 1)
    def _():
        out_ref[...] = _layer_norm(ALPHA * x_ref[...] + acc_ref[...], ln_ref[0:1, :], ln_ref[1:2, :])


def _ffn(x, w_gu, w_down, ln8, tm, tf):
    n, d = x.shape
    d_ff = w_gu.shape[1] // 2
    nf = d_ff // tf
    return pl.pallas_call(
        _ffn_kernel,
        grid=(n // tm, nf),
        in_specs=[
            pl.BlockSpec((tm, d), lambda i, f: (i, 0)),
            pl.BlockSpec((d, tf), lambda i, f: (0, f)),
            pl.BlockSpec((d, tf), lambda i, f: (0, nf + f)),
            pl.BlockSpec((tf, d), lambda i, f: (f, 0)),
            _const_spec(ln8.shape),
        ],
        out_specs=pl.BlockSpec((tm, d), lambda i, f: (i, 0)),
        out_shape=jax.ShapeDtypeStruct((n, d), F32),
        scratch_shapes=[pltpu.VMEM((tm, d), F32), pltpu.VMEM((tm, d), BF16)],
        compiler_params=_params(("parallel", "arbitrary")),
        name="dense_ffn",
    )(x, w_gu, w_gu, w_down, ln8)


ROUTE_I1, ROUTE_I2, ROUTE_G1, ROUTE_G2, ROUTE_Q1, ROUTE_Q2 = range(6)
META_CARRY, META_COUNT = 0, 1


def _proj_ln_router_kernel(y_ref, x_ref, w_ref, ln_ref, rw_ref, out_ref, route_ref, meta_ref, count_ref,
                           carry_ref):
    tm = y_ref.shape[0]
    n_blk = POST_ROW_BLOCKS if tm % (POST_ROW_BLOCKS * SUBLANES) == 0 else 1
    blk = tm // n_blk
    sls = [pl.ds(b * blk, blk) for b in range(n_blk)]
    h = [_dot(y_ref[sl, :], w_ref[...]) for sl in sls]
    x3 = jnp.concatenate([_layer_norm(ALPHA * x_ref[sl, :] + a, ln_ref[0:1, :], ln_ref[1:2, :])
                          for a, sl in zip(h, sls)], axis=0)
    out_ref[...] = x3
    _route_rows(x3, rw_ref, route_ref, meta_ref, count_ref, carry_ref)


def _route_rows(x, w_ref, route_ref, meta_ref, count_ref, carry_ref):
    i = pl.program_id(0)

    @pl.when(i == 0)
    def _():
        carry_ref[...] = jnp.zeros_like(carry_ref)

    x_hi, x_lo = _split(x, 2)
    w_hi, w_lo = _split(w_ref[...], 2)
    logits = (jnp.dot(x_hi, w_hi, preferred_element_type=F32)
              + jnp.dot(x_lo, w_hi, preferred_element_type=F32)
              + jnp.dot(x_hi, w_lo, preferred_element_type=F32))
    tm = logits.shape[0]
    lane = lax.broadcasted_iota(jnp.int32, logits.shape, 1).astype(F32)
    neg = -jnp.inf
    logits = jnp.where(lane < N_EXPERTS, logits, neg)
    m1 = jnp.max(logits, axis=-1, keepdims=True)
    i1 = jnp.min(jnp.where(logits == m1, lane, float(LANES)), axis=-1, keepdims=True)
    rest = jnp.where(lane == i1, neg, logits)
    m2 = jnp.max(rest, axis=-1, keepdims=True)
    i2 = jnp.min(jnp.where(rest == m2, lane, float(LANES)), axis=-1, keepdims=True)
    e2 = jnp.exp(m2 - m1)
    den = 1.0 + e2

    sel1 = lane == i1
    sel2 = lane == i2
    onehot = jnp.where(sel1, 1.0, 0.0) + jnp.where(sel2, 1.0, 0.0)
    ri = lax.broadcasted_iota(jnp.int32, (tm, tm), 0)
    ci = lax.broadcasted_iota(jnp.int32, (tm, tm), 1)
    earlier = jnp.where(ri > ci, 1.0, 0.0).astype(BF16)
    in_tile = jnp.dot(earlier, onehot.astype(BF16), preferred_element_type=F32)
    q1 = jnp.sum(jnp.where(sel1, in_tile, 0.0), axis=-1, keepdims=True)
    q2 = jnp.sum(jnp.where(sel2, in_tile, 0.0), axis=-1, keepdims=True)
    tile_count = jnp.sum(onehot, axis=0, keepdims=True)

    route = jnp.zeros_like(logits)
    for col, val in ((ROUTE_I1, i1), (ROUTE_I2, i2), (ROUTE_G1, 1.0 / den), (ROUTE_G2, e2 / den),
                     (ROUTE_Q1, q1), (ROUTE_Q2, q2)):
        route = jnp.where(lane == float(col), val, route)
    route_ref[...] = route
    meta_row = lax.broadcasted_iota(jnp.int32, meta_ref.shape, 0)
    meta_ref[...] = jnp.where(meta_row == META_CARRY, carry_ref[0:1, :],
                              jnp.where(meta_row == META_COUNT, tile_count, 0.0))
    slab_rows = jnp.floor((tile_count + (SLAB_UNIT - 1)) * (1.0 / SLAB_UNIT)) * SLAB_UNIT
    carry_ref[0:1, :] = carry_ref[0:1, :] + slab_rows
    count_ref[...] = carry_ref[...]


def _proj_ln_router(y, x, w, ln8, router_pad, tm):
    n, d = x.shape
    row = pl.BlockSpec((tm, d), lambda i: (i, 0))
    return pl.pallas_call(
        _proj_ln_router_kernel,
        grid=(n // tm,),
        in_specs=[pl.BlockSpec((tm, y.shape[1]), lambda i: (i, 0)), row,
                  _const_spec(w.shape), _const_spec(ln8.shape), _const_spec(router_pad.shape)],
        out_specs=[row,
                   pl.BlockSpec((tm, LANES), lambda i: (i, 0)),
                   pl.BlockSpec((None, SUBLANES, LANES), lambda i: (i, 0, 0)),
                   _const_spec((SUBLANES, LANES))],
        out_shape=[jax.ShapeDtypeStruct((n, d), F32),
                   jax.ShapeDtypeStruct((n, LANES), F32),
                   jax.ShapeDtypeStruct((n // tm, SUBLANES, LANES), F32),
                   jax.ShapeDtypeStruct((SUBLANES, LANES), F32)],
        scratch_shapes=[pltpu.VMEM((SUBLANES, LANES), F32)],
        compiler_params=_params(("arbitrary",)),
        name="proj_ln_router",
    )(y, x, w, ln8, router_pad)


def _slab_plan(meta_ref):
    plan = []
    off = 0
    for e in range(N_EXPERTS):
        units = (meta_ref[0, e] + (SLAB_UNIT - 1)) // SLAB_UNIT
        plan.append((units, meta_ref[0, N_EXPERTS + e], off))
        off = off + units * SLAB_UNIT
    return plan


def _slab_buffer_rows(tm):
    worst = 2 * tm + N_EXPERTS * (SLAB_UNIT - 1)
    return ((worst + LANES - 1) // LANES) * LANES


def _slab_rows(expert, rank, plan):
    off = jnp.zeros_like(expert)
    for e, (_, _, e_off) in enumerate(plan):
        off = jnp.where(expert == e, e_off, off)
    return off + rank


def _slab_copies(plan, make_copy):
    total = 0
    for units, first_row, off in plan:
        def start(u, carry, first_row=first_row, off=off):
            make_copy(pl.multiple_of(off + u * SLAB_UNIT, SLAB_UNIT),
                      pl.multiple_of(first_row + u * SLAB_UNIT, SLAB_UNIT)).start()
            return carry
        lax.fori_loop(0, units, start, 0)
        total = total + units
    return total


def _dispatch_kernel(meta_ref, fill_ref, route_ref, x_ref, *rest):
    xs_ref, cbuf, zrows, sem = rest[-4:]
    tm = x_ref.shape[0]
    plan = _slab_plan(meta_ref)

    @pl.when(pl.program_id(0) == 0)
    def _():
        zrows[...] = jnp.zeros_like(zrows)

        def zero_copy(row, n_rows):
            return pltpu.make_async_copy(zrows.at[pl.ds(0, n_rows), :], xs_ref.at[pl.ds(row, n_rows), :],
                                         sem.at[1])

        def zero_range(first, pieces, n_rows):
            def start(u, carry):
                zero_copy(pl.multiple_of(first + u * n_rows, n_rows), n_rows).start()
                return carry

            def wait(u, carry):
                zero_copy(0, n_rows).wait()
                return carry

            lax.fori_loop(0, pieces, start, 0)
            lax.fori_loop(0, pieces, wait, 0)

        for e in range(N_EXPERTS):
            zero_range(fill_ref[0, e], fill_ref[0, N_EXPERTS + e], SLAB_UNIT)
        zero_range(fill_ref[0, 2 * N_EXPERTS], fill_ref[0, 2 * N_EXPERTS + 1], zrows.shape[0])

    sel = (lax.broadcasted_iota(jnp.int32, (SUBLANES, LANES), 0)
           == lax.broadcasted_iota(jnp.int32, (SUBLANES, LANES), 1)).astype(BF16)
    route_t = None
    for piece in _split(route_ref[...], 3):
        t = lax.dot_general(sel, piece, (((1,), (1,)), ((), ())), preferred_element_type=F32)
        route_t = t if route_t is None else route_t + t
    route_t = route_t.astype(jnp.int32)
    row1 = _slab_rows(route_t[ROUTE_I1:ROUTE_I1 + 1, :], route_t[ROUTE_Q1:ROUTE_Q1 + 1, :], plan)
    row2 = _slab_rows(route_t[ROUTE_I2:ROUTE_I2 + 1, :], route_t[ROUTE_Q2:ROUTE_Q2 + 1, :], plan)

    r_iota = lax.broadcasted_iota(jnp.int32, (cbuf.shape[0], tm), 0)
    perm = jnp.where((r_iota == row1) | (r_iota == row2), 1.0, 0.0).astype(BF16)
    cbuf[...] = jnp.dot(perm, x_ref[...].astype(BF16), preferred_element_type=F32)

    def make_copy(buf_row, sorted_row):
        return pltpu.make_async_copy(cbuf.at[pl.ds(buf_row, SLAB_UNIT), :],
                                     xs_ref.at[pl.ds(sorted_row, SLAB_UNIT), :], sem.at[0])

    n_started = _slab_copies(plan, make_copy)

    def wait(u, carry):
        make_copy(0, 0).wait()
        return carry

    lax.fori_loop(0, n_started, wait, 0)


def _dispatch(meta, fill, route, x, xs, xs_rows, tm):
    n, d = x.shape
    smem = lambda a, imap: pl.BlockSpec((None,) + a.shape[1:], imap, memory_space=pltpu.SMEM)
    in_specs = [smem(meta, lambda i: (i, 0, 0)), smem(fill, lambda i: (0, 0, 0)),
                pl.BlockSpec((tm, LANES), lambda i: (i, 0)),
                pl.BlockSpec((tm, d), lambda i: (i, 0))]
    args = [meta, fill, route, x]
    aliases = {}
    if xs is not None:
        in_specs.append(pl.BlockSpec(memory_space=pl.ANY))
        args.append(xs)
        aliases = {len(args) - 1: 0}
    return pl.pallas_call(
        _dispatch_kernel,
        grid=(n // tm,),
        in_specs=in_specs,
        out_specs=pl.BlockSpec(memory_space=pl.ANY),
        out_shape=jax.ShapeDtypeStruct((xs_rows, d), F32),
        scratch_shapes=[pltpu.VMEM((_slab_buffer_rows(tm), d), F32), pltpu.VMEM((ZERO_ROWS, d), F32),
                        pltpu.SemaphoreType.DMA((2,))],
        input_output_aliases=aliases,
        compiler_params=_params(("arbitrary",)),
        name="moe_dispatch",
    )(*args)


def _experts_kernel(te_ref, rows_ref, x_ref, wg_ref, wu_ref, wd_ref, out_ref, acc_ref, xb_ref):
    del te_ref
    f = pl.program_id(1)
    rows = rows_ref[pl.program_id(0)]
    tm = x_ref.shape[0]

    @pl.when(f == 0)
    def _():
        acc_ref[...] = jnp.zeros_like(acc_ref)

    def swiglu_rows(n_rows):
        sl = pl.ds(0, n_rows)

        @pl.when(f == 0)
        def _():
            xb_ref[sl, :] = x_ref[sl, :].astype(BF16)

        xb = xb_ref[sl, :]
        gate = jnp.dot(xb, wg_ref[...].astype(BF16), preferred_element_type=F32)
        up = jnp.dot(xb, wu_ref[...].astype(BF16), preferred_element_type=F32)
        h = gate * _sigmoid(gate) * up
        acc_ref[sl, :] += jnp.dot(h.astype(BF16), wd_ref[...].astype(BF16), preferred_element_type=F32)

    part = tm // MOE_TILE_PARTS
    for k in range(1, MOE_TILE_PARTS + 1):
        @pl.when(jnp.logical_and(rows > (k - 1) * part, rows <= k * part))
        def _(k=k):
            swiglu_rows(k * part)

    @pl.when(f == pl.num_programs(1) - 1)
    def _():
        out_ref[...] = acc_ref[...]


def _experts(tile_expert, tile_rows, xs, w_gu, w_down, tm, tf):
    s_total, d = xs.shape
    d_ff = w_gu.shape[2] // 2
    nf = d_ff // tf

    def f_eff(i, f, rows):
        return jnp.where(rows[i] > 0, f, nf - 1)

    grid_spec = pltpu.PrefetchScalarGridSpec(
        num_scalar_prefetch=2,
        grid=(s_total // tm, nf),
        in_specs=[
            pl.BlockSpec((tm, d), lambda i, f, te, rows: (jnp.where(rows[i] > 0, i, 0), 0)),
            pl.BlockSpec((None, d, tf), lambda i, f, te, rows: (te[i], 0, f_eff(i, f, rows))),
            pl.BlockSpec((None, d, tf), lambda i, f, te, rows: (te[i], 0, nf + f_eff(i, f, rows))),
            pl.BlockSpec((None, tf, d), lambda i, f, te, rows: (te[i], f_eff(i, f, rows), 0)),
        ],
        out_specs=pl.BlockSpec((tm, d), lambda i, f, te, rows: (i, 0)),
        scratch_shapes=[pltpu.VMEM((tm, d), F32), pltpu.VMEM((tm, d), BF16)],
    )
    return pl.pallas_call(
        _experts_kernel,
        grid_spec=grid_spec,
        out_shape=jax.ShapeDtypeStruct((s_total, d), F32),
        compiler_params=_params(("arbitrary", "arbitrary")),
        name="moe_experts",
    )(tile_expert, tile_rows, xs, w_gu, w_gu, w_down)


def _combine_kernel(meta_ref, meta_next_ref, route_ref, x_ref, ys_ref, ln_ref, out_ref, cbuf, sem):
    tm = x_ref.shape[0]
    i = pl.program_id(0)
    slot = i & 1
    plan = _slab_plan(meta_ref)

    def gather(tile_plan, dst_slot):
        def make_copy(buf_row, sorted_row):
            return pltpu.make_async_copy(ys_ref.at[pl.ds(sorted_row, SLAB_UNIT), :],
                                         cbuf.at[dst_slot, pl.ds(buf_row, SLAB_UNIT), :],
                                         sem.at[dst_slot])
        return make_copy, lambda: _slab_copies(tile_plan, make_copy)

    @pl.when(i == 0)
    def _():
        cbuf[...] = jnp.zeros_like(cbuf)
        gather(plan, 0)[1]()

    @pl.when(i + 1 < pl.num_programs(0))
    def _():
        gather(_slab_plan(meta_next_ref), 1 - slot)[1]()

    make_copy = gather(plan, slot)[0]
    n_mine = sum(units for units, _, _ in plan)

    def wait(u, carry):
        make_copy(0, 0).wait()
        return carry

    lax.fori_loop(0, n_mine, wait, 0)

    route = route_ref[...]
    col = lambda k: route[:, k:k + 1]
    row1 = _slab_rows(col(ROUTE_I1).astype(jnp.int32), col(ROUTE_Q1).astype(jnp.int32), plan)
    row2 = _slab_rows(col(ROUTE_I2).astype(jnp.int32), col(ROUTE_Q2).astype(jnp.int32), plan)
    c_iota = lax.broadcasted_iota(jnp.int32, (tm, cbuf.shape[1]), 1)
    gate = (jnp.where(c_iota == row1, col(ROUTE_G1), 0.0)
            + jnp.where(c_iota == row2, col(ROUTE_G2), 0.0))
    g_hi, g_lo = _split(gate, 2)
    c_hi, c_lo = _split(cbuf[slot], 2)
    y = (jnp.dot(g_hi, c_hi, preferred_element_type=F32)
         + jnp.dot(g_hi, c_lo, preferred_element_type=F32)
         + jnp.dot(g_lo, c_hi, preferred_element_type=F32))
    out_ref[...] = _layer_norm(ALPHA * x_ref[...] + y, ln_ref[0:1, :], ln_ref[1:2, :])


def _combine(meta, route, x, ys, ln8, tm):
    n, d = x.shape
    last = n // tm - 1
    meta_spec = lambda shift: pl.BlockSpec((None, 1, 2 * N_EXPERTS),
                                           lambda i: (jnp.minimum(i + shift, last), 0, 0),
                                           memory_space=pltpu.SMEM)
    return pl.pallas_call(
        _combine_kernel,
        grid=(n // tm,),
        in_specs=[meta_spec(0), meta_spec(1),
                  pl.BlockSpec((tm, LANES), lambda i: (i, 0)),
                  pl.BlockSpec((tm, d), lambda i: (i, 0)),
                  pl.BlockSpec(memory_space=pl.ANY),
                  _const_spec(ln8.shape)],
        out_specs=pl.BlockSpec((tm, d), lambda i: (i, 0)),
        out_shape=jax.ShapeDtypeStruct((n, d), F32),
        scratch_shapes=[pltpu.VMEM((2, _slab_buffer_rows(tm), d), F32), pltpu.SemaphoreType.DMA((2,))],
        compiler_params=_params(("arbitrary",)),
        name="moe_combine",
    )(meta, meta, route, x, ys, ln8)


def _rope(y, cos_t, sin_next, sin_prev):
    n = y.shape[1]
    reps = n // LANES
    tile = lambda t: jnp.concatenate([t] * reps, axis=1) if reps > 1 else t
    half = ROPE_DIM // 2
    return (y * tile(cos_t)
            + pltpu.roll(y, n - half, 1) * tile(sin_next)
            + pltpu.roll(y, half, 1) * tile(sin_prev))


def _qkv_kernel(x_ref, wq_ref, wkv_ref, cos_ref, sn_ref, sp_ref, q_out, k_out, v_out):
    xb = x_ref[...].astype(BF16)
    tables = (cos_ref[...], sn_ref[...], sp_ref[...])
    q_out[...] = _rope(jnp.dot(xb, wq_ref[...], preferred_element_type=F32), *tables)
    kv = jnp.dot(xb, wkv_ref[...], preferred_element_type=F32)
    nk = k_out.shape[1]
    k_out[...] = _rope(kv[:, :nk], *tables)
    v_out[...] = kv[:, nk:]


def _qkv_proj(x, wq, wkv, tables, tm):
    n, d = x.shape
    kvw = wkv.shape[1] // 2
    t_tiles = tables[0].shape[0] // tm
    tab = pl.BlockSpec((tm, LANES), lambda i: (i % t_tiles, 0))
    widths = (wq.shape[1], kvw, kvw)
    return pl.pallas_call(
        _qkv_kernel,
        grid=(n // tm,),
        in_specs=[pl.BlockSpec((tm, d), lambda i: (i, 0)), _const_spec(wq.shape), _const_spec(wkv.shape),
                  tab, tab, tab],
        out_specs=[pl.BlockSpec((tm, ow), lambda i: (i, 0)) for ow in widths],
        out_shape=[jax.ShapeDtypeStruct((n, ow), F32) for ow in widths],
        compiler_params=_params(("parallel",)),
        name="qkv_proj",
    )(x, wq, wkv, *tables)


def _attn_kernel(sink_ref, q_ref, kp_ref, kc_ref, vp_ref, vc_ref, o_ref, *, banded):
    tq = q_ref.shape[0]
    n_prev = kp_ref.shape[0]
    tk = n_prev + kc_ref.shape[0]
    q = q_ref[...] * ATTN_SCALE
    kband = jnp.concatenate([kp_ref[...], kc_ref[...]], axis=0)
    vband = jnp.concatenate([vp_ref[...], vc_ref[...]], axis=0)
    n_heads = q.shape[1] // HEAD_B
    group = n_heads // KV_HEADS
    qc = CHUNK if banded else tq
    kc = WINDOW + CHUNK if banded else tk
    units = [(kh, ci) for kh in range(KV_HEADS) for ci in range(tq // qc)]
    if banded:
        kj = lax.broadcasted_iota(jnp.int32, (kc, group * qc), 0)
        band_start = pl.program_id(1) * tq - n_prev

    qlane = lax.broadcasted_iota(jnp.int32, (1, group * qc), 1)

    def scores(kh, ci):
        qs = jnp.concatenate([q[ci * qc:(ci + 1) * qc, (kh * group + j) * HEAD_B:(kh * group + j + 1) * HEAD_B]
                              for j in range(group)], axis=0)
        return _dot_nt(kband[ci * qc:ci * qc + kc, kh * HEAD_B:(kh + 1) * HEAD_B], qs)

    def softmax(kh, ci, s):
        sk = jnp.full((1, group * qc), sink_ref[kh * group], F32)
        for j in range(1, group):
            sk = jnp.where(qlane >= j * qc, sink_ref[kh * group + j], sk)
        if banded:
            s = jnp.where(band_start + ci * qc + kj >= 0, s, -jnp.inf)
        m = jnp.maximum(jnp.max(s, axis=0, keepdims=True), sk)
        p = jnp.exp(s - m)
        return p * (1.0 / (jnp.sum(p, axis=0, keepdims=True) + jnp.exp(sk - m)))

    def weighted(kh, ci, p):
        return _dot_tn(p, vband[ci * qc:ci * qc + kc, kh * HEAD_B:(kh + 1) * HEAD_B])

    chunks = range(tq // qc)
    outs = {}
    s_next = [scores(0, ci) for ci in chunks]
    for kh in range(KV_HEADS):
        s_cur = s_next
        if kh + 1 < KV_HEADS:
            s_next = [scores(kh + 1, ci) for ci in chunks]
        probs = [softmax(kh, ci, s_cur[ci]) for ci in chunks]
        for ci in chunks:
            o = weighted(kh, ci, probs[ci])
            for j in range(group):
                outs[(kh * group + j, ci)] = o[j * qc:(j + 1) * qc, :]
    o_ref[...] = jnp.concatenate(
        [jnp.concatenate([outs[(h, ci)] for ci in range(tq // qc)], axis=0) for h in range(n_heads)],
        axis=1)


def _attention(q, k_prev_src, k_cur_src, v_prev_src, v_cur_src, sinks, tq, banded):
    bn, t, d = q.shape
    kw = k_cur_src.shape[2]
    if banded:
        ratio = tq // WINDOW
        prev_map = lambda b, i: (b, jnp.maximum(i * ratio - 1, 0), 0)
    else:
        prev_map = lambda b, i: (b, 0, 0)
    prev = pl.BlockSpec((None, WINDOW, kw), prev_map)
    cur = pl.BlockSpec((None, tq, kw), lambda b, i: (b, i, 0))
    kern = functools.partial(_attn_kernel, banded=banded)
    return pl.pallas_call(
        kern,
        grid=(bn, t // tq),
        in_specs=[pl.BlockSpec(memory_space=pltpu.SMEM),
                  pl.BlockSpec((None, tq, d), lambda b, i: (b, i, 0)), prev, cur, prev, cur],
        out_specs=pl.BlockSpec((None, tq, d), lambda b, i: (b, i, 0)),
        out_shape=jax.ShapeDtypeStruct((bn, t, d), F32),
        compiler_params=_params(("parallel", "parallel")),
        name="swa_attn",
    )(sinks, q, k_prev_src, k_cur_src, v_prev_src, v_cur_src)


def _moe_layer(groups, P):
    xs_rows = [g[0] for g in groups]
    routed = [g[1:] for g in groups]
    d = xs_rows[0].shape[1]
    tm_e = MOE_TILE
    counts = [c[0, :N_EXPERTS].astype(jnp.int32) for _, _, c in routed]
    total = sum(counts)
    padded = ((total + tm_e - 1) // tm_e) * tm_e
    ends = jnp.cumsum(padded)
    starts = ends - padded
    n_slabs = sum(-(-x.shape[0] // MOE_ROW_TILE) for x in xs_rows) * N_EXPERTS
    n_assign = 2 * sum(x.shape[0] for x in xs_rows) + n_slabs * (SLAB_UNIT - 1)
    n_tiles = (n_assign + N_EXPERTS * (tm_e - 1)) // tm_e
    n_used = (ends[-1] // tm_e).astype(jnp.int32)
    tile_expert = jnp.sum((jnp.arange(n_tiles) * tm_e)[:, None] >= ends[None, :], axis=1).astype(jnp.int32)
    tile_expert = jnp.minimum(tile_expert, N_EXPERTS - 1)
    tile_expert = jnp.where(jnp.arange(n_tiles) < n_used, tile_expert,
                            tile_expert[jnp.maximum(n_used - 1, 0)])
    tile_rows = jnp.clip((starts + total)[tile_expert] - jnp.arange(n_tiles) * tm_e, 0, tm_e)
    tile_rows = jnp.where(jnp.arange(n_tiles) < n_used, tile_rows, 0).astype(jnp.int32)

    metas = []
    base = starts
    for (_, meta, _), cnt in zip(routed, counts):
        carry = meta[:, META_CARRY, :N_EXPERTS].astype(jnp.int32)
        sent = meta[:, META_COUNT, :N_EXPERTS].astype(jnp.int32)
        metas.append(jnp.concatenate([sent, base[None, :] + carry], axis=1)[:, None, :])
        base = base + cnt

    first_end = starts + counts[0]
    tail = jnp.stack([ends[-1], (n_tiles * tm_e - ends[-1]) // ZERO_ROWS])
    fill = jnp.concatenate([first_end, (ends - first_end) // SLAB_UNIT, tail]).astype(jnp.int32)
    xs = None
    for m, (route, _, _), x in zip(metas, routed, xs_rows):
        xs = _dispatch(m, fill.reshape(1, 1, -1), route, x, xs, n_tiles * tm_e,
                       min(MOE_ROW_TILE, x.shape[0]))
        fill = jnp.zeros_like(fill)
    ys = _experts(tile_expert, tile_rows, xs, P['moe_w_gu'][0], P['moe_w_down'][0], tm_e, FFN_COLS)
    ln11 = _pad_rows([P['ln_g'][1, 1], P['ln_b'][1, 1]], d)
    return [_combine(m, route, x, ys, ln11, min(MOE_ROW_TILE, x.shape[0]))
            for m, (route, _, _), x in zip(metas, routed, xs_rows)]


def _pad_rows(rows, d):
    a = jnp.stack(rows).astype(F32)
    return jnp.concatenate([a, jnp.zeros((SUBLANES - a.shape[0], d), F32)], axis=0)


def _rope_tables(pos, reps):
    inv_freq = ROPE_THETA ** (-jnp.arange(0, ROPE_DIM, 2, dtype=jnp.float32) / ROPE_DIM)
    ang = pos.astype(jnp.float32)[:, None] * inv_freq[None, :]
    cos = jnp.cos(ang)
    sin = jnp.sin(ang)
    t = pos.shape[0]
    half = ROPE_DIM // 2
    rest = HEAD_B - ROPE_DIM
    z_half = jnp.zeros((t, half), F32)
    z_rest = jnp.zeros((t, rest), F32)
    cos_h = jnp.concatenate([cos, cos, jnp.ones((t, rest), F32)], axis=1)
    sn_h = jnp.concatenate([-sin, z_half, z_rest], axis=1)
    sp_h = jnp.concatenate([z_half, sin, z_rest], axis=1)
    per_tile = LANES // HEAD_B
    return tuple(jnp.tile(a, (reps, per_tile)) for a in (cos_h, sn_h, sp_h))


def _trunk(x, shift_in, wkv_in, k_cache, v_cache, pos0, P):
    bn, t, d = x.shape
    n = bn * t
    h_a = d // HEAD_A
    pw = 2 * HEAD_A
    n_pairs = d // pw
    xf = x.reshape(n, d)

    tm_pre = min(256, t)
    tm_row = min(256, n)
    tm_ffn = min(1024, n)
    chunk = min(CHUNK, t)
    t_blk = min(SCAN_BLOCK_FRAMES // SCAN_PAIRS, t)

    tiles = jnp.arange(n // tm_pre) * tm_pre
    prev_rows = xf[jnp.maximum(tiles - 1, 0)]
    start_rows = shift_in[0][tiles // t]
    bnd = jnp.where(((tiles % t) == 0)[:, None], start_rows, prev_rows)
    bnd = jnp.broadcast_to(bnd[:, None, :], (n // tm_pre, SUBLANES, d))

    mu8 = jnp.concatenate([P['a_mu'][0], jnp.zeros((2, d), F32)], axis=0)
    vec_pre = _pad_rows([P['a_w0'][0], P['a_a0'][0], P['a_k_k'][0], P['a_k_a'][0],
                         P['a_r_k'][0].reshape(d)], d)
    lane_head = jnp.arange(d) // HEAD_A
    to_head = (lane_head[:, None] == jnp.arange(LANES)[None, :]).astype(BF16)
    from_head = to_head.T
    bf = lambda a: a.astype(BF16)
    w_rkv = P['a_w_rkv'][0]
    r, ld, k_h, v, kk, ka, g, bonus = _rwkv_pre(
        xf, bnd, mu8, vec_pre, bf(w_rkv[0]), bf(w_rkv[1]), bf(w_rkv[2]),
        bf(P['a_w1'][0]), bf(P['a_w2'][0]), bf(P['a_a1'][0]), bf(P['a_a2'][0]),
        bf(P['a_g1'][0]), bf(P['a_g2'][0]), to_head, from_head, tm_pre)

    s0 = jnp.swapaxes(wkv_in[0].astype(F32), -1, -2).reshape(bn, n_pairs, 2, HEAD_A, HEAD_A)
    seq3 = lambda a: a.reshape(bn, t, d)
    o, s_fin = _wkv_scan(seq3(r), seq3(ld), seq3(k_h), seq3(v), seq3(kk), seq3(ka), s0, chunk, t_blk)
    wkv_out = jnp.swapaxes(s_fin, -1, -2).reshape(bn, h_a, HEAD_A, HEAD_A)
    shift_out = x[:, -1]

    vec_post = _pad_rows([P['a_lnx_g'][0], P['a_lnx_b'][0], P['ln_g'][0, 0], P['ln_b'][0, 0]], d)
    x1 = _rwkv_post(o.reshape(n, d), bonus, g, xf, vec_post, to_head, from_head, bf(P['a_w_o'][0]),
                    min(512, n))

    ln01 = _pad_rows([P['ln_g'][0, 1], P['ln_b'][0, 1]], d)
    x2 = _ffn(x1, P['ffn_w_gu'][0], P['ffn_w_down'][0], ln01, tm_ffn, FFN_COLS)

    pos = pos0 + jnp.arange(t, dtype=jnp.int32)
    tm_qkv = min(512, n)
    tables = _rope_tables(pos, max(tm_qkv // t, 1))
    kvw = KV_HEADS * HEAD_B
    q, k_new, v_new = _qkv_proj(x2, bf(P['b_w_q'][0]), bf(P['kv_w']), tables, tm_qkv)
    k_new = k_new.reshape(bn, t, kvw)
    v_new = v_new.reshape(bn, t, kvw)
    q = q.reshape(bn, t, d)
    sinks = P['b_sinks'][0].astype(F32)
    if k_cache is None:
        att = _attention(q, k_new, k_new, v_new, v_new, sinks, min(512, t), banded=True)
        k_out = k_new[:, -WINDOW:]
        v_out = v_new[:, -WINDOW:]
    else:
        kc = k_cache.astype(F32).reshape(bn, WINDOW, kvw)
        vc = v_cache.astype(F32).reshape(bn, WINDOW, kvw)
        att = _attention(q, kc, k_new, vc, v_new, sinks, t, banded=False)
        k_out = jnp.concatenate([kc, k_new], axis=1)[:, -WINDOW:]
        v_out = jnp.concatenate([vc, v_new], axis=1)[:, -WINDOW:]
    ln10 = _pad_rows([P['ln_g'][1, 0], P['ln_b'][1, 0]], d)
    router_pad = jnp.concatenate([P['moe_router'][0], jnp.zeros((d, LANES - N_EXPERTS), F32)], axis=1)
    routed = _proj_ln_router(att.reshape(n, d), x2, bf(P['b_w_o'][0]), ln10, router_pad,
                             min(MOE_ROW_TILE, n))

    return (routed, shift_out[None], wkv_out[None],
            k_out.reshape(bn, WINDOW, KV_HEADS, HEAD_B), v_out.reshape(bn, WINDOW, KV_HEADS, HEAD_B))


def kernel(x_prompt, x_sample, cache_shift_a, state_wkv_a, cache_k_b, cache_v_b, a_mu, a_w_rkv, a_w0, a_w1, a_w2, a_a0, a_a1, a_a2, a_g1, a_g2, a_k_k, a_k_a, a_r_k, a_lnx_g, a_lnx_b, a_w_o, kv_w, b_w_q, b_sinks, b_w_o, ln_g, ln_b, ffn_w_gu, ffn_w_down, moe_router, moe_w_gu, moe_w_down):
    P = {
        'a_mu': a_mu, 'a_w_rkv': a_w_rkv, 'a_w0': a_w0, 'a_w1': a_w1, 'a_w2': a_w2,
        'a_a0': a_a0, 'a_a1': a_a1, 'a_a2': a_a2, 'a_g1': a_g1, 'a_g2': a_g2,
        'a_k_k': a_k_k, 'a_k_a': a_k_a, 'a_r_k': a_r_k, 'a_lnx_g': a_lnx_g,
        'a_lnx_b': a_lnx_b, 'a_w_o': a_w_o, 'kv_w': kv_w, 'b_w_q': b_w_q,
        'b_sinks': b_sinks, 'b_w_o': b_w_o, 'ln_g': ln_g, 'ln_b': ln_b,
        'ffn_w_gu': ffn_w_gu, 'ffn_w_down': ffn_w_down, 'moe_router': moe_router,
        'moe_w_gu': moe_w_gu, 'moe_w_down': moe_w_down,
    }
    bp = x_prompt.shape[0]
    d = x_prompt.shape[2]
    h_a = d // HEAD_A
    zero_shift = jnp.zeros((1, bp, d), x_prompt.dtype)
    zero_wkv = jnp.zeros((1, bp, h_a, HEAD_A, HEAD_A), F32)
    x3_p, p_shift, p_wkv, p_k, p_v = _trunk(x_prompt, zero_shift, zero_wkv, None, None, 0, P)
    x3_s, s_shift, s_wkv, s_k, s_v = _trunk(x_sample, cache_shift_a, state_wkv_a,
                                            cache_k_b, cache_v_b, PAST_LEN, P)
    y_p, y_s = _moe_layer([x3_p, x3_s], P)
    return (y_p.reshape(x_prompt.shape), y_s.reshape(x_sample.shape),
            p_shift, p_wkv, p_k, p_v, s_shift, s_wkv, s_k, s_v)
```
